```python
import jax, jax.numpy as jnp
from jax import lax
import numpy as np

D_MODEL = 2048
BATCH = 16
SEQ = 2048
DEPTH = 2
DEC_BATCH = 16
DEC_SEQ = 16
PAST_LEN = 1024

CHUNK = 64
Q_BLOCK = 128
EPS = 1e-6
ROPE_BASE = 10000.0
MIX_A = D_MODEL // 4
MIX_B = D_MODEL // 4
MIX_C = D_MODEL // 2
D_MIX = MIX_A + MIX_B + MIX_C
LRU_BLOCKS = 8
LRU_BW = MIX_A // LRU_BLOCKS
CONV_W = 4
RG_C = 8.0
RET_DV = 128
RET_HEADS = MIX_B // RET_DV
RET_DK = RET_DV // 2
RET_QK = RET_HEADS * RET_DK
V_HEAD = 128
MLA_HEADS = MIX_C // V_HEAD
QK_NOPE = 128
QK_ROPE = 64
Q_LORA = D_MODEL // 4
KV_LORA = D_MODEL // 8
N_EXPERTS = 32
TOP_K = 4
D_EXPERT = D_MODEL
SWIGLU_LIMIT = 7.0
SWIGLU_ALPHA = 1.702
MOE_BLOCK = 256
IN_SPLITS = (MIX_A, MIX_A, RET_QK, RET_QK, MIX_B, MIX_B, Q_LORA, KV_LORA, QK_ROPE)
N_IN = sum(IN_SPLITS)

kernel_name = 'hybrid_stream_encoder_step'


def rms_norm(x, g):
    xf = x.astype(jnp.float32)
    y = xf * lax.rsqrt(jnp.mean(xf * xf, axis=-1, keepdims=True) + EPS)
    return (y * g.astype(jnp.float32)).astype(x.dtype)


def rope(x, pos):
    half = x.shape[-1] // 2
    inv = ROPE_BASE ** (-jnp.arange(half, dtype=jnp.float32) / half)
    ang = pos.astype(jnp.float32)[:, None] * inv[None, :]
    cos = jnp.cos(ang)[None, :, None, :]
    sin = jnp.sin(ang)[None, :, None, :]
    xf = x.astype(jnp.float32)
    x1, x2 = xf[..., :half], xf[..., half:]
    return jnp.concatenate([x1 * cos - x2 * sin, x1 * sin + x2 * cos], axis=-1).astype(x.dtype)


def _lin_combine(left, right):
    a_l, b_l = left
    a_r, b_r = right
    return a_l * a_r, a_r * b_l + b_r


def rg_lru_block(xa, ya, conv_buf, h0, p):
    B, L, W = xa.shape
    conv_in = jnp.concatenate([conv_buf.astype(xa.dtype), xa], axis=1)
    xc = lax.conv_general_dilated(conv_in, p['conv_w'][:, None, :].astype(xa.dtype), window_strides=(1,), padding='VALID', dimension_numbers=('NWC', 'WIO', 'NWC'), feature_group_count=W) + p['conv_b']
    xb = xc.reshape(B, L, LRU_BLOCKS, LRU_BW)
    r = jax.nn.sigmoid((jnp.einsum('blnd,nde->blne', xb, p['lru_w_a']).reshape(B, L, W) + p['lru_b_a']).astype(jnp.float32))
    i = jax.nn.sigmoid((jnp.einsum('blnd,nde->blne', xb, p['lru_w_x']).reshape(B, L, W) + p['lru_b_x']).astype(jnp.float32))
    log_a = -RG_C * r * jax.nn.softplus(-p['lru_lam'].astype(jnp.float32))
    a = jnp.exp(log_a)
    b = jnp.sqrt(-jnp.expm1(2.0 * log_a)) * i * xc.astype(jnp.float32)
    b = b.at[:, 0].add(a[:, 0] * h0.astype(jnp.float32))
    _, h = lax.associative_scan(_lin_combine, (a, b), axis=1)
    out = (h * jax.nn.gelu(ya.astype(jnp.float32))).astype(xa.dtype)
    return out, conv_in[:, -(CONV_W - 1):].astype(conv_buf.dtype), h[:, -1].astype(h0.dtype)


def retention(qr, kr, vr, gr, S0, pos):
    B, L, H, _ = qr.shape
    dt = vr.dtype
    q = rope(qr, pos).astype(jnp.float32)
    k = rope(kr, pos).astype(jnp.float32) * (RET_DK ** -0.5)
    v = vr.astype(jnp.float32)
    C = CHUNK if L % CHUNK == 0 else L
    nc = L // C
    log_g = jnp.log1p(-jnp.exp2(-5.0 - jnp.arange(H, dtype=jnp.float32)))
    idx = jnp.arange(C, dtype=jnp.float32)
    diff = idx[:, None] - idx[None, :]
    dmask = jnp.where(diff[None] >= 0, jnp.exp(diff[None] * log_g[:, None, None]), 0.0)
    qc = q.reshape(B, nc, C, H, RET_DK)
    kc = k.reshape(B, nc, C, H, RET_DK)
    vc = v.reshape(B, nc, C, H, RET_DV)
    s = jnp.einsum('bnihd,bnjhd->bnhij', qc, kc) * dmask
    intra = jnp.einsum('bnhij,bnjhe->bnihe', s, vc)
    kdec = jnp.exp((C - 1 - idx)[:, None] * log_g[None, :])
    chunk_kv = jnp.einsum('bnjhd,jh,bnjhe->nbhde', kc, kdec, vc)
    g_chunk = jnp.exp(C * log_g)[:, None, None]

    def step(S, kv):
        return S * g_chunk + kv, S

    S_last, S_prev = lax.scan(step, S0.astype(jnp.float32), chunk_kv)
    qdec = jnp.exp((idx + 1.0)[:, None] * log_g[None, :])
    inter = jnp.einsum('bnihd,ih,nbhde->bnihe', qc, qdec, S_prev)
    o = (intra + inter).reshape(B, L, H, RET_DV)
    o = o * lax.rsqrt(jnp.mean(o * o, axis=-1, keepdims=True) + EPS)
    out = (o.reshape(B, L, H * RET_DV) * jax.nn.silu(gr.astype(jnp.float32))).astype(dt)
    return out, S_last.astype(S0.dtype)


def mla_attention(q_nope, q_pe, k_nope, k_pe, v, q_pos, k_pos):
    B, L, H, _ = q_nope.shape
    qb = Q_BLOCK if L % Q_BLOCK == 0 else L
    nb = L // qb
    k_chunk = k_pos // CHUNK
    scale = (QK_NOPE + QK_ROPE) ** -0.5

    def block(args):
        qn, qp, qpos = args
        s = jnp.einsum('bqhd,bkhd->bhqk', qn, k_nope, preferred_element_type=jnp.float32)
        s = s + jnp.einsum('bqhd,bkd->bhqk', qp, k_pe, preferred_element_type=jnp.float32)
        mask = k_chunk[None, :] <= (qpos // CHUNK)[:, None]
        s = jnp.where(mask[None, None], s * scale, -1e30)
        prob = jax.nn.softmax(s, axis=-1).astype(v.dtype)
        return jnp.einsum('bhqk,bkhd->bqhd', prob, v)

    def split(t):
        return t.reshape(B, nb, qb, *t.shape[2:]).swapaxes(0, 1)

    out = lax.map(block, (split(q_nope), split(q_pe), q_pos.reshape(nb, qb)))
    return out.swapaxes(0, 1).reshape(B, L, H, V_HEAD)


def mla(q_lat, kv_lat, k_pe_raw, past_ckv, past_kpe, pos, p):
    B, L, _ = q_lat.shape
    cq = rms_norm(q_lat, p['g_q_norm'])
    q = jnp.einsum('blr,re->ble', cq, p['w_uq']).reshape(B, L, MLA_HEADS, QK_NOPE + QK_ROPE)
    q_nope = q[..., :QK_NOPE]
    q_pe = rope(q[..., QK_NOPE:], pos)
    ckv_new = rms_norm(kv_lat, p['g_kv_norm'])
    kpe_new = rope(k_pe_raw[:, :, None, :], pos)[:, :, 0]
    ckv = jnp.concatenate([past_ckv.astype(ckv_new.dtype), ckv_new], axis=1)
    kpe = jnp.concatenate([past_kpe.astype(kpe_new.dtype), kpe_new], axis=1)
    k_nope = jnp.einsum('bsr,rhd->bshd', ckv, p['w_uk'])
    v = jnp.einsum('bsr,rhd->bshd', ckv, p['w_uv'])
    k_pos = jnp.arange(ckv.shape[1], dtype=jnp.int32)
    o = mla_attention(q_nope, q_pe, k_nope, kpe, v, pos, k_pos)
    return o.reshape(B, L, MIX_C), ckv_new.astype(past_ckv.dtype), kpe_new.astype(past_kpe.dtype)


def expert_ffn(xb, w_gu, b_gu, w_dn, b_dn):
    gu = jnp.dot(xb, w_gu) + b_gu
    x_glu = jnp.minimum(gu[:, :D_EXPERT], SWIGLU_LIMIT)
    x_lin = jnp.clip(gu[:, D_EXPERT:], -SWIGLU_LIMIT, SWIGLU_LIMIT)
    act = x_glu * jax.nn.sigmoid(SWIGLU_ALPHA * x_glu) * (x_lin + 1.0)
    return jnp.dot(act, w_dn) + b_dn


def moe_ffn(h, p):
    n_tok, d = h.shape
    logits = jnp.dot(h, p['w_router'], preferred_element_type=jnp.float32) + p['b_router'].astype(jnp.float32)
    top_logit, top_idx = lax.top_k(logits, TOP_K)
    gate = jax.nn.softmax(top_logit, axis=-1)
    n_rows = n_tok * TOP_K
    blk = max(8, min(MOE_BLOCK, n_rows // N_EXPERTS))
    n_blocks = -(-n_rows // blk) + N_EXPERTS
    flat_e = top_idx.reshape(-1)
    order = jnp.argsort(flat_e)
    e_sorted = flat_e[order]
    counts = jnp.zeros((N_EXPERTS,), jnp.int32).at[flat_e].add(1)
    padded = (counts + blk - 1) // blk * blk
    pad_end = jnp.cumsum(padded)
    pad_start = pad_end - padded
    start = jnp.cumsum(counts) - counts
    dest = pad_start[e_sorted] + jnp.arange(n_rows, dtype=jnp.int32) - start[e_sorted]
    src_tok = order // TOP_K
    buf = jnp.zeros((n_blocks * blk, d), h.dtype).at[dest].set(h[src_tok])
    block_e = jnp.minimum(jnp.searchsorted(pad_end, jnp.arange(n_blocks, dtype=jnp.int32) * blk, side='right'), N_EXPERTS - 1)

    def run_block(args):
        xb, e = args
        return expert_ffn(xb, p['w_gate_up'][e], p['b_gate_up'][e], p['w_down'][e], p['b_down'][e])

    out_buf = lax.map(run_block, (buf.reshape(n_blocks, blk, d), block_e)).reshape(n_blocks * blk, d)
    rows = out_buf[dest] * gate.reshape(-1)[order][:, None].astype(out_buf.dtype)
    return jnp.zeros((n_tok, d), out_buf.dtype).at[src_tok].add(rows)


def trunk_layer(x, c, conv_buf, h0, S0, past_ckv, past_kpe, p):
    B, L, _ = x.shape
    pos = past_ckv.shape[1] + jnp.arange(L, dtype=jnp.int32)
    mod = jnp.dot(jax.nn.silu(c), p['w_ada']) + p['b_ada']
    sh_m, sc_m, gt_m, sh_f, sc_f, gt_f = jnp.split(mod[:, None, :], 6, axis=-1)
    h = rms_norm(x, p['g_mix_pre']) * (1.0 + sc_m) + sh_m
    u = jnp.einsum('bld,de->ble', h, p['w_in'])
    xa, ya, qr, kr, vr, gr, q_lat, kv_lat, k_pe = jnp.split(u, np.cumsum(IN_SPLITS)[:-1].tolist(), axis=-1)
    a_out, conv_new, h_new = rg_lru_block(xa, ya, conv_buf, h0, p)
    b_out, S_new = retention(qr.reshape(B, L, RET_HEADS, RET_DK), kr.reshape(B, L, RET_HEADS, RET_DK), vr.reshape(B, L, RET_HEADS, RET_DV), gr, S0, pos)
    c_out, ckv_new, kpe_new = mla(q_lat, kv_lat, k_pe, past_ckv, past_kpe, pos, p)
    mix = jnp.einsum('ble,ed->bld', jnp.concatenate([a_out, b_out, c_out], axis=-1), p['w_out'])
    x = x + gt_m * rms_norm(mix, p['g_mix_post'])
    h2 = rms_norm(x, p['g_ffn_pre']) * (1.0 + sc_f) + sh_f
    f = moe_ffn(h2.reshape(B * L, D_MODEL), p).reshape(B, L, D_MODEL)
    x = x + gt_f * rms_norm(f, p['g_ffn_post'])
    return x, (ckv_new, kpe_new, conv_new, h_new, S_new)


def run_stack(x, c, past, params):
    new_states = []
    for l in range(DEPTH):
        x, new = trunk_layer(x, c, *past[l], params[l])
        new_states.append(new)
    ckv, kpe, conv, lru, ret = [jnp.stack(t, axis=0) for t in zip(*new_states)]
    return x, ckv, kpe, conv, lru, ret


def setup_inputs(seed: int = 0) -> dict:
    key = jax.random.key(seed)
    ks = iter(jax.random.split(key, 48))
    f32 = jnp.float32

    def nrm(shape, scale):
        return scale * jax.random.normal(next(ks), shape, f32)

    def gain(shape):
        return 1.0 + nrm(shape, 0.02)

    u = jax.random.uniform(next(ks), (DEPTH, MIX_A), f32, 0.9, 0.999)
    a0 = u ** (1.0 / RG_C)
    lam = jnp.log(a0) - jnp.log1p(-a0)
    return {
        'x_prompt': nrm((BATCH, SEQ, D_MODEL), 1.0),
        'x_sample': nrm((DEC_BATCH, DEC_SEQ, D_MODEL), 1.0),
        'c_prompt': nrm((BATCH, D_MODEL), 1.0),
        'c_sample': nrm((DEC_BATCH, D_MODEL), 1.0),
        'cache_ckv': nrm((DEPTH, DEC_BATCH, PAST_LEN, KV_LORA), 1.0),
        'cache_kpe': nrm((DEPTH, DEC_BATCH, PAST_LEN, QK_ROPE), 1.0),
        'state_conv': nrm((DEPTH, DEC_BATCH, CONV_W - 1, MIX_A), 1.0),
        'state_lru': nrm((DEPTH, DEC_BATCH, MIX_A), 0.5),
        'state_ret': nrm((DEPTH, DEC_BATCH, RET_HEADS, RET_DK, RET_DV), 1.0),
        'w_ada': nrm((DEPTH, D_MODEL, 6 * D_MODEL), 0.5 * D_MODEL ** -0.5),
        'b_ada': nrm((DEPTH, 6 * D_MODEL), 0.01),
        'g_mix_pre': gain((DEPTH, D_MODEL)),
        'g_mix_post': gain((DEPTH, D_MODEL)),
        'g_ffn_pre': gain((DEPTH, D_MODEL)),
        'g_ffn_post': gain((DEPTH, D_MODEL)),
        'w_in': nrm((DEPTH, D_MODEL, N_IN), D_MODEL ** -0.5),
        'conv_w': nrm((DEPTH, CONV_W, MIX_A), CONV_W ** -0.5),
        'conv_b': nrm((DEPTH, MIX_A), 0.01),
        'lru_w_a': nrm((DEPTH, LRU_BLOCKS, LRU_BW, LRU_BW), LRU_BW ** -0.5),
        'lru_b_a': nrm((DEPTH, MIX_A), 0.1),
        'lru_w_x': nrm((DEPTH, LRU_BLOCKS, LRU_BW, LRU_BW), LRU_BW ** -0.5),
        'lru_b_x': nrm((DEPTH, MIX_A), 0.1),
        'lru_lam': lam,
        'g_q_norm': gain((DEPTH, Q_LORA)),
        'w_uq': nrm((DEPTH, Q_LORA, MLA_HEADS * (QK_NOPE + QK_ROPE)), Q_LORA ** -0.5),
        'g_kv_norm': gain((DEPTH, KV_LORA)),
        'w_uk': nrm((DEPTH, KV_LORA, MLA_HEADS, QK_NOPE), KV_LORA ** -0.5),
        'w_uv': nrm((DEPTH, KV_LORA, MLA_HEADS, V_HEAD), KV_LORA ** -0.5),
        'w_out': nrm((DEPTH, D_MIX, D_MODEL), D_MIX ** -0.5),
        'w_router': nrm((DEPTH, D_MODEL, N_EXPERTS), D_MODEL ** -0.5),
        'b_router': nrm((DEPTH, N_EXPERTS), 0.01),
        'w_gate_up': nrm((DEPTH, N_EXPERTS, D_MODEL, 2 * D_EXPERT), D_MODEL ** -0.5),
        'b_gate_up': nrm((DEPTH, N_EXPERTS, 2 * D_EXPERT), 0.01),
        'w_down': nrm((DEPTH, N_EXPERTS, D_EXPERT, D_MODEL), D_EXPERT ** -0.5),
        'b_down': nrm((DEPTH, N_EXPERTS, D_MODEL), 0.01),
    }


def reference(x_prompt, x_sample, c_prompt, c_sample, cache_ckv, cache_kpe, state_conv, state_lru, state_ret, w_ada, b_ada, g_mix_pre, g_mix_post, g_ffn_pre, g_ffn_post, w_in, conv_w, conv_b, lru_w_a, lru_b_a, lru_w_x, lru_b_x, lru_lam, g_q_norm, w_uq, g_kv_norm, w_uk, w_uv, w_out, w_router, b_router, w_gate_up, b_gate_up, w_down, b_down):
    params = [dict(w_ada=w_ada[l], b_ada=b_ada[l], g_mix_pre=g_mix_pre[l], g_mix_post=g_mix_post[l], g_ffn_pre=g_ffn_pre[l], g_ffn_post=g_ffn_post[l], w_in=w_in[l], conv_w=conv_w[l], conv_b=conv_b[l], lru_w_a=lru_w_a[l], lru_b_a=lru_b_a[l], lru_w_x=lru_w_x[l], lru_b_x=lru_b_x[l], lru_lam=lru_lam[l], g_q_norm=g_q_norm[l], w_uq=w_uq[l], g_kv_norm=g_kv_norm[l], w_uk=w_uk[l], w_uv=w_uv[l], w_out=w_out[l], w_router=w_router[l], b_router=b_router[l], w_gate_up=w_gate_up[l], b_gate_up=b_gate_up[l], w_down=w_down[l], b_down=b_down[l]) for l in range(DEPTH)]
    bp, dt = x_prompt.shape[0], x_prompt.dtype
    prompt_past = [(jnp.zeros((bp, CONV_W - 1, MIX_A), dt), jnp.zeros((bp, MIX_A), dt), jnp.zeros((bp, RET_HEADS, RET_DK, RET_DV), dt), jnp.zeros((bp, 0, KV_LORA), dt), jnp.zeros((bp, 0, QK_ROPE), dt)) for _ in range(DEPTH)]
    y_prompt, p_ckv, p_kpe, p_conv, p_lru, p_ret = run_stack(x_prompt, c_prompt, prompt_past, params)
    sample_past = [(state_conv[l], state_lru[l], state_ret[l], cache_ckv[l], cache_kpe[l]) for l in range(DEPTH)]
    y_sample, s_ckv, s_kpe, s_conv, s_lru, s_ret = run_stack(x_sample, c_sample, sample_past, params)
    return (y_prompt, y_sample, p_ckv, p_kpe, p_conv, p_lru, p_ret, s_ckv, s_kpe, s_conv, s_lru, s_ret)
```

```python
import functools

import numpy as np
import jax
import jax.numpy as jnp
from jax import lax
from jax.experimental import pallas as pl
from jax.experimental.pallas import tpu as pltpu

CHUNK = 64
EPS = 1e-6
ROPE_BASE = 10000.0
RG_C = 8.0
TOP_K = 4
SWIGLU_LIMIT = 7.0
SWIGLU_ALPHA = 1.702
NEG_BIG = -1e30
GELU_C = float(np.sqrt(2.0 / np.pi))

LANES = 128
VMEM_LIMIT_BYTES = 56 * 1024 * 1024

F32 = jnp.float32
BF16 = jnp.bfloat16


def _params(semantics):
    return pltpu.CompilerParams(dimension_semantics=semantics, vmem_limit_bytes=VMEM_LIMIT_BYTES)


def _tile(n, pref):
    if n <= pref:
        return n
    t = pref
    while n % t:
        t //= 2
    return t


def _rms(x, g):
    return x * lax.rsqrt(jnp.mean(x * x, axis=-1, keepdims=True) + EPS) * g


def _dot(a, b):
    return jnp.dot(a, b, preferred_element_type=F32)


def _dot_nt(a, b):
    return lax.dot_general(a, b, (((1,), (1,)), ((), ())), preferred_element_type=F32)


def _dot_tn(a, b):
    return lax.dot_general(a, b, (((0,), (0,)), ((), ())), preferred_element_type=F32)


def _sigmoid(x):
    return 1.0 / (1.0 + jnp.exp(-x))


def _ada_kernel(c_ref, w_ref, b_ref, o_ref):
    c = c_ref[...]
    s = (c * _sigmoid(c)).astype(BF16)
    o_ref[...] = _dot(s, w_ref[...].astype(BF16)) + b_ref[...]


def _ada_mod(c, w_ada, b_ada, layer):
    bsz, d = c.shape
    n = w_ada.shape[-1]
    tn = _tile(n, 1024)
    return pl.pallas_call(
        _ada_kernel,
        grid=(n // tn,),
        in_specs=[
            pl.BlockSpec((bsz, d), lambda j: (0, 0)),
            pl.BlockSpec((None, d, tn), lambda j: (layer, 0, j)),
            pl.BlockSpec((None, 1, tn), lambda j: (layer, 0, j)),
        ],
        out_specs=pl.BlockSpec((bsz, tn), lambda j: (0, j)),
        out_shape=jax.ShapeDtypeStruct((bsz, n), F32),
        compiler_params=_params(("arbitrary",)),
        name="ada_mod",
    )(c, w_ada, b_ada.reshape(b_ada.shape[0], 1, n))


def _in_proj_kernel(x_ref, mod_ref, g_ref, w_ref, u_ref, h_sc):
    @pl.when(pl.program_id(2) == 0)
    def _():
        mod = mod_ref[0]
        h = _rms(x_ref[0], g_ref[...]) * (1.0 + mod[1:2]) + mod[0:1]
        h_sc[...] = h.astype(BF16)

    u_ref[0] = _dot(h_sc[...], w_ref[...])


def _in_proj(x, mod, g, w_ext):
    bsz, seq, d = x.shape
    n = w_ext.shape[1]
    tm = _tile(seq, 512)
    tn = _tile(n, 1024)
    return pl.pallas_call(
        _in_proj_kernel,
        grid=(bsz, seq // tm, n // tn),
        in_specs=[
            pl.BlockSpec((1, tm, d), lambda b, i, j: (b, i, 0)),
            pl.BlockSpec((1, 6, d), lambda b, i, j: (b, 0, 0)),
            pl.BlockSpec((1, d), lambda b, i, j: (0, 0)),
            pl.BlockSpec((d, tn), lambda b, i, j: (0, j)),
        ],
        out_specs=pl.BlockSpec((1, tm, tn), lambda b, i, j: (b, i, j)),
        out_shape=jax.ShapeDtypeStruct((bsz, seq, n), F32),
        scratch_shapes=[pltpu.VMEM((tm, d), BF16)],
        compiler_params=_params(("parallel", "parallel", "arbitrary")),
        name="in_proj",
    )(x, mod, g, w_ext)


def _lru_kernel(xa_ref, ya_ref, cbuf_ref, h0_ref, cw_ref, cb_ref, wa_ref, wx_ref, ba_ref, bx_ref,
                lam_ref, out_ref, cnew_ref, hlast_ref, xbuf, hcar):
    tl = xa_ref.shape[1]
    width = xa_ref.shape[2]
    ctx = xbuf.shape[0] - tl

    @pl.when(pl.program_id(1) == 0)
    def _():
        xbuf[0:ctx, :] = cbuf_ref[0]
        hcar[...] = h0_ref[0]

    xa = xa_ref[0]
    xbuf[ctx:ctx + tl, :] = xa
    cw = cw_ref[...]
    n_tap = cw.shape[0]
    xc = cb_ref[...] + cw[n_tap - 1:n_tap] * xa
    for k in range(n_tap - 1):
        off = ctx - (n_tap - 1) + k
        xc = xc + cw[k:k + 1] * xbuf[off:off + tl, :]
    new_ctx = xbuf[tl:tl + ctx, :]
    xbuf[0:ctx, :] = new_ctx
    cnew_ref[0] = new_ctx

    xcb = xc.astype(BF16)
    r = _sigmoid(_dot(xcb, wa_ref[...]) + ba_ref[...])
    gi = _sigmoid(_dot(xcb, wx_ref[...]) + bx_ref[...])
    z = -lam_ref[...]
    softplus = jnp.maximum(z, 0.0) + jnp.log1p(jnp.exp(-jnp.abs(z)))
    log_a = (-RG_C) * r * softplus
    a = jnp.exp(log_a)
    b = jnp.sqrt(1.0 - jnp.exp(2.0 * log_a)) * gi * xc

    row = lax.broadcasted_iota(jnp.int32, (tl, width), 0)
    s = 1
    while s < tl:
        a_sh = pltpu.roll(a, s, 0)
        b_sh = pltpu.roll(b, s, 0)
        valid = row >= s
        b = jnp.where(valid, a * b_sh + b, b)
        a = jnp.where(valid, a * a_sh, a)
        s *= 2
    h = a * hcar[...] + b
    h_last = h[tl - 1:tl, :]
    hcar[...] = h_last
    hlast_ref[0] = h_last

    ya = ya_ref[0]
    gelu = 0.5 * ya * (1.0 + jnp.tanh(GELU_C * (ya + 0.044715 * (ya * ya * ya))))
    out_ref[0] = (h * gelu).astype(out_ref.dtype)


def _rg_lru(u, conv_ctx, h0, cw, cb, wa_bd, wx_bd, ba, bx, lam):
    bsz, seq, _ = u.shape
    width = cw.shape[1]
    tl = _tile(seq, 512)
    ctx = conv_ctx.shape[1]
    xa_blk = 0
    ya_blk = 1
    vec = lambda: pl.BlockSpec((1, width), lambda b, t: (0, 0))
    return pl.pallas_call(
        _lru_kernel,
        grid=(bsz, seq // tl),
        in_specs=[
            pl.BlockSpec((1, tl, width), lambda b, t: (b, t, xa_blk)),
            pl.BlockSpec((1, tl, width), lambda b, t: (b, t, ya_blk)),
            pl.BlockSpec((1, ctx, width), lambda b, t: (b, 0, 0)),
            pl.BlockSpec((1, 1, width), lambda b, t: (b, 0, 0)),
            pl.BlockSpec(cw.shape, lambda b, t: (0, 0)),
            vec(),
            pl.BlockSpec((width, width), lambda b, t: (0, 0)),
            pl.BlockSpec((width, width), lambda b, t: (0, 0)),
            vec(), vec(), vec(),
        ],
        out_specs=[
            pl.BlockSpec((1, tl, width), lambda b, t: (b, t, 0)),
            pl.BlockSpec((1, ctx, width), lambda b, t: (b, 0, 0)),
            pl.BlockSpec((1, 1, width), lambda b, t: (b, 0, 0)),
        ],
        out_shape=[
            jax.ShapeDtypeStruct((bsz, seq, width), BF16),
            jax.ShapeDtypeStruct((bsz, ctx, width), F32),
            jax.ShapeDtypeStruct((bsz, 1, width), F32),
        ],
        scratch_shapes=[pltpu.VMEM((tl + ctx, width), F32), pltpu.VMEM((1, width), F32)],
        compiler_params=_params(("parallel", "arbitrary")),
        name="rg_lru",
    )(u, u, conv_ctx, h0, cw, cb, wa_bd, wx_bd, ba, bx, lam)


def _ret_kernel(qk_ref, rot_ref, v_ref, g_ref, cos_ref, sin_ref, dmask_ref, qdec_ref, kdec_ref,
                s0_ref, out_ref, slast_ref, s_sc, *, gchunk):
    n_heads, dk, dv = s_sc.shape
    hdk = n_heads * dk

    @pl.when(pl.program_id(1) == 0)
    def _():
        s_sc[...] = s0_ref[0]

    qkr = qk_ref[0] * cos_ref[...] + rot_ref[0] * sin_ref[...]
    v = v_ref[0]
    vdec = v * kdec_ref[...]
    gate = g_ref[0]
    qdec = qdec_ref[...]
    for h in range(n_heads):
        q = qkr[:, h * dk:(h + 1) * dk].astype(BF16)
        k = qkr[:, hdk + h * dk:hdk + (h + 1) * dk].astype(BF16)
        vh = v[:, h * dv:(h + 1) * dv].astype(BF16)
        vd = vdec[:, h * dv:(h + 1) * dv].astype(BF16)
        s_prev = s_sc[h]
        scores = _dot_nt(q, k) * dmask_ref[h]
        o = _dot(scores.astype(BF16), vh) + qdec[:, h * dv:(h + 1) * dv] * _dot(q, s_prev.astype(BF16))
        s_sc[h] = s_prev * gchunk[h] + _dot_tn(k, vd)
        o = o * lax.rsqrt(jnp.mean(o * o, axis=-1, keepdims=True) + EPS)
        gh = gate[:, h * dv:(h + 1) * dv]
        out_ref[0, :, h * dv:(h + 1) * dv] = (o * (gh * _sigmoid(gh))).astype(out_ref.dtype)
    slast_ref[0] = s_sc[...]


def _retention(u, s0, cos_t, sin_t, blocks):
    bsz, seq, _ = u.shape
    _, n_heads, dk, dv = s0.shape
    width = n_heads * dv
    tc = _tile(seq, 256)
    log_g = np.log1p(-np.exp2(-5.0 - np.arange(n_heads, dtype=np.float64)))
    idx = np.arange(tc, dtype=np.float64)
    diff = idx[:, None] - idx[None, :]
    dmask = np.where(diff[None] >= 0, np.exp(np.maximum(diff, 0.0)[None] * log_g[:, None, None]), 0.0)
    qdec = np.repeat(np.exp((idx + 1.0)[:, None] * log_g[None, :]), dv, axis=1)
    kdec = np.repeat(np.exp((tc - 1.0 - idx)[:, None] * log_g[None, :]), dv, axis=1)
    gchunk = tuple(float(g) for g in np.exp(tc * log_g))
    qk_blk, v_blk, g_blk, rot_blk = blocks
    u_spec = lambda blk: pl.BlockSpec((1, tc, width), lambda b, n: (b, n, blk))
    full2 = lambda a: pl.BlockSpec(a.shape, lambda b, n: (0, 0))
    return pl.pallas_call(
        functools.partial(_ret_kernel, gchunk=gchunk),
        grid=(bsz, seq // tc),
        in_specs=[
            u_spec(qk_blk), u_spec(rot_blk), u_spec(v_blk), u_spec(g_blk),
            pl.BlockSpec((tc, width), lambda b, n: (n, 0)),
            pl.BlockSpec((tc, width), lambda b, n: (n, 0)),
            pl.BlockSpec(dmask.shape, lambda b, n: (0, 0, 0)),
            full2(qdec), full2(kdec),
            pl.BlockSpec((1, n_heads, dk, dv), lambda b, n: (b, 0, 0, 0)),
        ],
        out_specs=[
            pl.BlockSpec((1, tc, width), lambda b, n: (b, n, 0)),
            pl.BlockSpec((1, n_heads, dk, dv), lambda b, n: (b, 0, 0, 0)),
        ],
        out_shape=[
            jax.ShapeDtypeStruct((bsz, seq, width), BF16),
            jax.ShapeDtypeStruct((bsz, n_heads, dk, dv), F32),
        ],
        scratch_shapes=[pltpu.VMEM((n_heads, dk, dv), F32)],
        compiler_params=_params(("parallel", "arbitrary")),
        name="retention",
    )(u, u, u, u, cos_t, sin_t, jnp.asarray(dmask, F32), jnp.asarray(qdec, F32),
      jnp.asarray(kdec, F32), s0)


def _latent_kernel(kv_ref, g_ref, cos_ref, sin_ref, ckv_ref, kpe_ref, *, rope_w):
    lora = ckv_ref.shape[2]
    blk = kv_ref[0]
    ckv_ref[0] = _rms(blk[:, :lora], g_ref[...])
    kp = blk[:, lora:lora + LANES]
    rolled = pltpu.roll(kp, LANES - rope_w, 1)
    kpe_ref[0] = kp * cos_ref[...] + rolled * sin_ref[...]


def _latent(u, g_kv, cos_p, sin_p, kv_blk, blk_w, rope_w):
    bsz, seq, _ = u.shape
    lora = g_kv.shape[1]
    tm = _tile(seq, 512)
    return pl.pallas_call(
        functools.partial(_latent_kernel, rope_w=rope_w),
        grid=(bsz, seq // tm),
        in_specs=[
            pl.BlockSpec((1, tm, blk_w), lambda b, i: (b, i, kv_blk)),
            pl.BlockSpec((1, lora), lambda b, i: (0, 0)),
            pl.BlockSpec((tm, LANES), lambda b, i: (i, 0)),
            pl.BlockSpec((tm, LANES), lambda b, i: (i, 0)),
        ],
        out_specs=[
            pl.BlockSpec((1, tm, lora), lambda b, i: (b, i, 0)),
            pl.BlockSpec((1, tm, LANES), lambda b, i: (b, i, 0)),
        ],
        out_shape=[
            jax.ShapeDtypeStruct((bsz, seq, lora), F32),
            jax.ShapeDtypeStruct((bsz, seq, LANES), F32),
        ],
        compiler_params=_params(("parallel", "parallel")),
        name="latent_norm_rope",
    )(u, g_kv, cos_p, sin_p)


def _q_proj_kernel(ql_ref, g_ref, w_ref, cos_ref, sin_ref, q_ref, cq_sc, *, rope_w):
    @pl.when(pl.program_id(2) == 0)
    def _():
        cq_sc[...] = _rms(ql_ref[0], g_ref[...]).astype(BF16)

    r = _dot(cq_sc[...], w_ref[0])
    rolled = pltpu.roll(r, r.shape[1] - rope_w, 1)
    q_ref[0, 0] = (r * cos_ref[...] + rolled * sin_ref[...]).astype(q_ref.dtype)


def _q_proj(u, g_q, w_uq_ext, cos_q, sin_q, ql_blk, rope_w):
    bsz, seq, _ = u.shape
    n_heads, q_lora, width = w_uq_ext.shape
    tm = _tile(seq, 512)
    return pl.pallas_call(
        functools.partial(_q_proj_kernel, rope_w=rope_w),
        grid=(bsz, seq // tm, n_heads),
        in_specs=[
            pl.BlockSpec((1, tm, q_lora), lambda b, i, h: (b, i, ql_blk)),
            pl.BlockSpec((1, q_lora), lambda b, i, h: (0, 0)),
            pl.BlockSpec((1, q_lora, width), lambda b, i, h: (h, 0, 0)),
            pl.BlockSpec((tm, width), lambda b, i, h: (i, 0)),
            pl.BlockSpec((tm, width), lambda b, i, h: (i, 0)),
        ],
        out_specs=pl.BlockSpec((1, 1, tm, width), lambda b, i, h: (b, h, i, 0)),
        out_shape=jax.ShapeDtypeStruct((bsz, n_heads, seq, width), BF16),
        scratch_shapes=[pltpu.VMEM((tm, q_lora), BF16)],
        compiler_params=_params(("parallel", "parallel", "arbitrary")),
        name="mla_q_proj",
    )(u, g_q, w_uq_ext, cos_q, sin_q)


def _kv_proj_kernel(ckv_ref, kpe_ref, wk_ref, wv_ref, k_ref, v_ref):
    ckv = ckv_ref[0].astype(BF16)
    nope = wk_ref.shape[2]
    k_ref[0, 0, :, :nope] = _dot(ckv, wk_ref[0]).astype(k_ref.dtype)
    k_ref[0, 0, :, nope:] = kpe_ref[0].astype(k_ref.dtype)
    v_ref[0, 0] = _dot(ckv, wv_ref[0]).astype(v_ref.dtype)


def _kv_proj(ckv, kpe, w_uk, w_uv):
    bsz, klen, lora = ckv.shape
    n_heads, _, nope = w_uk.shape
    vdim = w_uv.shape[2]
    tm = _tile(klen, 512) if klen % 512 == 0 else klen
    return pl.pallas_call(
        _kv_proj_kernel,
        grid=(bsz, klen // tm, n_heads),
        in_specs=[
            pl.BlockSpec((1, tm, lora), lambda b, i, h: (b, i, 0)),
            pl.BlockSpec((1, tm, LANES), lambda b, i, h: (b, i, 0)),
            pl.BlockSpec((1, lora, nope), lambda b, i, h: (h, 0, 0)),
            pl.BlockSpec((1, lora, vdim), lambda b, i, h: (h, 0, 0)),
        ],
        out_specs=[
            pl.BlockSpec((1, 1, tm, nope + LANES), lambda b, i, h: (b, h, i, 0)),
            pl.BlockSpec((1, 1, tm, vdim), lambda b, i, h: (b, h, i, 0)),
        ],
        out_shape=[
            jax.ShapeDtypeStruct((bsz, n_heads, klen, nope + LANES), BF16),
            jax.ShapeDtypeStruct((bsz, n_heads, klen, vdim), BF16),
        ],
        compiler_params=_params(("parallel", "parallel", "arbitrary")),
        name="mla_kv_proj",
    )(ckv, kpe, w_uk, w_uv)


def _last_kv_block(qi, tq, tk, nk, q_pos0):
    last_q_chunk = (q_pos0 + qi * tq + tq - 1) // CHUNK
    return jnp.minimum((last_q_chunk * CHUNK + CHUNK - 1) // tk, nk - 1)


def _attn_kernel(q_ref, k_ref, v_ref, o_ref, m_sc, l_sc, acc_sc, *, q_pos0, scale):
    qi = pl.program_id(2)
    ki = pl.program_id(3)
    nk = pl.num_programs(3)
    tq = q_ref.shape[2]
    tk = k_ref.shape[2]

    @pl.when(ki == 0)
    def _():
        m_sc[...] = jnp.full(m_sc.shape, NEG_BIG, F32)
        l_sc[...] = jnp.zeros(l_sc.shape, F32)
        acc_sc[...] = jnp.zeros(acc_sc.shape, F32)

    @pl.when(ki <= _last_kv_block(qi, tq, tk, nk, q_pos0))
    def _():
        s = _dot_nt(q_ref[0, 0], k_ref[0, 0]) * scale
        q_chunk = (q_pos0 + qi * tq + lax.broadcasted_iota(jnp.int32, (tq, tk), 0)) // CHUNK
        k_chunk = (ki * tk + lax.broadcasted_iota(jnp.int32, (tq, tk), 1)) // CHUNK
        s = jnp.where(k_chunk <= q_chunk, s, NEG_BIG)
        m_prev = m_sc[...]
        m_new = jnp.maximum(m_prev, jnp.max(s, axis=-1, keepdims=True))
        alpha = jnp.exp(m_prev - m_new)
        p = jnp.exp(s - m_new)
        l_sc[...] = alpha * l_sc[...] + jnp.sum(p, axis=-1, keepdims=True)
        acc_sc[...] = alpha * acc_sc[...] + _dot(p.astype(BF16), v_ref[0, 0])
        m_sc[...] = m_new

    @pl.when(ki == nk - 1)
    def _():
        o_ref[0] = (acc_sc[...] / l_sc[...]).astype(o_ref.dtype)


def _attention(q, k, v, q_pos0, qk_dim):
    bsz, n_heads, seq, width = q.shape
    klen = k.shape[2]
    vdim = v.shape[3]
    tq = _tile(seq, 256)
    tk = _tile(klen, 256) if klen % 256 == 0 else klen
    nk = klen // tk
    kv_idx = lambda b, h, qi, ki: (b, h, jnp.minimum(ki, _last_kv_block(qi, tq, tk, nk, q_pos0)), 0)
    return pl.pallas_call(
        functools.partial(_attn_kernel, q_pos0=q_pos0, scale=float(qk_dim) ** -0.5),
        grid=(bsz, n_heads, seq // tq, nk),
        in_specs=[
            pl.BlockSpec((1, 1, tq, width), lambda b, h, qi, ki: (b, h, qi, 0)),
            pl.BlockSpec((1, 1, tk, width), kv_idx),
            pl.BlockSpec((1, 1, tk, vdim), kv_idx),
        ],
        out_specs=pl.BlockSpec((1, tq, vdim), lambda b, h, qi, ki: (b, qi, h)),
        out_shape=jax.ShapeDtypeStruct((bsz, seq, n_heads * vdim), BF16),
        scratch_shapes=[pltpu.VMEM((tq, 1), F32), pltpu.VMEM((tq, 1), F32), pltpu.VMEM((tq, vdim), F32)],
        compiler_params=_params(("parallel", "parallel", "parallel", "arbitrary")),
        name="mla_attention",
    )(q, k, v)


def _out_proj_kernel(a_ref, b_ref, c_ref, x_ref, mod_ref, gpost_ref, gpre_ref, wa_ref, wb_ref, wc_ref,
                     wrh_ref, wrl_ref, br_ref, x1_ref, h2_ref, gate_ref, idx_ref):
    mix = _dot(a_ref[0], wa_ref[...]) + _dot(b_ref[0], wb_ref[...]) + _dot(c_ref[0], wc_ref[...])
    mod = mod_ref[0]
    x1 = x_ref[0] + mod[2:3] * _rms(mix, gpost_ref[...])
    x1_ref[0] = x1
    h2 = _rms(x1, gpre_ref[...]) * (1.0 + mod[4:5]) + mod[3:4]
    h2_ref[0] = h2
    hi = h2.astype(BF16)
    lo = (h2 - hi.astype(F32)).astype(BF16)
    logits = _dot(hi, wrh_ref[...]) + (_dot(hi, wrl_ref[...]) + _dot(lo, wrh_ref[...])) + br_ref[...]
    lane = lax.broadcasted_iota(jnp.int32, logits.shape, 1)
    lane_f = lane.astype(F32)
    gates = jnp.zeros(logits.shape, F32)
    idxs = jnp.zeros(logits.shape, F32)
    top0 = None
    denom = None
    for k in range(TOP_K):
        m = jnp.max(logits, axis=-1, keepdims=True)
        sel = jnp.min(jnp.where(logits == m, lane_f, float(LANES)), axis=-1, keepdims=True)
        if k == 0:
            top0 = m
            e = jnp.ones_like(m)
            denom = e
        else:
            e = jnp.exp(m - top0)
            denom = denom + e
        gates = jnp.where(lane == k, e, gates)
        idxs = jnp.where(lane == k, sel, idxs)
        logits = jnp.where(lane_f == sel, NEG_BIG * 2.0, logits)
    gate_ref[0] = gates / denom
    idx_ref[0] = idxs.astype(jnp.int32)


def _out_proj(a_out, b_out, c_out, x, mod, g_post, g_pre, wa, wb, wc, wr_hi, wr_lo, br):
    bsz, seq, d = x.shape
    tm = _tile(seq, 256)
    row = lambda w: pl.BlockSpec((1, tm, w), lambda b, i: (b, i, 0))
    const = lambda a: pl.BlockSpec(a.shape, lambda b, i: (0, 0))
    return pl.pallas_call(
        _out_proj_kernel,
        grid=(bsz, seq // tm),
        in_specs=[
            row(a_out.shape[2]), row(b_out.shape[2]), row(c_out.shape[2]), row(d),
            pl.BlockSpec((1, 6, d), lambda b, i: (b, 0, 0)),
            const(g_post), const(g_pre), const(wa), const(wb), const(wc),
            const(wr_hi), const(wr_lo), const(br),
        ],
        out_specs=[row(d), row(d), row(LANES), row(LANES)],
        out_shape=[
            jax.ShapeDtypeStruct((bsz, seq, d), F32),
            jax.ShapeDtypeStruct((bsz, seq, d), F32),
            jax.ShapeDtypeStruct((bsz, seq, LANES), F32),
            jax.ShapeDtypeStruct((bsz, seq, LANES), jnp.int32),
        ],
        compiler_params=_params(("parallel", "parallel")),
        name="out_proj_router",
    )(a_out, b_out, c_out, x, mod, g_post, g_pre, wa, wb, wc, wr_hi, wr_lo, br)


def _moe_kernel(be_ref, nu_ref, x_ref, wg_ref, wu_ref, bg_ref, bu_ref, wd_ref, bd_ref, o_ref, acc_sc):
    m = pl.program_id(0)
    f = pl.program_id(1)
    nf = pl.num_programs(1)
    used = m < nu_ref[0]

    @pl.when(used)
    def _():
        x = x_ref[...]
        g = _dot(x, wg_ref[...].astype(BF16)) + bg_ref[...]
        u = _dot(x, wu_ref[...].astype(BF16)) + bu_ref[...]
        x_glu = jnp.minimum(g, SWIGLU_LIMIT)
        x_lin = jnp.clip(u, -SWIGLU_LIMIT, SWIGLU_LIMIT)
        act = x_glu * _sigmoid(SWIGLU_ALPHA * x_glu) * (x_lin + 1.0)
        part = _dot(act.astype(BF16), wd_ref[...].astype(BF16))

        @pl.when(f == 0)
        def _():
            acc_sc[...] = part + bd_ref[...]

        @pl.when(f > 0)
        def _():
            acc_sc[...] += part

    @pl.when(jnp.logical_and(f == nf - 1, used))
    def _():
        o_ref[...] = acc_sc[...].astype(o_ref.dtype)

    @pl.when(jnp.logical_and(f == nf - 1, jnp.logical_not(used)))
    def _():
        o_ref[...] = jnp.zeros(o_ref.shape, o_ref.dtype)


def _moe_ffn(x_sorted, block_e, n_used, w_gu, b_gu, w_dn, b_dn, layer, tm):
    rows, d = x_sorted.shape
    n_blocks = rows // tm
    d_exp = w_dn.shape[2]
    tf = _tile(d_exp, 256)
    nf = d_exp // tf
    n_exp = w_gu.shape[1]
    b_gu4 = b_gu.reshape(b_gu.shape[0], n_exp, 1, 2 * d_exp)
    b_dn4 = b_dn.reshape(b_dn.shape[0], n_exp, 1, d)

    def f_idx(m, f, nu):
        return jnp.where(m < nu[0], f, nf - 1)

    def m_idx(m, nu):
        return jnp.minimum(m, nu[0] - 1)

    grid_spec = pltpu.PrefetchScalarGridSpec(
        num_scalar_prefetch=2,
        grid=(n_blocks, nf),
        in_specs=[
            pl.BlockSpec((tm, d), lambda m, f, be, nu: (m_idx(m, nu), 0)),
            pl.BlockSpec((None, None, d, tf), lambda m, f, be, nu: (layer, be[m_idx(m, nu)], 0, f_idx(m, f, nu))),
            pl.BlockSpec((None, None, d, tf), lambda m, f, be, nu: (layer, be[m_idx(m, nu)], 0, nf + f_idx(m, f, nu))),
            pl.BlockSpec((None, None, 1, tf), lambda m, f, be, nu: (layer, be[m_idx(m, nu)], 0, f_idx(m, f, nu))),
            pl.BlockSpec((None, None, 1, tf), lambda m, f, be, nu: (layer, be[m_idx(m, nu)], 0, nf + f_idx(m, f, nu))),
            pl.BlockSpec((None, None, tf, d), lambda m, f, be, nu: (layer, be[m_idx(m, nu)], f_idx(m, f, nu), 0)),
            pl.BlockSpec((None, None, 1, d), lambda m, f, be, nu: (layer, be[m_idx(m, nu)], 0, 0)),
        ],
        out_specs=pl.BlockSpec((tm, d), lambda m, f, be, nu: (m, 0)),
        scratch_shapes=[pltpu.VMEM((tm, d), F32)],
    )
    return pl.pallas_call(
        _moe_kernel,
        grid_spec=grid_spec,
        out_shape=jax.ShapeDtypeStruct((rows, d), F32),
        compiler_params=_params(("arbitrary", "arbitrary")),
        name="moe_expert_ffn",
    )(block_e, n_used, x_sorted, w_gu, w_gu, b_gu4, b_gu4, w_dn, b_dn4)


def _route(top_idx, n_exp, tm, n_blocks):
    n_tok, top_k = top_idx.shape
    n_rows = n_tok * top_k
    flat_e = top_idx.reshape(-1)
    onehot = (flat_e[:, None] == jnp.arange(n_exp, dtype=jnp.int32)[None, :]).astype(jnp.int32)
    csum = jnp.cumsum(onehot, axis=0)
    rank = jnp.sum(csum * onehot, axis=1) - 1
    counts = csum[-1]
    padded = (counts + tm - 1) // tm * tm
    pad_end = jnp.cumsum(padded)
    pad_start = pad_end - padded
    dest = pad_start[flat_e] + rank
    n_used = (pad_end[-1] // tm).astype(jnp.int32).reshape(1)
    block_e = jnp.minimum(
        jnp.searchsorted(pad_end, jnp.arange(n_blocks, dtype=jnp.int32) * tm, side="right"), n_exp - 1
    ).astype(jnp.int32)
    src_tok = jnp.zeros((n_blocks * tm,), jnp.int32).at[dest].set(jnp.arange(n_rows, dtype=jnp.int32) // top_k)
    return dest, src_tok, block_e, n_used


def _ffn_post_kernel(x1_ref, f_ref, mod_ref, g_ref, o_ref):
    mod = mod_ref[0]
    o_ref[0] = x1_ref[0] + mod[5:6] * _rms(f_ref[0], g_ref[...])


def _ffn_post(x1, f, mod, g):
    bsz, seq, d = x1.shape
    tm = _tile(seq, 512)
    row = pl.BlockSpec((1, tm, d), lambda b, i: (b, i, 0))
    return pl.pallas_call(
        _ffn_post_kernel,
        grid=(bsz, seq // tm),
        in_specs=[row, row, pl.BlockSpec((1, 6, d), lambda b, i: (b, 0, 0)), pl.BlockSpec((1, d), lambda b, i: (0, 0))],
        out_specs=row,
        out_shape=jax.ShapeDtypeStruct((bsz, seq, d), F32),
        compiler_params=_params(("parallel", "parallel")),
        name="ffn_post",
    )(x1, f, mod, g)


def _rot_half_cols(w, head_dim):
    d = w.shape[0]
    w3 = w.reshape(d, -1, head_dim)
    half = head_dim // 2
    return jnp.concatenate([-w3[..., half:], w3[..., :half]], axis=-1).reshape(d, -1)


def _rope_tables(pos, head_dim):
    half = head_dim // 2
    inv = ROPE_BASE ** (-jnp.arange(half, dtype=F32) / half)
    ang = pos.astype(F32)[:, None] * inv[None, :]
    cos = jnp.cos(ang)
    sin = jnp.sin(ang)
    return jnp.concatenate([cos, cos], axis=1), jnp.concatenate([sin, sin], axis=1)


class _Dims:
    def __init__(self, state_ret, cache_ckv, cache_kpe, lru_w_a, w_uk, w_uv, conv_w, w_router, w_down):
        _, _, self.ret_heads, self.ret_dk, self.ret_dv = state_ret.shape
        self.kv_lora = cache_ckv.shape[-1]
        self.qk_rope = cache_kpe.shape[-1]
        self.mix_a = conv_w.shape[-1]
        self.conv_w = conv_w.shape[1]
        self.mla_heads, self.qk_nope = w_uk.shape[2], w_uk.shape[3]
        self.v_head = w_uv.shape[3]
        self.n_exp = w_router.shape[-1]
        self.mix_b = self.ret_heads * self.ret_dv
        self.ret_qk = self.ret_heads * self.ret_dk


def _layer_weights(l, dm, w_in, conv_w, conv_b, lru_w_a, lru_b_a, lru_w_x, lru_b_x, lru_lam, g_q_norm, w_uq,
                   g_kv_norm, w_uk, w_uv, w_out, w_router, b_router, g_mix_pre, g_mix_post, g_ffn_pre, g_ffn_post):
    d = w_in.shape[1]
    q_lora = g_q_norm.shape[1]
    sizes = (dm.mix_a, dm.mix_a, dm.ret_qk, dm.ret_qk, dm.mix_b, dm.mix_b, q_lora, dm.kv_lora, dm.qk_rope)
    offs = np.concatenate([[0], np.cumsum(sizes)])
    wl = w_in[l]
    seg = [wl[:, offs[i]:offs[i + 1]] for i in range(len(sizes))]
    w_xa, w_ya, w_qr, w_kr, w_vr, w_gr, w_ql, w_kvl, w_kpe = seg
    blk = dm.mix_b
    kv_cols = dm.kv_lora + 2 * dm.qk_rope
    kv_pad = (-kv_cols) % blk
    cols = [w_xa, w_ya, jnp.concatenate([w_qr, w_kr], axis=1), w_vr, w_gr, w_ql,
            jnp.concatenate([w_kvl, w_kpe, _rot_half_cols(w_kpe, dm.qk_rope), jnp.zeros((d, kv_pad), F32)], axis=1),
            jnp.concatenate([_rot_half_cols(w_qr, dm.ret_dk), _rot_half_cols(w_kr, dm.ret_dk)], axis=1)]
    assert all(c.shape[1] == blk for c in cols), [c.shape for c in cols]
    w_ext = jnp.concatenate(cols, axis=1).astype(BF16)

    n_lru = lru_w_a.shape[1]
    eye = jnp.eye(n_lru, dtype=F32)
    block_diag = lambda w: jnp.einsum("nde,nm->ndme", w, eye).reshape(dm.mix_a, dm.mix_a).astype(BF16)

    qk_dim = dm.qk_nope + dm.qk_rope
    wq3 = w_uq[l].reshape(q_lora, dm.mla_heads, qk_dim)
    pe = wq3[..., dm.qk_nope:]
    half = dm.qk_rope // 2
    pe_rot = jnp.concatenate([-pe[..., half:], pe[..., :half]], axis=-1)
    w_uq_ext = jnp.concatenate([wq3, pe_rot], axis=-1).transpose(1, 0, 2).astype(BF16)

    wo = w_out[l].astype(BF16)
    n_exp = dm.n_exp
    wr = jnp.concatenate([w_router[l], jnp.zeros((d, LANES - n_exp), F32)], axis=1)
    wr_hi = wr.astype(BF16)
    wr_lo = (wr - wr_hi.astype(F32)).astype(BF16)
    br = jnp.concatenate([b_router[l], jnp.full((LANES - n_exp,), NEG_BIG, F32)]).reshape(1, LANES)
    row = lambda v: v[l].reshape(1, -1)
    return dict(
        w_ext=w_ext, conv_w=conv_w[l], conv_b=row(conv_b),
        wa_bd=block_diag(lru_w_a[l]), wx_bd=block_diag(lru_w_x[l]),
        ba=row(lru_b_a), bx=row(lru_b_x), lam=row(lru_lam),
        g_q=row(g_q_norm), w_uq_ext=w_uq_ext, g_kv=row(g_kv_norm),
        w_uk=w_uk[l].transpose(1, 0, 2).astype(BF16), w_uv=w_uv[l].transpose(1, 0, 2).astype(BF16),
        wo_a=wo[:dm.mix_a], wo_b=wo[dm.mix_a:dm.mix_a + dm.mix_b], wo_c=wo[dm.mix_a + dm.mix_b:],
        wr_hi=wr_hi, wr_lo=wr_lo, br=br,
        g_mix_pre=row(g_mix_pre), g_mix_post=row(g_mix_post), g_ffn_pre=row(g_ffn_pre), g_ffn_post=row(g_ffn_post),
    )


def _mixer(x, mod, wl, dm, conv_buf, h0, s0, past_ckv, past_kpe):
    bsz, seq, d = x.shape
    past_len = 0 if past_ckv is None else past_ckv.shape[1]
    pos = past_len + jnp.arange(seq, dtype=jnp.int32)
    u = _in_proj(x, mod, wl["g_mix_pre"], wl["w_ext"])

    ctx_rows = 8
    n_ctx = dm.conv_w - 1
    conv_ctx = jnp.concatenate([jnp.zeros((bsz, ctx_rows - n_ctx, dm.mix_a), F32), conv_buf], axis=1)
    a_out, conv_new, h_new = _rg_lru(u, conv_ctx, h0.reshape(bsz, 1, dm.mix_a), wl["conv_w"], wl["conv_b"],
                                     wl["wa_bd"], wl["wx_bd"], wl["ba"], wl["bx"], wl["lam"])

    cos_k, sin_k = _rope_tables(pos, dm.ret_dk)
    k_scale = float(dm.ret_dk) ** -0.5
    scale_row = jnp.concatenate([jnp.ones((dm.ret_qk,), F32), jnp.full((dm.ret_qk,), k_scale, F32)])[None, :]
    cos_t = jnp.tile(cos_k, (1, 2 * dm.ret_heads)) * scale_row
    sin_t = jnp.tile(sin_k, (1, 2 * dm.ret_heads)) * scale_row
    b_out, s_new = _retention(u, s0, cos_t, sin_t, blocks=(2, 3, 4, 7))

    cos_r, sin_r = _rope_tables(pos, dm.qk_rope)
    lane_pad = lambda t, left, right, fill: jnp.concatenate(
        [jnp.full((seq, left), fill, F32), t, jnp.zeros((seq, right), F32)], axis=1)
    ckv_new, kpe_new = _latent(u, wl["g_kv"], lane_pad(cos_r, 0, LANES - dm.qk_rope, 0.0),
                               lane_pad(sin_r, 0, LANES - dm.qk_rope, 0.0), kv_blk=6, blk_w=dm.mix_b, rope_w=dm.qk_rope)
    q = _q_proj(u, wl["g_q"], wl["w_uq_ext"], lane_pad(cos_r, dm.qk_nope, dm.qk_rope, 1.0),
                lane_pad(sin_r, dm.qk_nope, dm.qk_rope, 0.0), ql_blk=5, rope_w=dm.qk_rope)
    if past_ckv is None:
        ckv_all, kpe_all = ckv_new, kpe_new
    else:
        past_kpe_pad = jnp.concatenate([past_kpe, jnp.zeros(past_kpe.shape[:2] + (LANES - dm.qk_rope,), F32)], axis=2)
        ckv_all = jnp.concatenate([past_ckv, ckv_new], axis=1)
        kpe_all = jnp.concatenate([past_kpe_pad, kpe_new], axis=1)
    k, v = _kv_proj(ckv_all, kpe_all, wl["w_uk"], wl["w_uv"])
    c_out = _attention(q, k, v, past_len, dm.qk_nope + dm.qk_rope)

    x1, h2, gates, idx = _out_proj(a_out, b_out, c_out, x, mod, wl["g_mix_post"], wl["g_ffn_pre"],
                                   wl["wo_a"], wl["wo_b"], wl["wo_c"], wl["wr_hi"], wl["wr_lo"], wl["br"])
    states = (ckv_new, kpe_new[:, :, :dm.qk_rope], conv_new[:, ctx_rows - n_ctx:], h_new.reshape(bsz, dm.mix_a), s_new)
    return x1, h2, gates[:, :, :TOP_K], idx[:, :, :TOP_K], states


def kernel(x_prompt, x_sample, c_prompt, c_sample, cache_ckv, cache_kpe, state_conv, state_lru, state_ret, w_ada, b_ada, g_mix_pre, g_mix_post, g_ffn_pre, g_ffn_post, w_in, conv_w, conv_b, lru_w_a, lru_b_a, lru_w_x, lru_b_x, lru_lam, g_q_norm, w_uq, g_kv_norm, w_uk, w_uv, w_out, w_router, b_router, w_gate_up, b_gate_up, w_down, b_down):
    depth = w_in.shape[0]
    dm = _Dims(state_ret, cache_ckv, cache_kpe, lru_w_a, w_uk, w_uv, conv_w, w_router, w_down)
    bp, lp, d = x_prompt.shape
    bs, ls, _ = x_sample.shape
    n_tok = bp * lp + bs * ls
    moe_tm = min(512, max(16, 1 << int(np.log2((n_tok * TOP_K) // dm.n_exp))))
    n_blocks = -(-(n_tok * TOP_K) // moe_tm) + dm.n_exp

    xp, xs = x_prompt, x_sample
    p_states, s_states = [], []
    for l in range(depth):
        wl = _layer_weights(l, dm, w_in, conv_w, conv_b, lru_w_a, lru_b_a, lru_w_x, lru_b_x, lru_lam, g_q_norm, w_uq,
                            g_kv_norm, w_uk, w_uv, w_out, w_router, b_router, g_mix_pre, g_mix_post, g_ffn_pre, g_ffn_post)
        mod_p = _ada_mod(c_prompt, w_ada, b_ada, l).reshape(bp, 6, d)
        mod_s = _ada_mod(c_sample, w_ada, b_ada, l).reshape(bs, 6, d)
        zeros = lambda *s: jnp.zeros(s, F32)
        x1p, h2p, gp, ip, st_p = _mixer(
            xp, mod_p, wl, dm, zeros(bp, dm.conv_w - 1, dm.mix_a), zeros(bp, dm.mix_a),
            zeros(bp, dm.ret_heads, dm.ret_dk, dm.ret_dv), None, None)
        x1s, h2s, gs, is_, st_s = _mixer(
            xs, mod_s, wl, dm, state_conv[l], state_lru[l], state_ret[l], cache_ckv[l], cache_kpe[l])
        p_states.append(st_p)
        s_states.append(st_s)

        h2 = jnp.concatenate([h2p.reshape(-1, d), h2s.reshape(-1, d)], axis=0)
        gates = jnp.concatenate([gp.reshape(-1, TOP_K), gs.reshape(-1, TOP_K)], axis=0)
        top_idx = jnp.concatenate([ip.reshape(-1, TOP_K), is_.reshape(-1, TOP_K)], axis=0)
        dest, src_tok, block_e, n_used = _route(top_idx, dm.n_exp, moe_tm, n_blocks)
        x_sorted = jnp.take(h2, src_tok, axis=0).astype(BF16)
        out_buf = _moe_ffn(x_sorted, block_e, n_used, w_gate_up, b_gate_up, w_down, b_down, l, moe_tm)
        rows = jnp.take(out_buf, dest, axis=0).reshape(n_tok, TOP_K, d)
        f = jnp.sum(rows * gates[:, :, None], axis=1)
        xp = _ffn_post(x1p, f[:bp * lp].reshape(bp, lp, d), mod_p, wl["g_ffn_post"])
        xs = _ffn_post(x1s, f[bp * lp:].reshape(bs, ls, d), mod_s, wl["g_ffn_post"])

    stack = lambda sts: tuple(jnp.stack(t, axis=0) for t in zip(*sts))
    p_ckv, p_kpe, p_conv, p_lru, p_ret = stack(p_states)
    s_ckv, s_kpe, s_conv, s_lru, s_ret = stack(s_states)
    return (xp, xs, p_ckv, p_kpe, p_conv, p_lru, p_ret, s_ckv, s_kpe, s_conv, s_lru, s_ret)
```

```python
import functools

import numpy as np
import jax
import jax.numpy as jnp
from jax import lax
from jax.experimental import pallas as pl
from jax.experimental.pallas import tpu as pltpu

CHUNK = 64
EPS = 1e-6
ROPE_BASE = 10000.0
RG_C = 8.0
TOP_K = 4
SWIGLU_LIMIT = 7.0
SWIGLU_ALPHA = 1.702
NEG_BIG = -1e30
GELU_C = float(np.sqrt(2.0 / np.pi))

LANES = 128
VMEM_LIMIT_BYTES = 56 * 1024 * 1024

F32 = jnp.float32
BF16 = jnp.bfloat16


def _params(semantics):
    return pltpu.CompilerParams(dimension_semantics=semantics, vmem_limit_bytes=VMEM_LIMIT_BYTES)


def _tile(n, pref):
    if n <= pref:
        return n
    t = pref
    while n % t:
        t //= 2
    return t


def _rms(x, g):
    return x * lax.rsqrt(jnp.mean(x * x, axis=-1, keepdims=True) + EPS) * g


def _dot(a, b):
    return jnp.dot(a, b, preferred_element_type=F32)


def _dot_nt(a, b):
    return lax.dot_general(a, b, (((1,), (1,)), ((), ())), preferred_element_type=F32)


def _dot_tn(a, b):
    return lax.dot_general(a, b, (((0,), (0,)), ((), ())), preferred_element_type=F32)


def _sigmoid(x):
    return 1.0 / (1.0 + jnp.exp(-x))


def _ada_kernel(c_ref, w_ref, b_ref, o_ref):
    c = c_ref[...]
    s = (c * _sigmoid(c)).astype(BF16)
    o_ref[...] = _dot(s, w_ref[...].astype(BF16)) + b_ref[...]


def _ada_mod(c, w_ada, b_ada, layer):
    bsz, d = c.shape
    n = w_ada.shape[-1]
    tn = _tile(n, 1024)
    return pl.pallas_call(
        _ada_kernel,
        grid=(n // tn,),
        in_specs=[
            pl.BlockSpec((bsz, d), lambda j: (0, 0)),
            pl.BlockSpec((None, d, tn), lambda j: (layer, 0, j)),
            pl.BlockSpec((None, 1, tn), lambda j: (layer, 0, j)),
        ],
        out_specs=pl.BlockSpec((bsz, tn), lambda j: (0, j)),
        out_shape=jax.ShapeDtypeStruct((bsz, n), F32),
        compiler_params=_params(("arbitrary",)),
        name="ada_mod",
    )(c, w_ada, b_ada.reshape(b_ada.shape[0], 1, n))


def _in_proj_kernel(x_ref, mod_ref, g_ref, w_ref, u_ref, h_sc):
    @pl.when(pl.program_id(2) == 0)
    def _():
        mod = mod_ref[0]
        h = _rms(x_ref[0], g_ref[...]) * (1.0 + mod[1:2]) + mod[0:1]
        h_sc[...] = h.astype(BF16)

    u_ref[0] = _dot(h_sc[...], w_ref[...])


def _in_proj(x, mod, g, w_ext):
    bsz, seq, d = x.shape
    n = w_ext.shape[1]
    tm = _tile(seq, 512)
    tn = _tile(n, 1024)
    return pl.pallas_call(
        _in_proj_kernel,
        grid=(bsz, seq // tm, n // tn),
        in_specs=[
            pl.BlockSpec((1, tm, d), lambda b, i, j: (b, i, 0)),
            pl.BlockSpec((1, 6, d), lambda b, i, j: (b, 0, 0)),
            pl.BlockSpec((1, d), lambda b, i, j: (0, 0)),
            pl.BlockSpec((d, tn), lambda b, i, j: (0, j)),
        ],
        out_specs=pl.BlockSpec((1, tm, tn), lambda b, i, j: (b, i, j)),
        out_shape=jax.ShapeDtypeStruct((bsz, seq, n), F32),
        scratch_shapes=[pltpu.VMEM((tm, d), BF16)],
        compiler_params=_params(("parallel", "parallel", "arbitrary")),
        name="in_proj",
    )(x, mod, g, w_ext)


def _lru_kernel(xa_ref, ya_ref, cbuf_ref, h0_ref, cw_ref, cb_ref, wa_ref, wx_ref, ba_ref, bx_ref,
                lam_ref, out_ref, cnew_ref, hlast_ref, xbuf, hcar):
    tl = xa_ref.shape[1]
    width = xa_ref.shape[2]
    ctx = xbuf.shape[0] - tl

    @pl.when(pl.program_id(1) == 0)
    def _():
        xbuf[0:ctx, :] = cbuf_ref[0]
        hcar[...] = h0_ref[0]

    xa = xa_ref[0]
    xbuf[ctx:ctx + tl, :] = xa
    cw = cw_ref[...]
    n_tap = cw.shape[0]
    xc = cb_ref[...] + cw[n_tap - 1:n_tap] * xa
    for k in range(n_tap - 1):
        off = ctx - (n_tap - 1) + k
        xc = xc + cw[k:k + 1] * xbuf[off:off + tl, :]
    new_ctx = xbuf[tl:tl + ctx, :]
    xbuf[0:ctx, :] = new_ctx
    cnew_ref[0] = new_ctx

    xcb = xc.astype(BF16)
    r = _sigmoid(_dot(xcb, wa_ref[...]) + ba_ref[...])
    gi = _sigmoid(_dot(xcb, wx_ref[...]) + bx_ref[...])
    z = -lam_ref[...]
    softplus = jnp.maximum(z, 0.0) + jnp.log1p(jnp.exp(-jnp.abs(z)))
    log_a = (-RG_C) * r * softplus
    a = jnp.exp(log_a)
    b = jnp.sqrt(1.0 - jnp.exp(2.0 * log_a)) * gi * xc

    row = lax.broadcasted_iota(jnp.int32, (tl, width), 0)
    s = 1
    while s < tl:
        a_sh = pltpu.roll(a, s, 0)
        b_sh = pltpu.roll(b, s, 0)
        valid = row >= s
        b = jnp.where(valid, a * b_sh + b, b)
        a = jnp.where(valid, a * a_sh, a)
        s *= 2
    h = a * hcar[...] + b
    h_last = h[tl - 1:tl, :]
    hcar[...] = h_last
    hlast_ref[0] = h_last

    ya = ya_ref[0]
    gelu = 0.5 * ya * (1.0 + jnp.tanh(GELU_C * (ya + 0.044715 * (ya * ya * ya))))
    out_ref[0] = (h * gelu).astype(out_ref.dtype)


def _rg_lru(u, conv_ctx, h0, cw, cb, wa_bd, wx_bd, ba, bx, lam):
    bsz, seq, _ = u.shape
    width = cw.shape[1]
    tl = _tile(seq, 512)
    ctx = conv_ctx.shape[1]
    xa_blk = 0
    ya_blk = 1
    vec = lambda: pl.BlockSpec((1, width), lambda b, t: (0, 0))
    return pl.pallas_call(
        _lru_kernel,
        grid=(bsz, seq // tl),
        in_specs=[
            pl.BlockSpec((1, tl, width), lambda b, t: (b, t, xa_blk)),
            pl.BlockSpec((1, tl, width), lambda b, t: (b, t, ya_blk)),
            pl.BlockSpec((1, ctx, width), lambda b, t: (b, 0, 0)),
            pl.BlockSpec((1, 1, width), lambda b, t: (b, 0, 0)),
            pl.BlockSpec(cw.shape, lambda b, t: (0, 0)),
            vec(),
            pl.BlockSpec((width, width), lambda b, t: (0, 0)),
            pl.BlockSpec((width, width), lambda b, t: (0, 0)),
            vec(), vec(), vec(),
        ],
        out_specs=[
            pl.BlockSpec((1, tl, width), lambda b, t: (b, t, 0)),
            pl.BlockSpec((1, ctx, width), lambda b, t: (b, 0, 0)),
            pl.BlockSpec((1, 1, width), lambda b, t: (b, 0, 0)),
        ],
        out_shape=[
            jax.ShapeDtypeStruct((bsz, seq, width), BF16),
            jax.ShapeDtypeStruct((bsz, ctx, width), F32),
            jax.ShapeDtypeStruct((bsz, 1, width), F32),
        ],
        scratch_shapes=[pltpu.VMEM((tl + ctx, width), F32), pltpu.VMEM((1, width), F32)],
        compiler_params=_params(("parallel", "arbitrary")),
        name="rg_lru",
    )(u, u, conv_ctx, h0, cw, cb, wa_bd, wx_bd, ba, bx, lam)


def _ret_kernel(qk_ref, rot_ref, v_ref, g_ref, cos_ref, sin_ref, dmask_ref, qdec_ref, kdec_ref,
                s0_ref, out_ref, slast_ref, s_sc, *, gchunk):
    n_heads, dk, dv = s_sc.shape
    hdk = n_heads * dk

    @pl.when(pl.program_id(1) == 0)
    def _():
        s_sc[...] = s0_ref[0]

    qkr = qk_ref[0] * cos_ref[...] + rot_ref[0] * sin_ref[...]
    v = v_ref[0]
    vdec = v * kdec_ref[...]
    gate = g_ref[0]
    qdec = qdec_ref[...]
    for h in range(n_heads):
        q = qkr[:, h * dk:(h + 1) * dk].astype(BF16)
        k = qkr[:, hdk + h * dk:hdk + (h + 1) * dk].astype(BF16)
        vh = v[:, h * dv:(h + 1) * dv].astype(BF16)
        vd = vdec[:, h * dv:(h + 1) * dv].astype(BF16)
        s_prev = s_sc[h]
        scores = _dot_nt(q, k) * dmask_ref[h]
        o = _dot(scores.astype(BF16), vh) + qdec[:, h * dv:(h + 1) * dv] * _dot(q, s_prev.astype(BF16))
        s_sc[h] = s_prev * gchunk[h] + _dot_tn(k, vd)
        o = o * lax.rsqrt(jnp.mean(o * o, axis=-1, keepdims=True) + EPS)
        gh = gate[:, h * dv:(h + 1) * dv]
        out_ref[0, :, h * dv:(h + 1) * dv] = (o * (gh * _sigmoid(gh))).astype(out_ref.dtype)
    slast_ref[0] = s_sc[...]


def _retention(u, s0, cos_t, sin_t, blocks):
    bsz, seq, _ = u.shape
    _, n_heads, dk, dv = s0.shape
    width = n_heads * dv
    tc = _tile(seq, 256)
    log_g = np.log1p(-np.exp2(-5.0 - np.arange(n_heads, dtype=np.float64)))
    idx = np.arange(tc, dtype=np.float64)
    diff = idx[:, None] - idx[None, :]
    dmask = np.where(diff[None] >= 0, np.exp(np.maximum(diff, 0.0)[None] * log_g[:, None, None]), 0.0)
    qdec = np.repeat(np.exp((idx + 1.0)[:, None] * log_g[None, :]), dv, axis=1)
    kdec = np.repeat(np.exp((tc - 1.0 - idx)[:, None] * log_g[None, :]), dv, axis=1)
    gchunk = tuple(float(g) for g in np.exp(tc * log_g))
    qk_blk, v_blk, g_blk, rot_blk = blocks
    u_spec = lambda blk: pl.BlockSpec((1, tc, width), lambda b, n: (b, n, blk))
    full2 = lambda a: pl.BlockSpec(a.shape, lambda b, n: (0, 0))
    return pl.pallas_call(
        functools.partial(_ret_kernel, gchunk=gchunk),
        grid=(bsz, seq // tc),
        in_specs=[
            u_spec(qk_blk), u_spec(rot_blk), u_spec(v_blk), u_spec(g_blk),
            pl.BlockSpec((tc, width), lambda b, n: (n, 0)),
            pl.BlockSpec((tc, width), lambda b, n: (n, 0)),
            pl.BlockSpec(dmask.shape, lambda b, n: (0, 0, 0)),
            full2(qdec), full2(kdec),
            pl.BlockSpec((1, n_heads, dk, dv), lambda b, n: (b, 0, 0, 0)),
        ],
        out_specs=[
            pl.BlockSpec((1, tc, width), lambda b, n: (b, n, 0)),
            pl.BlockSpec((1, n_heads, dk, dv), lambda b, n: (b, 0, 0, 0)),
        ],
        out_shape=[
            jax.ShapeDtypeStruct((bsz, seq, width), BF16),
            jax.ShapeDtypeStruct((bsz, n_heads, dk, dv), F32),
        ],
        scratch_shapes=[pltpu.VMEM((n_heads, dk, dv), F32)],
        compiler_params=_params(("parallel", "arbitrary")),
        name="retention",
    )(u, u, u, u, cos_t, sin_t, jnp.asarray(dmask, F32), jnp.asarray(qdec, F32),
      jnp.asarray(kdec, F32), s0)


def _latent_kernel(kv_ref, g_ref, cos_ref, sin_ref, ckv_ref, kpe_ref, *, rope_w):
    lora = ckv_ref.shape[2]
    blk = kv_ref[0]
    ckv_ref[0] = _rms(blk[:, :lora], g_ref[...])
    kp = blk[:, lora:lora + LANES]
    rolled = pltpu.roll(kp, LANES - rope_w, 1)
    kpe_ref[0] = kp * cos_ref[...] + rolled * sin_ref[...]


def _latent(u, g_kv, cos_p, sin_p, kv_blk, blk_w, rope_w):
    bsz, seq, _ = u.shape
    lora = g_kv.shape[1]
    tm = _tile(seq, 512)
    return pl.pallas_call(
        functools.partial(_latent_kernel, rope_w=rope_w),
        grid=(bsz, seq // tm),
        in_specs=[
            pl.BlockSpec((1, tm, blk_w), lambda b, i: (b, i, kv_blk)),
            pl.BlockSpec((1, lora), lambda b, i: (0, 0)),
            pl.BlockSpec((tm, LANES), lambda b, i: (i, 0)),
            pl.BlockSpec((tm, LANES), lambda b, i: (i, 0)),
        ],
        out_specs=[
            pl.BlockSpec((1, tm, lora), lambda b, i: (b, i, 0)),
            pl.BlockSpec((1, tm, LANES), lambda b, i: (b, i, 0)),
        ],
        out_shape=[
            jax.ShapeDtypeStruct((bsz, seq, lora), F32),
            jax.ShapeDtypeStruct((bsz, seq, LANES), F32),
        ],
        compiler_params=_params(("parallel", "parallel")),
        name="latent_norm_rope",
    )(u, g_kv, cos_p, sin_p)


def _q_proj_kernel(ql_ref, g_ref, w_ref, cos_ref, sin_ref, q_ref, cq_sc, *, rope_w):
    @pl.when(pl.program_id(2) == 0)
    def _():
        cq_sc[...] = _rms(ql_ref[0], g_ref[...]).astype(BF16)

    r = _dot(cq_sc[...], w_ref[0])
    rolled = pltpu.roll(r, r.shape[1] - rope_w, 1)
    q_ref[0, 0] = (r * cos_ref[...] + rolled * sin_ref[...]).astype(q_ref.dtype)


def _q_proj(u, g_q, w_uq_ext, cos_q, sin_q, ql_blk, rope_w):
    bsz, seq, _ = u.shape
    n_heads, q_lora, width = w_uq_ext.shape
    tm = _tile(seq, 512)
    return pl.pallas_call(
        functools.partial(_q_proj_kernel, rope_w=rope_w),
        grid=(bsz, seq // tm, n_heads),
        in_specs=[
            pl.BlockSpec((1, tm, q_lora), lambda b, i, h: (b, i, ql_blk)),
            pl.BlockSpec((1, q_lora), lambda b, i, h: (0, 0)),
            pl.BlockSpec((1, q_lora, width), lambda b, i, h: (h, 0, 0)),
            pl.BlockSpec((tm, width), lambda b, i, h: (i, 0)),
            pl.BlockSpec((tm, width), lambda b, i, h: (i, 0)),
        ],
        out_specs=pl.BlockSpec((1, 1, tm, width), lambda b, i, h: (b, h, i, 0)),
        out_shape=jax.ShapeDtypeStruct((bsz, n_heads, seq, width), BF16),
        scratch_shapes=[pltpu.VMEM((tm, q_lora), BF16)],
        compiler_params=_params(("parallel", "parallel", "arbitrary")),
        name="mla_q_proj",
    )(u, g_q, w_uq_ext, cos_q, sin_q)


def _kv_proj_kernel(ckv_ref, kpe_ref, wk_ref, wv_ref, k_ref, v_ref):
    ckv = ckv_ref[0].astype(BF16)
    nope = wk_ref.shape[2]
    k_ref[0, 0, :, :nope] = _dot(ckv, wk_ref[0]).astype(k_ref.dtype)
    k_ref[0, 0, :, nope:] = kpe_ref[0].astype(k_ref.dtype)
    v_ref[0, 0] = _dot(ckv, wv_ref[0]).astype(v_ref.dtype)


def _kv_proj(ckv, kpe, w_uk, w_uv):
    bsz, klen, lora = ckv.shape
    n_heads, _, nope = w_uk.shape
    vdim = w_uv.shape[2]
    tm = _tile(klen, 512) if klen % 512 == 0 else klen
    return pl.pallas_call(
        _kv_proj_kernel,
        grid=(bsz, klen // tm, n_heads),
        in_specs=[
            pl.BlockSpec((1, tm, lora), lambda b, i, h: (b, i, 0)),
            pl.BlockSpec((1, tm, LANES), lambda b, i, h: (b, i, 0)),
            pl.BlockSpec((1, lora, nope), lambda b, i, h: (h, 0, 0)),
            pl.BlockSpec((1, lora, vdim), lambda b, i, h: (h, 0, 0)),
        ],
        out_specs=[
            pl.BlockSpec((1, 1, tm, nope + LANES), lambda b, i, h: (b, h, i, 0)),
            pl.BlockSpec((1, 1, tm, vdim), lambda b, i, h: (b, h, i, 0)),
        ],
        out_shape=[
            jax.ShapeDtypeStruct((bsz, n_heads, klen, nope + LANES), BF16),
            jax.ShapeDtypeStruct((bsz, n_heads, klen, vdim), BF16),
        ],
        compiler_params=_params(("parallel", "parallel", "arbitrary")),
        name="mla_kv_proj",
    )(ckv, kpe, w_uk, w_uv)


def _attn_kernel(q_ref, k_ref, v_ref, o_ref, m_sc, l_sc, acc_sc, *, q_pos0, scale, tk):
    qi = pl.program_id(2)
    tq = q_ref.shape[2]
    nk = k_ref.shape[2] // tk
    q = q_ref[0, 0]
    m_sc[...] = jnp.full(m_sc.shape, NEG_BIG, F32)
    l_sc[...] = jnp.zeros(l_sc.shape, F32)
    acc_sc[...] = jnp.zeros(acc_sc.shape, F32)

    def step(j, masked):
        start = pl.multiple_of(j * tk, tk)
        s = _dot_nt(q, k_ref[0, 0, pl.ds(start, tk), :]) * scale
        if masked:
            q_chunk = (q_pos0 + qi * tq + lax.broadcasted_iota(jnp.int32, (tq, tk), 0)) // CHUNK
            k_chunk = (start + lax.broadcasted_iota(jnp.int32, (tq, tk), 1)) // CHUNK
            s = jnp.where(k_chunk <= q_chunk, s, NEG_BIG)
        m_prev = m_sc[...]
        m_new = jnp.maximum(m_prev, jnp.max(s, axis=-1, keepdims=True))
        alpha = jnp.exp(m_prev - m_new)
        p = jnp.exp(s - m_new)
        l_sc[...] = alpha * l_sc[...] + jnp.sum(p, axis=-1, keepdims=True)
        acc_sc[...] = alpha * acc_sc[...] + _dot(p.astype(BF16), v_ref[0, 0, pl.ds(start, tk), :])
        m_sc[...] = m_new

    first_q_chunk = (q_pos0 + qi * tq) // CHUNK
    last_q_chunk = (q_pos0 + qi * tq + tq - 1) // CHUNK
    n_full = jnp.minimum(((first_q_chunk + 1) * CHUNK) // tk, nk)
    n_seen = jnp.minimum((last_q_chunk * CHUNK + CHUNK - 1) // tk + 1, nk)
    lax.fori_loop(0, n_full, lambda j, c: (step(j, False), c)[1], 0)
    lax.fori_loop(n_full, n_seen, lambda j, c: (step(j, True), c)[1], 0)
    o_ref[0] = (acc_sc[...] / l_sc[...]).astype(o_ref.dtype)


def _attention(q, k, v, q_pos0, qk_dim):
    bsz, n_heads, seq, width = q.shape
    klen = k.shape[2]
    vdim = v.shape[3]
    tq = _tile(seq, 512)
    tk = _tile(klen, 512) if klen % 512 == 0 else klen
    return pl.pallas_call(
        functools.partial(_attn_kernel, q_pos0=q_pos0, scale=float(qk_dim) ** -0.5, tk=tk),
        grid=(bsz, n_heads, seq // tq),
        in_specs=[
            pl.BlockSpec((1, 1, tq, width), lambda b, h, qi: (b, h, qi, 0)),
            pl.BlockSpec((1, 1, klen, width), lambda b, h, qi: (b, h, 0, 0)),
            pl.BlockSpec((1, 1, klen, vdim), lambda b, h, qi: (b, h, 0, 0)),
        ],
        out_specs=pl.BlockSpec((1, tq, vdim), lambda b, h, qi: (b, qi, h)),
        out_shape=jax.ShapeDtypeStruct((bsz, seq, n_heads * vdim), BF16),
        scratch_shapes=[pltpu.VMEM((tq, 1), F32), pltpu.VMEM((tq, 1), F32), pltpu.VMEM((tq, vdim), F32)],
        compiler_params=_params(("parallel", "parallel", "arbitrary")),
        name="mla_attention",
    )(q, k, v)


def _out_proj_kernel(a_ref, b_ref, c_ref, x_ref, mod_ref, gpost_ref, gpre_ref, wa_ref, wb_ref, wc_ref,
                     wrh_ref, wrl_ref, br_ref, x1_ref, h2_ref, gate_ref, idx_ref):
    mix = _dot(a_ref[0], wa_ref[...]) + _dot(b_ref[0], wb_ref[...]) + _dot(c_ref[0], wc_ref[...])
    mod = mod_ref[0]
    x1 = x_ref[0] + mod[2:3] * _rms(mix, gpost_ref[...])
    x1_ref[0] = x1
    h2 = _rms(x1, gpre_ref[...]) * (1.0 + mod[4:5]) + mod[3:4]
    h2_ref[0] = h2
    hi = h2.astype(BF16)
    lo = (h2 - hi.astype(F32)).astype(BF16)
    logits = _dot(hi, wrh_ref[...]) + (_dot(hi, wrl_ref[...]) + _dot(lo, wrh_ref[...])) + br_ref[...]
    lane = lax.broadcasted_iota(jnp.int32, logits.shape, 1)
    lane_f = lane.astype(F32)
    gates = jnp.zeros(logits.shape, F32)
    idxs = jnp.zeros(logits.shape, F32)
    top0 = None
    denom = None
    for k in range(TOP_K):
        m = jnp.max(logits, axis=-1, keepdims=True)
        sel = jnp.min(jnp.where(logits == m, lane_f, float(LANES)), axis=-1, keepdims=True)
        if k == 0:
            top0 = m
            e = jnp.ones_like(m)
            denom = e
        else:
            e = jnp.exp(m - top0)
            denom = denom + e
        gates = jnp.where(lane == k, e, gates)
        idxs = jnp.where(lane == k, sel, idxs)
        logits = jnp.where(lane_f == sel, NEG_BIG * 2.0, logits)
    gate_ref[0] = gates / denom
    idx_ref[0] = idxs.astype(jnp.int32)


def _out_proj(a_out, b_out, c_out, x, mod, g_post, g_pre, wa, wb, wc, wr_hi, wr_lo, br):
    bsz, seq, d = x.shape
    tm = _tile(seq, 256)
    row = lambda w: pl.BlockSpec((1, tm, w), lambda b, i: (b, i, 0))
    const = lambda a: pl.BlockSpec(a.shape, lambda b, i: (0, 0))
    return pl.pallas_call(
        _out_proj_kernel,
        grid=(bsz, seq // tm),
        in_specs=[
            row(a_out.shape[2]), row(b_out.shape[2]), row(c_out.shape[2]), row(d),
            pl.BlockSpec((1, 6, d), lambda b, i: (b, 0, 0)),
            const(g_post), const(g_pre), const(wa), const(wb), const(wc),
            const(wr_hi), const(wr_lo), const(br),
        ],
        out_specs=[row(d), row(d), row(LANES), row(LANES)],
        out_shape=[
            jax.ShapeDtypeStruct((bsz, seq, d), F32),
            jax.ShapeDtypeStruct((bsz, seq, d), F32),
            jax.ShapeDtypeStruct((bsz, seq, LANES), F32),
            jax.ShapeDtypeStruct((bsz, seq, LANES), jnp.int32),
        ],
        compiler_params=_params(("parallel", "parallel")),
        name="out_proj_router",
    )(a_out, b_out, c_out, x, mod, g_post, g_pre, wa, wb, wc, wr_hi, wr_lo, br)


def _moe_kernel(be_ref, nu_ref, x_ref, wg_ref, wu_ref, bg_ref, bu_ref, wd_ref, bd_ref, o_ref, acc_sc):
    m = pl.program_id(0)
    f = pl.program_id(1)
    nf = pl.num_programs(1)
    used = m < nu_ref[0]

    @pl.when(jnp.logical_and(used, f == 0))
    def _():
        acc_sc[...] = jnp.broadcast_to(bd_ref[...], acc_sc.shape)

    @pl.when(used)
    def _():
        x = x_ref[...]
        g = _dot(x, wg_ref[...].astype(BF16)) + bg_ref[...]
        u = _dot(x, wu_ref[...].astype(BF16)) + bu_ref[...]
        x_glu = jnp.minimum(g, SWIGLU_LIMIT)
        x_lin = jnp.clip(u, -SWIGLU_LIMIT, SWIGLU_LIMIT)
        act = x_glu * _sigmoid(SWIGLU_ALPHA * x_glu) * (x_lin + 1.0)
        acc_sc[...] += _dot(act.astype(BF16), wd_ref[...].astype(BF16))

    @pl.when(jnp.logical_and(f == nf - 1, used))
    def _():
        o_ref[...] = acc_sc[...].astype(o_ref.dtype)

    @pl.when(jnp.logical_and(f == nf - 1, jnp.logical_not(used)))
    def _():
        o_ref[...] = jnp.zeros(o_ref.shape, o_ref.dtype)


def _moe_ffn(x_sorted, block_e, n_used, w_gu, b_gu, w_dn, b_dn, layer, tm):
    rows, d = x_sorted.shape
    n_blocks = rows // tm
    d_exp = w_dn.shape[2]
    tf = _tile(d_exp, 256)
    nf = d_exp // tf
    n_exp = w_gu.shape[1]
    b_gu4 = b_gu.reshape(b_gu.shape[0], n_exp, 1, 2 * d_exp)
    b_dn4 = b_dn.reshape(b_dn.shape[0], n_exp, 1, d)

    def f_idx(m, f, nu):
        return jnp.where(m < nu[0], f, nf - 1)

    def m_idx(m, nu):
        return jnp.minimum(m, nu[0] - 1)

    grid_spec = pltpu.PrefetchScalarGridSpec(
        num_scalar_prefetch=2,
        grid=(n_blocks, nf),
        in_specs=[
            pl.BlockSpec((tm, d), lambda m, f, be, nu: (m_idx(m, nu), 0)),
            pl.BlockSpec((None, None, d, tf), lambda m, f, be, nu: (layer, be[m_idx(m, nu)], 0, f_idx(m, f, nu))),
            pl.BlockSpec((None, None, d, tf), lambda m, f, be, nu: (layer, be[m_idx(m, nu)], 0, nf + f_idx(m, f, nu))),
            pl.BlockSpec((None, None, 1, tf), lambda m, f, be, nu: (layer, be[m_idx(m, nu)], 0, f_idx(m, f, nu))),
            pl.BlockSpec((None, None, 1, tf), lambda m, f, be, nu: (layer, be[m_idx(m, nu)], 0, nf + f_idx(m, f, nu))),
            pl.BlockSpec((None, None, tf, d), lambda m, f, be, nu: (layer, be[m_idx(m, nu)], f_idx(m, f, nu), 0)),
            pl.BlockSpec((None, None, 1, d), lambda m, f, be, nu: (layer, be[m_idx(m, nu)], 0, 0)),
        ],
        out_specs=pl.BlockSpec((tm, d), lambda m, f, be, nu: (m, 0)),
        scratch_shapes=[pltpu.VMEM((tm, d), F32)],
    )
    return pl.pallas_call(
        _moe_kernel,
        grid_spec=grid_spec,
        out_shape=jax.ShapeDtypeStruct((rows, d), F32),
        compiler_params=_params(("arbitrary", "arbitrary")),
        name="moe_expert_ffn",
    )(block_e, n_used, x_sorted, w_gu, w_gu, b_gu4, b_gu4, w_dn, b_dn4)


def _route(top_idx, n_exp, tm, n_blocks):
    n_tok, top_k = top_idx.shape
    n_rows = n_tok * top_k
    flat_e = top_idx.reshape(-1)
    onehot = (flat_e[:, None] == jnp.arange(n_exp, dtype=jnp.int32)[None, :]).astype(jnp.int32)
    csum = jnp.cumsum(onehot, axis=0)
    rank = jnp.sum(csum * onehot, axis=1) - 1
    counts = csum[-1]
    padded = (counts + tm - 1) // tm * tm
    pad_end = jnp.cumsum(padded)
    pad_start = pad_end - padded
    dest = pad_start[flat_e] + rank
    n_used = (pad_end[-1] // tm).astype(jnp.int32).reshape(1)
    block_start = jnp.arange(n_blocks, dtype=jnp.int32) * tm
    block_e = jnp.minimum(jnp.sum((pad_end[None, :] <= block_start[:, None]).astype(jnp.int32), axis=1), n_exp - 1)
    src_tok = jnp.zeros((n_blocks * tm,), jnp.int32).at[dest].set(jnp.arange(n_rows, dtype=jnp.int32) // top_k)
    return dest, src_tok, block_e, n_used


def _ffn_post_kernel(x1_ref, f_ref, mod_ref, g_ref, o_ref):
    mod = mod_ref[0]
    o_ref[0] = x1_ref[0] + mod[5:6] * _rms(f_ref[0], g_ref[...])


def _ffn_post(x1, f, mod, g):
    bsz, seq, d = x1.shape
    tm = _tile(seq, 512)
    row = pl.BlockSpec((1, tm, d), lambda b, i: (b, i, 0))
    return pl.pallas_call(
        _ffn_post_kernel,
        grid=(bsz, seq // tm),
        in_specs=[row, row, pl.BlockSpec((1, 6, d), lambda b, i: (b, 0, 0)), pl.BlockSpec((1, d), lambda b, i: (0, 0))],
        out_specs=row,
        out_shape=jax.ShapeDtypeStruct((bsz, seq, d), F32),
        compiler_params=_params(("parallel", "parallel")),
        name="ffn_post",
    )(x1, f, mod, g)


def _rot_half_cols(w, head_dim):
    d = w.shape[0]
    w3 = w.reshape(d, -1, head_dim)
    half = head_dim // 2
    return jnp.concatenate([-w3[..., half:], w3[..., :half]], axis=-1).reshape(d, -1)


def _rope_tables(pos, head_dim):
    half = head_dim // 2
    inv = ROPE_BASE ** (-jnp.arange(half, dtype=F32) / half)
    ang = pos.astype(F32)[:, None] * inv[None, :]
    cos = jnp.cos(ang)
    sin = jnp.sin(ang)
    return jnp.concatenate([cos, cos], axis=1), jnp.concatenate([sin, sin], axis=1)


class _Dims:
    def __init__(self, state_ret, cache_ckv, cache_kpe, lru_w_a, w_uk, w_uv, conv_w, w_router, w_down):
        _, _, self.ret_heads, self.ret_dk, self.ret_dv = state_ret.shape
        self.kv_lora = cache_ckv.shape[-1]
        self.qk_rope = cache_kpe.shape[-1]
        self.mix_a = conv_w.shape[-1]
        self.conv_w = conv_w.shape[1]
        self.mla_heads, self.qk_nope = w_uk.shape[2], w_uk.shape[3]
        self.v_head = w_uv.shape[3]
        self.n_exp = w_router.shape[-1]
        self.mix_b = self.ret_heads * self.ret_dv
        self.ret_qk = self.ret_heads * self.ret_dk


def _layer_weights(l, dm, w_in, conv_w, conv_b, lru_w_a, lru_b_a, lru_w_x, lru_b_x, lru_lam, g_q_norm, w_uq,
                   g_kv_norm, w_uk, w_uv, w_out, w_router, b_router, g_mix_pre, g_mix_post, g_ffn_pre, g_ffn_post):
    d = w_in.shape[1]
    q_lora = g_q_norm.shape[1]
    sizes = (dm.mix_a, dm.mix_a, dm.ret_qk, dm.ret_qk, dm.mix_b, dm.mix_b, q_lora, dm.kv_lora, dm.qk_rope)
    offs = np.concatenate([[0], np.cumsum(sizes)])
    wl = w_in[l]
    seg = [wl[:, offs[i]:offs[i + 1]] for i in range(len(sizes))]
    w_xa, w_ya, w_qr, w_kr, w_vr, w_gr, w_ql, w_kvl, w_kpe = seg
    blk = dm.mix_b
    kv_cols = dm.kv_lora + 2 * dm.qk_rope
    kv_pad = (-kv_cols) % blk
    cols = [w_xa, w_ya, jnp.concatenate([w_qr, w_kr], axis=1), w_vr, w_gr, w_ql,
            jnp.concatenate([w_kvl, w_kpe, _rot_half_cols(w_kpe, dm.qk_rope), jnp.zeros((d, kv_pad), F32)], axis=1),
            jnp.concatenate([_rot_half_cols(w_qr, dm.ret_dk), _rot_half_cols(w_kr, dm.ret_dk)], axis=1)]
    assert all(c.shape[1] == blk for c in cols), [c.shape for c in cols]
    w_ext = jnp.concatenate(cols, axis=1).astype(BF16)

    n_lru = lru_w_a.shape[1]
    eye = jnp.eye(n_lru, dtype=F32)
    block_diag = lambda w: jnp.einsum("nde,nm->ndme", w, eye).reshape(dm.mix_a, dm.mix_a).astype(BF16)

    qk_dim = dm.qk_nope + dm.qk_rope
    wq3 = w_uq[l].reshape(q_lora, dm.mla_heads, qk_dim)
    pe = wq3[..., dm.qk_nope:]
    half = dm.qk_rope // 2
    pe_rot = jnp.concatenate([-pe[..., half:], pe[..., :half]], axis=-1)
    w_uq_ext = jnp.concatenate([wq3, pe_rot], axis=-1).transpose(1, 0, 2).astype(BF16)

    wo = w_out[l].astype(BF16)
    n_exp = dm.n_exp
    wr = jnp.concatenate([w_router[l], jnp.zeros((d, LANES - n_exp), F32)], axis=1)
    wr_hi = wr.astype(BF16)
    wr_lo = (wr - wr_hi.astype(F32)).astype(BF16)
    br = jnp.concatenate([b_router[l], jnp.full((LANES - n_exp,), NEG_BIG, F32)]).reshape(1, LANES)
    row = lambda v: v[l].reshape(1, -1)
    return dict(
        w_ext=w_ext, conv_w=conv_w[l], conv_b=row(conv_b),
        wa_bd=block_diag(lru_w_a[l]), wx_bd=block_diag(lru_w_x[l]),
        ba=row(lru_b_a), bx=row(lru_b_x), lam=row(lru_lam),
        g_q=row(g_q_norm), w_uq_ext=w_uq_ext, g_kv=row(g_kv_norm),
        w_uk=w_uk[l].transpose(1, 0, 2).astype(BF16), w_uv=w_uv[l].transpose(1, 0, 2).astype(BF16),
        wo_a=wo[:dm.mix_a], wo_b=wo[dm.mix_a:dm.mix_a + dm.mix_b], wo_c=wo[dm.mix_a + dm.mix_b:],
        wr_hi=wr_hi, wr_lo=wr_lo, br=br,
        g_mix_pre=row(g_mix_pre), g_mix_post=row(g_mix_post), g_ffn_pre=row(g_ffn_pre), g_ffn_post=row(g_ffn_post),
    )


def _mixer(x, mod, wl, dm, conv_buf, h0, s0, past_ckv, past_kpe):
    bsz, seq, d = x.shape
    past_len = 0 if past_ckv is None else past_ckv.shape[1]
    pos = past_len + jnp.arange(seq, dtype=jnp.int32)
    u = _in_proj(x, mod, wl["g_mix_pre"], wl["w_ext"])

    ctx_rows = 8
    n_ctx = dm.conv_w - 1
    conv_ctx = jnp.concatenate([jnp.zeros((bsz, ctx_rows - n_ctx, dm.mix_a), F32), conv_buf], axis=1)
    a_out, conv_new, h_new = _rg_lru(u, conv_ctx, h0.reshape(bsz, 1, dm.mix_a), wl["conv_w"], wl["conv_b"],
                                     wl["wa_bd"], wl["wx_bd"], wl["ba"], wl["bx"], wl["lam"])

    cos_k, sin_k = _rope_tables(pos, dm.ret_dk)
    k_scale = float(dm.ret_dk) ** -0.5
    scale_row = jnp.concatenate([jnp.ones((dm.ret_qk,), F32), jnp.full((dm.ret_qk,), k_scale, F32)])[None, :]
    cos_t = jnp.tile(cos_k, (1, 2 * dm.ret_heads)) * scale_row
    sin_t = jnp.tile(sin_k, (1, 2 * dm.ret_heads)) * scale_row
    b_out, s_new = _retention(u, s0, cos_t, sin_t, blocks=(2, 3, 4, 7))

    cos_r, sin_r = _rope_tables(pos, dm.qk_rope)
    lane_pad = lambda t, left, right, fill: jnp.concatenate(
        [jnp.full((seq, left), fill, F32), t, jnp.zeros((seq, right), F32)], axis=1)
    ckv_new, kpe_new = _latent(u, wl["g_kv"], lane_pad(cos_r, 0, LANES - dm.qk_rope, 0.0),
                               lane_pad(sin_r, 0, LANES - dm.qk_rope, 0.0), kv_blk=6, blk_w=dm.mix_b, rope_w=dm.qk_rope)
    q = _q_proj(u, wl["g_q"], wl["w_uq_ext"], lane_pad(cos_r, dm.qk_nope, dm.qk_rope, 1.0),
                lane_pad(sin_r, dm.qk_nope, dm.qk_rope, 0.0), ql_blk=5, rope_w=dm.qk_rope)
    if past_ckv is None:
        ckv_all, kpe_all = ckv_new, kpe_new
    else:
        past_kpe_pad = jnp.concatenate([past_kpe, jnp.zeros(past_kpe.shape[:2] + (LANES - dm.qk_rope,), F32)], axis=2)
        ckv_all = jnp.concatenate([past_ckv, ckv_new], axis=1)
        kpe_all = jnp.concatenate([past_kpe_pad, kpe_new], axis=1)
    k, v = _kv_proj(ckv_all, kpe_all, wl["w_uk"], wl["w_uv"])
    c_out = _attention(q, k, v, past_len, dm.qk_nope + dm.qk_rope)

    x1, h2, gates, idx = _out_proj(a_out, b_out, c_out, x, mod, wl["g_mix_post"], wl["g_ffn_pre"],
                                   wl["wo_a"], wl["wo_b"], wl["wo_c"], wl["wr_hi"], wl["wr_lo"], wl["br"])
    states = (ckv_new, kpe_new[:, :, :dm.qk_rope], conv_new[:, ctx_rows - n_ctx:], h_new.reshape(bsz, dm.mix_a), s_new)
    return x1, h2, gates[:, :, :TOP_K], idx[:, :, :TOP_K], states


def kernel(x_prompt, x_sample, c_prompt, c_sample, cache_ckv, cache_kpe, state_conv, state_lru, state_ret, w_ada, b_ada, g_mix_pre, g_mix_post, g_ffn_pre, g_ffn_post, w_in, conv_w, conv_b, lru_w_a, lru_b_a, lru_w_x, lru_b_x, lru_lam, g_q_norm, w_uq, g_kv_norm, w_uk, w_uv, w_out, w_router, b_router, w_gate_up, b_gate_up, w_down, b_down):
    depth = w_in.shape[0]
    dm = _Dims(state_ret, cache_ckv, cache_kpe, lru_w_a, w_uk, w_uv, conv_w, w_router, w_down)
    bp, lp, d = x_prompt.shape
    bs, ls, _ = x_sample.shape
    n_tok = bp * lp + bs * ls
    moe_tm = min(512, max(16, 1 << int(np.log2((n_tok * TOP_K) // dm.n_exp))))
    n_blocks = -(-(n_tok * TOP_K) // moe_tm) + dm.n_exp

    xp, xs = x_prompt, x_sample
    p_states, s_states = [], []
    for l in range(depth):
        wl = _layer_weights(l, dm, w_in, conv_w, conv_b, lru_w_a, lru_b_a, lru_w_x, lru_b_x, lru_lam, g_q_norm, w_uq,
                            g_kv_norm, w_uk, w_uv, w_out, w_router, b_router, g_mix_pre, g_mix_post, g_ffn_pre, g_ffn_post)
        mod_p = _ada_mod(c_prompt, w_ada, b_ada, l).reshape(bp, 6, d)
        mod_s = _ada_mod(c_sample, w_ada, b_ada, l).reshape(bs, 6, d)
        zeros = lambda *s: jnp.zeros(s, F32)
        x1p, h2p, gp, ip, st_p = _mixer(
            xp, mod_p, wl, dm, zeros(bp, dm.conv_w - 1, dm.mix_a), zeros(bp, dm.mix_a),
            zeros(bp, dm.ret_heads, dm.ret_dk, dm.ret_dv), None, None)
        x1s, h2s, gs, is_, st_s = _mixer(
            xs, mod_s, wl, dm, state_conv[l], state_lru[l], state_ret[l], cache_ckv[l], cache_kpe[l])
        p_states.append(st_p)
        s_states.append(st_s)

        h2 = jnp.concatenate([h2p.reshape(-1, d), h2s.reshape(-1, d)], axis=0)
        gates = jnp.concatenate([gp.reshape(-1, TOP_K), gs.reshape(-1, TOP_K)], axis=0)
        top_idx = jnp.concatenate([ip.reshape(-1, TOP_K), is_.reshape(-1, TOP_K)], axis=0)
        dest, src_tok, block_e, n_used = _route(top_idx, dm.n_exp, moe_tm, n_blocks)
        x_sorted = jnp.take(h2, src_tok, axis=0).astype(BF16)
        out_buf = _moe_ffn(x_sorted, block_e, n_used, w_gate_up, b_gate_up, w_down, b_down, l, moe_tm)
        rows = jnp.take(out_buf, dest, axis=0).reshape(n_tok, TOP_K, d)
        f = jnp.sum(rows * gates[:, :, None], axis=1)
        xp = _ffn_post(x1p, f[:bp * lp].reshape(bp, lp, d), mod_p, wl["g_ffn_post"])
        xs = _ffn_post(x1s, f[bp * lp:].reshape(bs, ls, d), mod_s, wl["g_ffn_post"])

    stack = lambda sts: tuple(jnp.stack(t, axis=0) for t in zip(*sts))
    p_ckv, p_kpe, p_conv, p_lru, p_ret = stack(p_states)
    s_ckv, s_kpe, s_conv, s_lru, s_ret = stack(s_states)
    return (xp, xs, p_ckv, p_kpe, p_conv, p_lru, p_ret, s_ckv, s_kpe, s_conv, s_lru, s_ret)
```

```python
import functools

import numpy as np
import jax
import jax.numpy as jnp
from jax import lax
from jax.experimental import pallas as pl
from jax.experimental.pallas import tpu as pltpu

CHUNK = 64
EPS = 1e-6
ROPE_BASE = 10000.0
RG_C = 8.0
TOP_K = 4
SWIGLU_LIMIT = 7.0
SWIGLU_ALPHA = 1.702
NEG_BIG = -1e30
GELU_C = float(np.sqrt(2.0 / np.pi))

LANES = 128
VMEM_LIMIT_BYTES = 56 * 1024 * 1024

F32 = jnp.float32
BF16 = jnp.bfloat16


def _params(semantics):
    return pltpu.CompilerParams(dimension_semantics=semantics, vmem_limit_bytes=VMEM_LIMIT_BYTES)


def _tile(n, pref):
    if n <= pref:
        return n
    t = pref
    while n % t:
        t //= 2
    return t


def _rms(x, g):
    return x * lax.rsqrt(jnp.mean(x * x, axis=-1, keepdims=True) + EPS) * g


def _dot(a, b):
    return jnp.dot(a, b, preferred_element_type=F32)


def _dot_nt(a, b):
    return lax.dot_general(a, b, (((1,), (1,)), ((), ())), preferred_element_type=F32)


def _dot_tn(a, b):
    return lax.dot_general(a, b, (((0,), (0,)), ((), ())), preferred_element_type=F32)


def _sigmoid(x):
    return 1.0 / (1.0 + jnp.exp(-x))


def _ada_kernel(c_ref, w_ref, b_ref, o_ref):
    c = c_ref[...]
    s = (c * _sigmoid(c)).astype(BF16)
    o_ref[...] = _dot(s, w_ref[...].astype(BF16)) + b_ref[...]


def _ada_mod(c, w_ada, b_ada, layer):
    bsz, d = c.shape
    n = w_ada.shape[-1]
    tn = _tile(n, 1024)
    return pl.pallas_call(
        _ada_kernel,
        grid=(n // tn,),
        in_specs=[
            pl.BlockSpec((bsz, d), lambda j: (0, 0)),
            pl.BlockSpec((None, d, tn), lambda j: (layer, 0, j)),
            pl.BlockSpec((None, 1, tn), lambda j: (layer, 0, j)),
        ],
        out_specs=pl.BlockSpec((bsz, tn), lambda j: (0, j)),
        out_shape=jax.ShapeDtypeStruct((bsz, n), F32),
        compiler_params=_params(("arbitrary",)),
        name="ada_mod",
    )(c, w_ada, b_ada.reshape(b_ada.shape[0], 1, n))


def _in_proj_kernel(x_ref, mod_ref, g_ref, w_ref, u_ref, h_sc):
    @pl.when(pl.program_id(2) == 0)
    def _():
        mod = mod_ref[0]
        h = _rms(x_ref[0], g_ref[...]) * (1.0 + mod[1:2]) + mod[0:1]
        h_sc[...] = h.astype(BF16)

    u_ref[0] = _dot(h_sc[...], w_ref[...])


def _in_proj(x, mod, g, w_ext):
    bsz, seq, d = x.shape
    n = w_ext.shape[1]
    tm = _tile(seq, 512)
    tn = _tile(n, 1024)
    return pl.pallas_call(
        _in_proj_kernel,
        grid=(bsz, seq // tm, n // tn),
        in_specs=[
            pl.BlockSpec((1, tm, d), lambda b, i, j: (b, i, 0)),
            pl.BlockSpec((1, 6, d), lambda b, i, j: (b, 0, 0)),
            pl.BlockSpec((1, d), lambda b, i, j: (0, 0)),
            pl.BlockSpec((d, tn), lambda b, i, j: (0, j)),
        ],
        out_specs=pl.BlockSpec((1, tm, tn), lambda b, i, j: (b, i, j)),
        out_shape=jax.ShapeDtypeStruct((bsz, seq, n), F32),
        scratch_shapes=[pltpu.VMEM((tm, d), BF16)],
        compiler_params=_params(("parallel", "parallel", "arbitrary")),
        name="in_proj",
    )(x, mod, g, w_ext)


def _lru_kernel(xa_ref, ya_ref, cbuf_ref, h0_ref, cw_ref, cb_ref, wa_ref, wx_ref, ba_ref, bx_ref,
                lam_ref, out_ref, cnew_ref, hlast_ref, xbuf, hcar):
    tl = xa_ref.shape[1]
    width = xa_ref.shape[2]
    ctx = xbuf.shape[0] - tl

    @pl.when(pl.program_id(1) == 0)
    def _():
        xbuf[0:ctx, :] = cbuf_ref[0]
        hcar[...] = h0_ref[0]

    xa = xa_ref[0]
    xbuf[ctx:ctx + tl, :] = xa
    cw = cw_ref[...]
    n_tap = cw.shape[0]
    xc = cb_ref[...] + cw[n_tap - 1:n_tap] * xa
    for k in range(n_tap - 1):
        off = ctx - (n_tap - 1) + k
        xc = xc + cw[k:k + 1] * xbuf[off:off + tl, :]
    new_ctx = xbuf[tl:tl + ctx, :]
    xbuf[0:ctx, :] = new_ctx
    cnew_ref[0] = new_ctx

    xcb = xc.astype(BF16)
    r = _sigmoid(_dot(xcb, wa_ref[...]) + ba_ref[...])
    gi = _sigmoid(_dot(xcb, wx_ref[...]) + bx_ref[...])
    z = -lam_ref[...]
    softplus = jnp.maximum(z, 0.0) + jnp.log1p(jnp.exp(-jnp.abs(z)))
    log_a = (-RG_C) * r * softplus
    a = jnp.exp(log_a)
    b = jnp.sqrt(1.0 - jnp.exp(2.0 * log_a)) * gi * xc

    row = lax.broadcasted_iota(jnp.int32, (tl, width), 0)
    s = 1
    while s < tl:
        a_sh = pltpu.roll(a, s, 0)
        b_sh = pltpu.roll(b, s, 0)
        valid = row >= s
        b = jnp.where(valid, a * b_sh + b, b)
        a = jnp.where(valid, a * a_sh, a)
        s *= 2
    h = a * hcar[...] + b
    h_last = h[tl - 1:tl, :]
    hcar[...] = h_last
    hlast_ref[0] = h_last

    ya = ya_ref[0]
    gelu = 0.5 * ya * (1.0 + jnp.tanh(GELU_C * (ya + 0.044715 * (ya * ya * ya))))
    out_ref[0] = (h * gelu).astype(out_ref.dtype)


def _rg_lru(u, conv_ctx, h0, cw, cb, wa_bd, wx_bd, ba, bx, lam):
    bsz, seq, _ = u.shape
    width = cw.shape[1]
    tl = _tile(seq, 512)
    ctx = conv_ctx.shape[1]
    xa_blk = 0
    ya_blk = 1
    vec = lambda: pl.BlockSpec((1, width), lambda b, t: (0, 0))
    return pl.pallas_call(
        _lru_kernel,
        grid=(bsz, seq // tl),
        in_specs=[
            pl.BlockSpec((1, tl, width), lambda b, t: (b, t, xa_blk)),
            pl.BlockSpec((1, tl, width), lambda b, t: (b, t, ya_blk)),
            pl.BlockSpec((1, ctx, width), lambda b, t: (b, 0, 0)),
            pl.BlockSpec((1, 1, width), lambda b, t: (b, 0, 0)),
            pl.BlockSpec(cw.shape, lambda b, t: (0, 0)),
            vec(),
            pl.BlockSpec((width, width), lambda b, t: (0, 0)),
            pl.BlockSpec((width, width), lambda b, t: (0, 0)),
            vec(), vec(), vec(),
        ],
        out_specs=[
            pl.BlockSpec((1, tl, width), lambda b, t: (b, t, 0)),
            pl.BlockSpec((1, ctx, width), lambda b, t: (b, 0, 0)),
            pl.BlockSpec((1, 1, width), lambda b, t: (b, 0, 0)),
        ],
        out_shape=[
            jax.ShapeDtypeStruct((bsz, seq, width), BF16),
            jax.ShapeDtypeStruct((bsz, ctx, width), F32),
            jax.ShapeDtypeStruct((bsz, 1, width), F32),
        ],
        scratch_shapes=[pltpu.VMEM((tl + ctx, width), F32), pltpu.VMEM((1, width), F32)],
        compiler_params=_params(("parallel", "arbitrary")),
        name="rg_lru",
    )(u, u, conv_ctx, h0, cw, cb, wa_bd, wx_bd, ba, bx, lam)


def _ret_kernel(qk_ref, rot_ref, v_ref, g_ref, cos_ref, sin_ref, dmask_ref, qdec_ref, kdec_ref,
                s0_ref, out_ref, slast_ref, s_sc, *, gchunk):
    n_heads, dk, dv = s_sc.shape
    hdk = n_heads * dk

    @pl.when(pl.program_id(1) == 0)
    def _():
        s_sc[...] = s0_ref[0]

    qkr = qk_ref[0] * cos_ref[...] + rot_ref[0] * sin_ref[...]
    v = v_ref[0]
    vdec = v * kdec_ref[...]
    gate = g_ref[0]
    qdec = qdec_ref[...]
    for h in range(n_heads):
        q = qkr[:, h * dk:(h + 1) * dk].astype(BF16)
        k = qkr[:, hdk + h * dk:hdk + (h + 1) * dk].astype(BF16)
        vh = v[:, h * dv:(h + 1) * dv].astype(BF16)
        vd = vdec[:, h * dv:(h + 1) * dv].astype(BF16)
        s_prev = s_sc[h]
        scores = _dot_nt(q, k) * dmask_ref[h]
        o = _dot(scores.astype(BF16), vh) + qdec[:, h * dv:(h + 1) * dv] * _dot(q, s_prev.astype(BF16))
        s_sc[h] = s_prev * gchunk[h] + _dot_tn(k, vd)
        o = o * lax.rsqrt(jnp.mean(o * o, axis=-1, keepdims=True) + EPS)
        gh = gate[:, h * dv:(h + 1) * dv]
        out_ref[0, :, h * dv:(h + 1) * dv] = (o * (gh * _sigmoid(gh))).astype(out_ref.dtype)
    slast_ref[0] = s_sc[...]


def _retention(u, s0, cos_t, sin_t, blocks):
    bsz, seq, _ = u.shape
    _, n_heads, dk, dv = s0.shape
    width = n_heads * dv
    tc = _tile(seq, 256)
    log_g = np.log1p(-np.exp2(-5.0 - np.arange(n_heads, dtype=np.float64)))
    idx = np.arange(tc, dtype=np.float64)
    diff = idx[:, None] - idx[None, :]
    dmask = np.where(diff[None] >= 0, np.exp(np.maximum(diff, 0.0)[None] * log_g[:, None, None]), 0.0)
    qdec = np.repeat(np.exp((idx + 1.0)[:, None] * log_g[None, :]), dv, axis=1)
    kdec = np.repeat(np.exp((tc - 1.0 - idx)[:, None] * log_g[None, :]), dv, axis=1)
    gchunk = tuple(float(g) for g in np.exp(tc * log_g))
    qk_blk, v_blk, g_blk, rot_blk = blocks
    u_spec = lambda blk: pl.BlockSpec((1, tc, width), lambda b, n: (b, n, blk))
    full2 = lambda a: pl.BlockSpec(a.shape, lambda b, n: (0, 0))
    return pl.pallas_call(
        functools.partial(_ret_kernel, gchunk=gchunk),
        grid=(bsz, seq // tc),
        in_specs=[
            u_spec(qk_blk), u_spec(rot_blk), u_spec(v_blk), u_spec(g_blk),
            pl.BlockSpec((tc, width), lambda b, n: (n, 0)),
            pl.BlockSpec((tc, width), lambda b, n: (n, 0)),
            pl.BlockSpec(dmask.shape, lambda b, n: (0, 0, 0)),
            full2(qdec), full2(kdec),
            pl.BlockSpec((1, n_heads, dk, dv), lambda b, n: (b, 0, 0, 0)),
        ],
        out_specs=[
            pl.BlockSpec((1, tc, width), lambda b, n: (b, n, 0)),
            pl.BlockSpec((1, n_heads, dk, dv), lambda b, n: (b, 0, 0, 0)),
        ],
        out_shape=[
            jax.ShapeDtypeStruct((bsz, seq, width), BF16),
            jax.ShapeDtypeStruct((bsz, n_heads, dk, dv), F32),
        ],
        scratch_shapes=[pltpu.VMEM((n_heads, dk, dv), F32)],
        compiler_params=_params(("parallel", "arbitrary")),
        name="retention",
    )(u, u, u, u, cos_t, sin_t, jnp.asarray(dmask, F32), jnp.asarray(qdec, F32),
      jnp.asarray(kdec, F32), s0)


def _latent_kernel(kv_ref, g_ref, cos_ref, sin_ref, ckv_ref, kpe_ref, *, rope_w):
    lora = ckv_ref.shape[2]
    blk = kv_ref[0]
    ckv_ref[0] = _rms(blk[:, :lora], g_ref[...])
    kp = blk[:, lora:lora + LANES]
    rolled = pltpu.roll(kp, LANES - rope_w, 1)
    kpe_ref[0] = kp * cos_ref[...] + rolled * sin_ref[...]


def _latent(u, g_kv, cos_p, sin_p, kv_blk, blk_w, rope_w):
    bsz, seq, _ = u.shape
    lora = g_kv.shape[1]
    tm = _tile(seq, 512)
    return pl.pallas_call(
        functools.partial(_latent_kernel, rope_w=rope_w),
        grid=(bsz, seq // tm),
        in_specs=[
            pl.BlockSpec((1, tm, blk_w), lambda b, i: (b, i, kv_blk)),
            pl.BlockSpec((1, lora), lambda b, i: (0, 0)),
            pl.BlockSpec((tm, LANES), lambda b, i: (i, 0)),
            pl.BlockSpec((tm, LANES), lambda b, i: (i, 0)),
        ],
        out_specs=[
            pl.BlockSpec((1, tm, lora), lambda b, i: (b, i, 0)),
            pl.BlockSpec((1, tm, LANES), lambda b, i: (b, i, 0)),
        ],
        out_shape=[
            jax.ShapeDtypeStruct((bsz, seq, lora), F32),
            jax.ShapeDtypeStruct((bsz, seq, LANES), F32),
        ],
        compiler_params=_params(("parallel", "parallel")),
        name="latent_norm_rope",
    )(u, g_kv, cos_p, sin_p)


def _q_proj_kernel(ql_ref, g_ref, w_ref, cos_ref, sin_ref, q_ref, cq_sc, *, rope_w):
    @pl.when(pl.program_id(2) == 0)
    def _():
        cq_sc[...] = _rms(ql_ref[0], g_ref[...]).astype(BF16)

    r = _dot(cq_sc[...], w_ref[0])
    rolled = pltpu.roll(r, r.shape[1] - rope_w, 1)
    q_ref[0, 0] = (r * cos_ref[...] + rolled * sin_ref[...]).astype(q_ref.dtype)


def _q_proj(u, g_q, w_uq_ext, cos_q, sin_q, ql_blk, rope_w):
    bsz, seq, _ = u.shape
    n_heads, q_lora, width = w_uq_ext.shape
    tm = _tile(seq, 512)
    return pl.pallas_call(
        functools.partial(_q_proj_kernel, rope_w=rope_w),
        grid=(bsz, seq // tm, n_heads),
        in_specs=[
            pl.BlockSpec((1, tm, q_lora), lambda b, i, h: (b, i, ql_blk)),
            pl.BlockSpec((1, q_lora), lambda b, i, h: (0, 0)),
            pl.BlockSpec((1, q_lora, width), lambda b, i, h: (h, 0, 0)),
            pl.BlockSpec((tm, width), lambda b, i, h: (i, 0)),
            pl.BlockSpec((tm, width), lambda b, i, h: (i, 0)),
        ],
        out_specs=pl.BlockSpec((1, 1, tm, width), lambda b, i, h: (b, h, i, 0)),
        out_shape=jax.ShapeDtypeStruct((bsz, n_heads, seq, width), BF16),
        scratch_shapes=[pltpu.VMEM((tm, q_lora), BF16)],
        compiler_params=_params(("parallel", "parallel", "arbitrary")),
        name="mla_q_proj",
    )(u, g_q, w_uq_ext, cos_q, sin_q)


def _kv_proj_kernel(ckv_ref, kpe_ref, wk_ref, wv_ref, k_ref, v_ref):
    ckv = ckv_ref[0].astype(BF16)
    nope = wk_ref.shape[2]
    k_ref[0, 0, :, :nope] = _dot(ckv, wk_ref[0]).astype(k_ref.dtype)
    k_ref[0, 0, :, nope:] = kpe_ref[0].astype(k_ref.dtype)
    v_ref[0, 0] = _dot(ckv, wv_ref[0]).astype(v_ref.dtype)


def _kv_proj(ckv, kpe, w_uk, w_uv):
    bsz, klen, lora = ckv.shape
    n_heads, _, nope = w_uk.shape
    vdim = w_uv.shape[2]
    tm = _tile(klen, 512) if klen % 512 == 0 else klen
    return pl.pallas_call(
        _kv_proj_kernel,
        grid=(bsz, klen // tm, n_heads),
        in_specs=[
            pl.BlockSpec((1, tm, lora), lambda b, i, h: (b, i, 0)),
            pl.BlockSpec((1, tm, LANES), lambda b, i, h: (b, i, 0)),
            pl.BlockSpec((1, lora, nope), lambda b, i, h: (h, 0, 0)),
            pl.BlockSpec((1, lora, vdim), lambda b, i, h: (h, 0, 0)),
        ],
        out_specs=[
            pl.BlockSpec((1, 1, tm, nope + LANES), lambda b, i, h: (b, h, i, 0)),
            pl.BlockSpec((1, 1, tm, vdim), lambda b, i, h: (b, h, i, 0)),
        ],
        out_shape=[
            jax.ShapeDtypeStruct((bsz, n_heads, klen, nope + LANES), BF16),
            jax.ShapeDtypeStruct((bsz, n_heads, klen, vdim), BF16),
        ],
        compiler_params=_params(("parallel", "parallel", "arbitrary")),
        name="mla_kv_proj",
    )(ckv, kpe, w_uk, w_uv)


def _attn_kernel(q_ref, k_ref, v_ref, o_ref, m_sc, l_sc, acc_sc, *, q_pos0, scale, tk):
    qi = pl.program_id(2)
    tq = q_ref.shape[2]
    nk = k_ref.shape[2] // tk
    q = q_ref[0, 0]
    m_sc[...] = jnp.full(m_sc.shape, NEG_BIG, F32)
    l_sc[...] = jnp.zeros(l_sc.shape, F32)
    acc_sc[...] = jnp.zeros(acc_sc.shape, F32)

    def step(j, masked):
        start = pl.multiple_of(j * tk, tk)
        s = _dot_nt(q, k_ref[0, 0, pl.ds(start, tk), :]) * scale
        if masked:
            q_chunk = (q_pos0 + qi * tq + lax.broadcasted_iota(jnp.int32, (tq, tk), 0)) // CHUNK
            k_chunk = (start + lax.broadcasted_iota(jnp.int32, (tq, tk), 1)) // CHUNK
            s = jnp.where(k_chunk <= q_chunk, s, NEG_BIG)
        m_prev = m_sc[...]
        m_new = jnp.maximum(m_prev, jnp.max(s, axis=-1, keepdims=True))
        alpha = jnp.exp(m_prev - m_new)
        p = jnp.exp(s - m_new)
        l_sc[...] = alpha * l_sc[...] + jnp.sum(p, axis=-1, keepdims=True)
        acc_sc[...] = alpha * acc_sc[...] + _dot(p.astype(BF16), v_ref[0, 0, pl.ds(start, tk), :])
        m_sc[...] = m_new

    first_q_chunk = (q_pos0 + qi * tq) // CHUNK
    last_q_chunk = (q_pos0 + qi * tq + tq - 1) // CHUNK
    n_full = jnp.minimum(((first_q_chunk + 1) * CHUNK) // tk, nk)
    n_seen = jnp.minimum((last_q_chunk * CHUNK + CHUNK - 1) // tk + 1, nk)
    lax.fori_loop(0, n_full, lambda j, c: (step(j, False), c)[1], 0)
    lax.fori_loop(n_full, n_seen, lambda j, c: (step(j, True), c)[1], 0)
    o_ref[0] = (acc_sc[...] / l_sc[...]).astype(o_ref.dtype)


def _attention(q, k, v, q_pos0, qk_dim):
    bsz, n_heads, seq, width = q.shape
    klen = k.shape[2]
    vdim = v.shape[3]
    tq = _tile(seq, 512)
    tk = _tile(klen, 512) if klen % 512 == 0 else klen
    return pl.pallas_call(
        functools.partial(_attn_kernel, q_pos0=q_pos0, scale=float(qk_dim) ** -0.5, tk=tk),
        grid=(bsz, n_heads, seq // tq),
        in_specs=[
            pl.BlockSpec((1, 1, tq, width), lambda b, h, qi: (b, h, qi, 0)),
            pl.BlockSpec((1, 1, klen, width), lambda b, h, qi: (b, h, 0, 0)),
            pl.BlockSpec((1, 1, klen, vdim), lambda b, h, qi: (b, h, 0, 0)),
        ],
        out_specs=pl.BlockSpec((1, tq, vdim), lambda b, h, qi: (b, qi, h)),
        out_shape=jax.ShapeDtypeStruct((bsz, seq, n_heads * vdim), BF16),
        scratch_shapes=[pltpu.VMEM((tq, 1), F32), pltpu.VMEM((tq, 1), F32), pltpu.VMEM((tq, vdim), F32)],
        compiler_params=_params(("parallel", "parallel", "arbitrary")),
        name="mla_attention",
    )(q, k, v)


def _out_proj_kernel(a_ref, b_ref, c_ref, x_ref, mod_ref, gpost_ref, gpre_ref, wa_ref, wb_ref, wc_ref,
                     wrh_ref, wrl_ref, br_ref, x1_ref, h2_ref, gate_ref, idx_ref):
    mix = _dot(a_ref[0], wa_ref[...]) + _dot(b_ref[0], wb_ref[...]) + _dot(c_ref[0], wc_ref[...])
    mod = mod_ref[0]
    x1 = x_ref[0] + mod[2:3] * _rms(mix, gpost_ref[...])
    x1_ref[0] = x1
    h2 = _rms(x1, gpre_ref[...]) * (1.0 + mod[4:5]) + mod[3:4]
    h2_ref[0] = h2
    hi = h2.astype(BF16)
    lo = (h2 - hi.astype(F32)).astype(BF16)
    logits = _dot(hi, wrh_ref[...]) + (_dot(hi, wrl_ref[...]) + _dot(lo, wrh_ref[...])) + br_ref[...]
    lane = lax.broadcasted_iota(jnp.int32, logits.shape, 1)
    lane_f = lane.astype(F32)
    gates = jnp.zeros(logits.shape, F32)
    idxs = jnp.zeros(logits.shape, F32)
    top0 = None
    denom = None
    for k in range(TOP_K):
        m = jnp.max(logits, axis=-1, keepdims=True)
        sel = jnp.min(jnp.where(logits == m, lane_f, float(LANES)), axis=-1, keepdims=True)
        if k == 0:
            top0 = m
            e = jnp.ones_like(m)
            denom = e
        else:
            e = jnp.exp(m - top0)
            denom = denom + e
        gates = jnp.where(lane == k, e, gates)
        idxs = jnp.where(lane == k, sel, idxs)
        logits = jnp.where(lane_f == sel, NEG_BIG * 2.0, logits)
    gate_ref[0] = gates / denom
    idx_ref[0] = idxs.astype(jnp.int32)


def _out_proj(a_out, b_out, c_out, x, mod, g_post, g_pre, wa, wb, wc, wr_hi, wr_lo, br):
    bsz, seq, d = x.shape
    tm = _tile(seq, 256)
    row = lambda w: pl.BlockSpec((1, tm, w), lambda b, i: (b, i, 0))
    const = lambda a: pl.BlockSpec(a.shape, lambda b, i: (0, 0))
    return pl.pallas_call(
        _out_proj_kernel,
        grid=(bsz, seq // tm),
        in_specs=[
            row(a_out.shape[2]), row(b_out.shape[2]), row(c_out.shape[2]), row(d),
            pl.BlockSpec((1, 6, d), lambda b, i: (b, 0, 0)),
            const(g_post), const(g_pre), const(wa), const(wb), const(wc),
            const(wr_hi), const(wr_lo), const(br),
        ],
        out_specs=[row(d), row(d), row(LANES), row(LANES)],
        out_shape=[
            jax.ShapeDtypeStruct((bsz, seq, d), F32),
            jax.ShapeDtypeStruct((bsz, seq, d), F32),
            jax.ShapeDtypeStruct((bsz, seq, LANES), F32),
            jax.ShapeDtypeStruct((bsz, seq, LANES), jnp.int32),
        ],
        compiler_params=_params(("parallel", "parallel")),
        name="out_proj_router",
    )(a_out, b_out, c_out, x, mod, g_post, g_pre, wa, wb, wc, wr_hi, wr_lo, br)


def _moe_kernel(be_ref, nu_ref, src_ref, dst_ref, h_hbm, wg_ref, wu_ref, bg_ref, bu_ref, wd_ref, bd_ref,
                rows_hbm, xbuf, xb, obuf, gsem, ssem):
    g = pl.program_id(0)
    f = pl.program_id(1)
    nf = pl.num_programs(1)
    n_used = nu_ref[0]
    tm = xbuf.shape[0]
    chunk = tm // nf
    live = jnp.logical_and(g >= 1, g <= n_used)
    slot = (g + 1) % 2

    def gather_copy(r):
        return pltpu.make_async_copy(h_hbm.at[pl.ds(src_ref[0, r], 1)], xbuf.at[pl.ds(r, 1)], gsem)

    def scatter_copy(r, half):
        return pltpu.make_async_copy(obuf.at[half, pl.ds(r, 1)], rows_hbm.at[pl.ds(dst_ref[0, r], 1)], ssem)

    def wait_gather():
        pltpu.make_async_copy(h_hbm.at[pl.ds(0, tm)], xbuf, gsem).wait()

    def wait_scatter(half):
        pltpu.make_async_copy(obuf.at[half], rows_hbm.at[pl.ds(0, tm)], ssem).wait()

    @pl.when(jnp.logical_and(g == 0, f == 0))
    def _():
        obuf[1] = jnp.zeros(obuf.shape[1:], F32)
        lax.fori_loop(0, tm, lambda r, c: (gather_copy(r).start(), c)[1], 0)

    @pl.when(jnp.logical_and(live, f == 0))
    def _():
        wait_gather()
        xb[...] = xbuf[...].astype(BF16)
        obuf[slot] = jnp.broadcast_to(bd_ref[...], obuf.shape[1:])

    @pl.when(live)
    def _():
        for i in range(chunk):
            r = f * chunk + i
            gather_copy(r).start()
            scatter_copy(r, 1 - slot).start()
        x = xb[...]
        gate = _dot(x, wg_ref[...].astype(BF16)) + bg_ref[...]
        up = _dot(x, wu_ref[...].astype(BF16)) + bu_ref[...]
        x_glu = jnp.minimum(gate, SWIGLU_LIMIT)
        x_lin = jnp.clip(up, -SWIGLU_LIMIT, SWIGLU_LIMIT)
        act = x_glu * _sigmoid(SWIGLU_ALPHA * x_glu) * (x_lin + 1.0)
        obuf[slot] += _dot(act.astype(BF16), wd_ref[...].astype(BF16))

    @pl.when(jnp.logical_and(live, f == nf - 1))
    def _():
        wait_scatter(1 - slot)

    @pl.when(jnp.logical_and(g == n_used + 1, f == 0))
    def _():
        wait_gather()
        lax.fori_loop(0, tm, lambda r, c: (scatter_copy(r, 1 - slot).start(), c)[1], 0)
        wait_scatter(1 - slot)


def _moe_ffn(h_all, src_rows, dst_rows, block_e, n_used, w_gu, b_gu, w_dn, b_dn, layer, n_out_rows):
    n_grid, tm = src_rows.shape
    d = h_all.shape[1]
    d_exp = w_dn.shape[2]
    tf = _tile(d_exp, 256)
    nf = d_exp // tf
    assert tm % nf == 0 and h_all.shape[0] >= tm and n_out_rows >= tm
    n_exp = w_gu.shape[1]
    b_gu4 = b_gu.reshape(b_gu.shape[0], n_exp, 1, 2 * d_exp)
    b_dn4 = b_dn.reshape(b_dn.shape[0], n_exp, 1, d)

    def f_idx(g, f, nu):
        return jnp.where(jnp.logical_and(g >= 1, g <= nu[0]), f, nf - 1)

    def e_idx(g, be, nu):
        return be[jnp.clip(g - 1, 0, nu[0] - 1)]

    grid_spec = pltpu.PrefetchScalarGridSpec(
        num_scalar_prefetch=2,
        grid=(n_grid, nf),
        in_specs=[
            pl.BlockSpec((None, 1, tm), lambda g, f, be, nu: (g, 0, 0), memory_space=pltpu.SMEM),
            pl.BlockSpec((None, 1, tm), lambda g, f, be, nu: (jnp.maximum(g - 2, 0), 0, 0), memory_space=pltpu.SMEM),
            pl.BlockSpec(memory_space=pl.ANY),
            pl.BlockSpec((None, None, d, tf), lambda g, f, be, nu: (layer, e_idx(g, be, nu), 0, f_idx(g, f, nu))),
            pl.BlockSpec((None, None, d, tf), lambda g, f, be, nu: (layer, e_idx(g, be, nu), 0, nf + f_idx(g, f, nu))),
            pl.BlockSpec((None, None, 1, tf), lambda g, f, be, nu: (layer, e_idx(g, be, nu), 0, f_idx(g, f, nu))),
            pl.BlockSpec((None, None, 1, tf), lambda g, f, be, nu: (layer, e_idx(g, be, nu), 0, nf + f_idx(g, f, nu))),
            pl.BlockSpec((None, None, tf, d), lambda g, f, be, nu: (layer, e_idx(g, be, nu), f_idx(g, f, nu), 0)),
            pl.BlockSpec((None, None, 1, d), lambda g, f, be, nu: (layer, e_idx(g, be, nu), 0, 0)),
        ],
        out_specs=pl.BlockSpec(memory_space=pl.ANY),
        scratch_shapes=[
            pltpu.VMEM((tm, d), F32),
            pltpu.VMEM((tm, d), BF16),
            pltpu.VMEM((2, tm, d), F32),
            pltpu.SemaphoreType.DMA(()),
            pltpu.SemaphoreType.DMA(()),
        ],
    )
    return pl.pallas_call(
        _moe_kernel,
        grid_spec=grid_spec,
        out_shape=jax.ShapeDtypeStruct((n_out_rows, d), F32),
        compiler_params=_params(("arbitrary", "arbitrary")),
        name="moe_expert_ffn",
    )(block_e, n_used, src_rows.reshape(n_grid, 1, tm), dst_rows.reshape(n_grid, 1, tm), h_all,
      w_gu, w_gu, b_gu4, b_gu4, w_dn, b_dn4)


def _route(top_idx, n_exp, tm, n_blocks):
    n_tok, top_k = top_idx.shape
    n_rows = n_tok * top_k
    flat_e = top_idx.reshape(-1)
    onehot = (flat_e[:, None] == jnp.arange(n_exp, dtype=jnp.int32)[None, :]).astype(jnp.int32)
    csum = jnp.cumsum(onehot, axis=0)
    rank = jnp.sum(csum * onehot, axis=1) - 1
    counts = csum[-1]
    padded = (counts + tm - 1) // tm * tm
    pad_end = jnp.cumsum(padded)
    pad_start = pad_end - padded
    dest = pad_start[flat_e] + rank
    n_used = (pad_end[-1] // tm).astype(jnp.int32).reshape(1)
    block_start = jnp.arange(n_blocks, dtype=jnp.int32) * tm
    block_e = jnp.minimum(jnp.sum((pad_end[None, :] <= block_start[:, None]).astype(jnp.int32), axis=1), n_exp - 1)
    n_pos = (n_blocks + 2) * tm
    row_of_pos = jnp.full((n_pos,), -1, jnp.int32).at[dest].set(jnp.arange(n_rows, dtype=jnp.int32))
    pos = jnp.arange(n_pos, dtype=jnp.int32)
    src_rows = jnp.where(row_of_pos >= 0, row_of_pos // top_k, 0).reshape(n_blocks + 2, tm)
    dst_rows = jnp.where(row_of_pos >= 0, row_of_pos, n_rows + pos).reshape(n_blocks + 2, tm)
    return src_rows, dst_rows, block_e, n_used, n_rows + n_pos


def _ffn_post_kernel(x1_ref, rows_ref, gate_ref, mod_ref, g_ref, o_ref):
    d = x1_ref.shape[2]
    gates = gate_ref[0]
    rows = rows_ref[...]
    ffn = gates[:, 0:1] * rows[:, 0:d]
    for k in range(1, TOP_K):
        ffn = ffn + gates[:, k:k + 1] * rows[:, k * d:(k + 1) * d]
    mod = mod_ref[0]
    o_ref[0] = x1_ref[0] + mod[5:6] * _rms(ffn, g_ref[...])


def _ffn_post(x1, rows_flat, gates, mod, g, tok_offset):
    bsz, seq, d = x1.shape
    tm = _tile(seq, 256)
    assert tok_offset % tm == 0
    blk0 = tok_offset // tm
    per_b = seq // tm
    row = pl.BlockSpec((1, tm, d), lambda b, i: (b, i, 0))
    return pl.pallas_call(
        _ffn_post_kernel,
        grid=(bsz, per_b),
        in_specs=[
            row,
            pl.BlockSpec((tm, TOP_K * d), lambda b, i: (blk0 + b * per_b + i, 0)),
            pl.BlockSpec((1, tm, LANES), lambda b, i: (b, i, 0)),
            pl.BlockSpec((1, 6, d), lambda b, i: (b, 0, 0)),
            pl.BlockSpec((1, d), lambda b, i: (0, 0)),
        ],
        out_specs=row,
        out_shape=jax.ShapeDtypeStruct((bsz, seq, d), F32),
        compiler_params=_params(("parallel", "parallel")),
        name="ffn_post",
    )(x1, rows_flat, gates, mod, g)


def _rot_half_cols(w, head_dim):
    d = w.shape[0]
    w3 = w.reshape(d, -1, head_dim)
    half = head_dim // 2
    return jnp.concatenate([-w3[..., half:], w3[..., :half]], axis=-1).reshape(d, -1)


def _rope_tables(pos, head_dim):
    half = head_dim // 2
    inv = ROPE_BASE ** (-jnp.arange(half, dtype=F32) / half)
    ang = pos.astype(F32)[:, None] * inv[None, :]
    cos = jnp.cos(ang)
    sin = jnp.sin(ang)
    return jnp.concatenate([cos, cos], axis=1), jnp.concatenate([sin, sin], axis=1)


class _Dims:
    def __init__(self, state_ret, cache_ckv, cache_kpe, lru_w_a, w_uk, w_uv, conv_w, w_router, w_down):
        _, _, self.ret_heads, self.ret_dk, self.ret_dv = state_ret.shape
        self.kv_lora = cache_ckv.shape[-1]
        self.qk_rope = cache_kpe.shape[-1]
        self.mix_a = conv_w.shape[-1]
        self.conv_w = conv_w.shape[1]
        self.mla_heads, self.qk_nope = w_uk.shape[2], w_uk.shape[3]
        self.v_head = w_uv.shape[3]
        self.n_exp = w_router.shape[-1]
        self.mix_b = self.ret_heads * self.ret_dv
        self.ret_qk = self.ret_heads * self.ret_dk


def _layer_weights(l, dm, w_in, conv_w, conv_b, lru_w_a, lru_b_a, lru_w_x, lru_b_x, lru_lam, g_q_norm, w_uq,
                   g_kv_norm, w_uk, w_uv, w_out, w_router, b_router, g_mix_pre, g_mix_post, g_ffn_pre, g_ffn_post):
    d = w_in.shape[1]
    q_lora = g_q_norm.shape[1]
    sizes = (dm.mix_a, dm.mix_a, dm.ret_qk, dm.ret_qk, dm.mix_b, dm.mix_b, q_lora, dm.kv_lora, dm.qk_rope)
    offs = np.concatenate([[0], np.cumsum(sizes)])
    wl = w_in[l]
    seg = [wl[:, offs[i]:offs[i + 1]] for i in range(len(sizes))]
    w_xa, w_ya, w_qr, w_kr, w_vr, w_gr, w_ql, w_kvl, w_kpe = seg
    blk = dm.mix_b
    kv_cols = dm.kv_lora + 2 * dm.qk_rope
    kv_pad = (-kv_cols) % blk
    cols = [w_xa, w_ya, jnp.concatenate([w_qr, w_kr], axis=1), w_vr, w_gr, w_ql,
            jnp.concatenate([w_kvl, w_kpe, _rot_half_cols(w_kpe, dm.qk_rope), jnp.zeros((d, kv_pad), F32)], axis=1),
            jnp.concatenate([_rot_half_cols(w_qr, dm.ret_dk), _rot_half_cols(w_kr, dm.ret_dk)], axis=1)]
    assert all(c.shape[1] == blk for c in cols), [c.shape for c in cols]
    w_ext = jnp.concatenate(cols, axis=1).astype(BF16)

    n_lru = lru_w_a.shape[1]
    eye = jnp.eye(n_lru, dtype=F32)
    block_diag = lambda w: jnp.einsum("nde,nm->ndme", w, eye).reshape(dm.mix_a, dm.mix_a).astype(BF16)

    qk_dim = dm.qk_nope + dm.qk_rope
    wq3 = w_uq[l].reshape(q_lora, dm.mla_heads, qk_dim)
    pe = wq3[..., dm.qk_nope:]
    half = dm.qk_rope // 2
    pe_rot = jnp.concatenate([-pe[..., half:], pe[..., :half]], axis=-1)
    w_uq_ext = jnp.concatenate([wq3, pe_rot], axis=-1).transpose(1, 0, 2).astype(BF16)

    wo = w_out[l].astype(BF16)
    n_exp = dm.n_exp
    wr = jnp.concatenate([w_router[l], jnp.zeros((d, LANES - n_exp), F32)], axis=1)
    wr_hi = wr.astype(BF16)
    wr_lo = (wr - wr_hi.astype(F32)).astype(BF16)
    br = jnp.concatenate([b_router[l], jnp.full((LANES - n_exp,), NEG_BIG, F32)]).reshape(1, LANES)
    row = lambda v: v[l].reshape(1, -1)
    return dict(
        w_ext=w_ext, conv_w=conv_w[l], conv_b=row(conv_b),
        wa_bd=block_diag(lru_w_a[l]), wx_bd=block_diag(lru_w_x[l]),
        ba=row(lru_b_a), bx=row(lru_b_x), lam=row(lru_lam),
        g_q=row(g_q_norm), w_uq_ext=w_uq_ext, g_kv=row(g_kv_norm),
        w_uk=w_uk[l].transpose(1, 0, 2).astype(BF16), w_uv=w_uv[l].transpose(1, 0, 2).astype(BF16),
        wo_a=wo[:dm.mix_a], wo_b=wo[dm.mix_a:dm.mix_a + dm.mix_b], wo_c=wo[dm.mix_a + dm.mix_b:],
        wr_hi=wr_hi, wr_lo=wr_lo, br=br,
        g_mix_pre=row(g_mix_pre), g_mix_post=row(g_mix_post), g_ffn_pre=row(g_ffn_pre), g_ffn_post=row(g_ffn_post),
    )


def _mixer(x, mod, wl, dm, conv_buf, h0, s0, past_ckv, past_kpe):
    bsz, seq, d = x.shape
    past_len = 0 if past_ckv is None else past_ckv.shape[1]
    pos = past_len + jnp.arange(seq, dtype=jnp.int32)
    u = _in_proj(x, mod, wl["g_mix_pre"], wl["w_ext"])

    ctx_rows = 8
    n_ctx = dm.conv_w - 1
    conv_ctx = jnp.concatenate([jnp.zeros((bsz, ctx_rows - n_ctx, dm.mix_a), F32), conv_buf], axis=1)
    a_out, conv_new, h_new = _rg_lru(u, conv_ctx, h0.reshape(bsz, 1, dm.mix_a), wl["conv_w"], wl["conv_b"],
                                     wl["wa_bd"], wl["wx_bd"], wl["ba"], wl["bx"], wl["lam"])

    cos_k, sin_k = _rope_tables(pos, dm.ret_dk)
    k_scale = float(dm.ret_dk) ** -0.5
    scale_row = jnp.concatenate([jnp.ones((dm.ret_qk,), F32), jnp.full((dm.ret_qk,), k_scale, F32)])[None, :]
    cos_t = jnp.tile(cos_k, (1, 2 * dm.ret_heads)) * scale_row
    sin_t = jnp.tile(sin_k, (1, 2 * dm.ret_heads)) * scale_row
    b_out, s_new = _retention(u, s0, cos_t, sin_t, blocks=(2, 3, 4, 7))

    cos_r, sin_r = _rope_tables(pos, dm.qk_rope)
    lane_pad = lambda t, left, right, fill: jnp.concatenate(
        [jnp.full((seq, left), fill, F32), t, jnp.zeros((seq, right), F32)], axis=1)
    ckv_new, kpe_new = _latent(u, wl["g_kv"], lane_pad(cos_r, 0, LANES - dm.qk_rope, 0.0),
                               lane_pad(sin_r, 0, LANES - dm.qk_rope, 0.0), kv_blk=6, blk_w=dm.mix_b, rope_w=dm.qk_rope)
    q = _q_proj(u, wl["g_q"], wl["w_uq_ext"], lane_pad(cos_r, dm.qk_nope, dm.qk_rope, 1.0),
                lane_pad(sin_r, dm.qk_nope, dm.qk_rope, 0.0), ql_blk=5, rope_w=dm.qk_rope)
    if past_ckv is None:
        ckv_all, kpe_all = ckv_new, kpe_new
    else:
        past_kpe_pad = jnp.concatenate([past_kpe, jnp.zeros(past_kpe.shape[:2] + (LANES - dm.qk_rope,), F32)], axis=2)
        ckv_all = jnp.concatenate([past_ckv, ckv_new], axis=1)
        kpe_all = jnp.concatenate([past_kpe_pad, kpe_new], axis=1)
    k, v = _kv_proj(ckv_all, kpe_all, wl["w_uk"], wl["w_uv"])
    c_out = _attention(q, k, v, past_len, dm.qk_nope + dm.qk_rope)

    x1, h2, gates, idx = _out_proj(a_out, b_out, c_out, x, mod, wl["g_mix_post"], wl["g_ffn_pre"],
                                   wl["wo_a"], wl["wo_b"], wl["wo_c"], wl["wr_hi"], wl["wr_lo"], wl["br"])
    states = (ckv_new, kpe_new[:, :, :dm.qk_rope], conv_new[:, ctx_rows - n_ctx:], h_new.reshape(bsz, dm.mix_a), s_new)
    return x1, h2, gates, idx[:, :, :TOP_K], states


def kernel(x_prompt, x_sample, c_prompt, c_sample, cache_ckv, cache_kpe, state_conv, state_lru, state_ret, w_ada, b_ada, g_mix_pre, g_mix_post, g_ffn_pre, g_ffn_post, w_in, conv_w, conv_b, lru_w_a, lru_b_a, lru_w_x, lru_b_x, lru_lam, g_q_norm, w_uq, g_kv_norm, w_uk, w_uv, w_out, w_router, b_router, w_gate_up, b_gate_up, w_down, b_down):
    depth = w_in.shape[0]
    dm = _Dims(state_ret, cache_ckv, cache_kpe, lru_w_a, w_uk, w_uv, conv_w, w_router, w_down)
    bp, lp, d = x_prompt.shape
    bs, ls, _ = x_sample.shape
    n_tok = bp * lp + bs * ls
    moe_tm = min(1024, max(16, 1 << int(np.log2((n_tok * TOP_K) // dm.n_exp))))
    n_blocks = -(-(n_tok * TOP_K) // moe_tm) + dm.n_exp

    xp, xs = x_prompt, x_sample
    p_states, s_states = [], []
    for l in range(depth):
        wl = _layer_weights(l, dm, w_in, conv_w, conv_b, lru_w_a, lru_b_a, lru_w_x, lru_b_x, lru_lam, g_q_norm, w_uq,
                            g_kv_norm, w_uk, w_uv, w_out, w_router, b_router, g_mix_pre, g_mix_post, g_ffn_pre, g_ffn_post)
        mod_p = _ada_mod(c_prompt, w_ada, b_ada, l).reshape(bp, 6, d)
        mod_s = _ada_mod(c_sample, w_ada, b_ada, l).reshape(bs, 6, d)
        zeros = lambda *s: jnp.zeros(s, F32)
        x1p, h2p, gp, ip, st_p = _mixer(
            xp, mod_p, wl, dm, zeros(bp, dm.conv_w - 1, dm.mix_a), zeros(bp, dm.mix_a),
            zeros(bp, dm.ret_heads, dm.ret_dk, dm.ret_dv), None, None)
        x1s, h2s, gs, is_, st_s = _mixer(
            xs, mod_s, wl, dm, state_conv[l], state_lru[l], state_ret[l], cache_ckv[l], cache_kpe[l])
        p_states.append(st_p)
        s_states.append(st_s)

        h2 = jnp.concatenate([h2p.reshape(-1, d), h2s.reshape(-1, d)], axis=0)
        top_idx = jnp.concatenate([ip.reshape(-1, TOP_K), is_.reshape(-1, TOP_K)], axis=0)
        src_rows, dst_rows, block_e, n_used, n_out_rows = _route(top_idx, dm.n_exp, moe_tm, n_blocks)
        rows = _moe_ffn(h2, src_rows, dst_rows, block_e, n_used, w_gate_up, b_gate_up, w_down, b_down, l, n_out_rows)
        rows_flat = rows.reshape(n_out_rows // TOP_K, TOP_K * d)
        xp = _ffn_post(x1p, rows_flat, gp, mod_p, wl["g_ffn_post"], 0)
        xs = _ffn_post(x1s, rows_flat, gs, mod_s, wl["g_ffn_post"], bp * lp)

    stack = lambda sts: tuple(jnp.stack(t, axis=0) for t in zip(*sts))
    p_ckv, p_kpe, p_conv, p_lru, p_ret = stack(p_states)
    s_ckv, s_kpe, s_conv, s_lru, s_ret = stack(s_states)
    return (xp, xs, p_ckv, p_kpe, p_conv, p_lru, p_ret, s_ckv, s_kpe, s_conv, s_lru, s_ret)
```

```python
import functools

import numpy as np
import jax
import jax.numpy as jnp
from jax import lax
from jax.experimental import pallas as pl
from jax.experimental.pallas import tpu as pltpu

CHUNK = 64
EPS = 1e-6
ROPE_BASE = 10000.0
RG_C = 8.0
TOP_K = 4
SWIGLU_LIMIT = 7.0
SWIGLU_ALPHA = 1.702
NEG_BIG = -1e30
GELU_C = float(np.sqrt(2.0 / np.pi))

LANES = 128
VMEM_LIMIT_BYTES = 56 * 1024 * 1024

F32 = jnp.float32
BF16 = jnp.bfloat16


def _params(semantics):
    return pltpu.CompilerParams(dimension_semantics=semantics, vmem_limit_bytes=VMEM_LIMIT_BYTES)


def _tile(n, pref):
    if n <= pref:
        return n
    t = pref
    while n % t:
        t //= 2
    return t


def _rms(x, g):
    return x * lax.rsqrt(jnp.mean(x * x, axis=-1, keepdims=True) + EPS) * g


def _dot(a, b):
    return jnp.dot(a, b, preferred_element_type=F32)


def _dot_nt(a, b):
    return lax.dot_general(a, b, (((1,), (1,)), ((), ())), preferred_element_type=F32)


def _dot_tn(a, b):
    return lax.dot_general(a, b, (((0,), (0,)), ((), ())), preferred_element_type=F32)


def _sigmoid(x):
    return 1.0 / (1.0 + jnp.exp(-x))


def _ada_kernel(c_ref, w_ref, b_ref, o_ref):
    c = c_ref[...]
    s = (c * _sigmoid(c)).astype(BF16)
    o_ref[...] = _dot(s, w_ref[...].astype(BF16)) + b_ref[...]


def _ada_mod(c, w_ada, b_ada, layer):
    bsz, d = c.shape
    n = w_ada.shape[-1]
    tn = _tile(n, 1024)
    return pl.pallas_call(
        _ada_kernel,
        grid=(n // tn,),
        in_specs=[
            pl.BlockSpec((bsz, d), lambda j: (0, 0)),
            pl.BlockSpec((None, d, tn), lambda j: (layer, 0, j)),
            pl.BlockSpec((None, 1, tn), lambda j: (layer, 0, j)),
        ],
        out_specs=pl.BlockSpec((bsz, tn), lambda j: (0, j)),
        out_shape=jax.ShapeDtypeStruct((bsz, n), F32),
        compiler_params=_params(("arbitrary",)),
        name="ada_mod",
    )(c, w_ada, b_ada.reshape(b_ada.shape[0], 1, n))


def _in_proj_kernel(x_ref, mod_ref, g_ref, w_ref, u_ref):
    mod = mod_ref[0]
    h = _rms(x_ref[0], g_ref[...]) * (1.0 + mod[1:2]) + mod[0:1]
    u_ref[0] = _dot(h.astype(BF16), w_ref[...])


def _in_proj(x, mod, g, w_ext):
    bsz, seq, d = x.shape
    n = w_ext.shape[1]
    tm = _tile(seq, 256)
    return pl.pallas_call(
        _in_proj_kernel,
        grid=(bsz, seq // tm),
        in_specs=[
            pl.BlockSpec((1, tm, d), lambda b, i: (b, i, 0)),
            pl.BlockSpec((1, 6, d), lambda b, i: (b, 0, 0)),
            pl.BlockSpec((1, d), lambda b, i: (0, 0)),
            pl.BlockSpec((d, n), lambda b, i: (0, 0)),
        ],
        out_specs=pl.BlockSpec((1, tm, n), lambda b, i: (b, i, 0)),
        out_shape=jax.ShapeDtypeStruct((bsz, seq, n), F32),
        compiler_params=_params(("parallel", "parallel")),
        name="in_proj",
    )(x, mod, g, w_ext)


def _lru_kernel(xa_ref, ya_ref, cbuf_ref, h0_ref, cw_ref, cb_ref, wa_ref, wx_ref, ba_ref, bx_ref,
                lam_ref, out_ref, cnew_ref, hlast_ref, xbuf, hcar):
    tl = xa_ref.shape[1]
    width = xa_ref.shape[2]
    ctx = xbuf.shape[0] - tl

    @pl.when(pl.program_id(1) == 0)
    def _():
        xbuf[0:ctx, :] = cbuf_ref[0]
        hcar[...] = h0_ref[0]

    xa = xa_ref[0]
    xbuf[ctx:ctx + tl, :] = xa
    cw = cw_ref[...]
    n_tap = cw.shape[0]
    xc = cb_ref[...] + cw[n_tap - 1:n_tap] * xa
    for k in range(n_tap - 1):
        off = ctx - (n_tap - 1) + k
        xc = xc + cw[k:k + 1] * xbuf[off:off + tl, :]
    new_ctx = xbuf[tl:tl + ctx, :]
    xbuf[0:ctx, :] = new_ctx
    cnew_ref[0] = new_ctx

    xcb = xc.astype(BF16)
    r = _sigmoid(_dot(xcb, wa_ref[...]) + ba_ref[...])
    gi = _sigmoid(_dot(xcb, wx_ref[...]) + bx_ref[...])
    z = -lam_ref[...]
    softplus = jnp.maximum(z, 0.0) + jnp.log1p(jnp.exp(-jnp.abs(z)))
    log_a = (-RG_C) * r * softplus
    a = jnp.exp(log_a)
    b = jnp.sqrt(1.0 - jnp.exp(2.0 * log_a)) * gi * xc

    row = lax.broadcasted_iota(jnp.int32, (tl, width), 0)
    s = 1
    while s < tl:
        a_sh = pltpu.roll(a, s, 0)
        b_sh = pltpu.roll(b, s, 0)
        valid = row >= s
        b = jnp.where(valid, a * b_sh + b, b)
        a = jnp.where(valid, a * a_sh, a)
        s *= 2
    h = a * hcar[...] + b
    h_last = h[tl - 1:tl, :]
    hcar[...] = h_last
    hlast_ref[0] = h_last

    ya = ya_ref[0]
    gelu = 0.5 * ya * (1.0 + jnp.tanh(GELU_C * (ya + 0.044715 * (ya * ya * ya))))
    out_ref[0] = (h * gelu).astype(out_ref.dtype)


def _rg_lru(u, conv_ctx, h0, cw, cb, wa_bd, wx_bd, ba, bx, lam):
    bsz, seq, _ = u.shape
    width = cw.shape[1]
    tl = _tile(seq, 512)
    ctx = conv_ctx.shape[1]
    xa_blk = 0
    ya_blk = 1
    vec = lambda: pl.BlockSpec((1, width), lambda b, t: (0, 0))
    return pl.pallas_call(
        _lru_kernel,
        grid=(bsz, seq // tl),
        in_specs=[
            pl.BlockSpec((1, tl, width), lambda b, t: (b, t, xa_blk)),
            pl.BlockSpec((1, tl, width), lambda b, t: (b, t, ya_blk)),
            pl.BlockSpec((1, ctx, width), lambda b, t: (b, 0, 0)),
            pl.BlockSpec((1, 1, width), lambda b, t: (b, 0, 0)),
            pl.BlockSpec(cw.shape, lambda b, t: (0, 0)),
            vec(),
            pl.BlockSpec((width, width), lambda b, t: (0, 0)),
            pl.BlockSpec((width, width), lambda b, t: (0, 0)),
            vec(), vec(), vec(),
        ],
        out_specs=[
            pl.BlockSpec((1, tl, width), lambda b, t: (b, t, 0)),
            pl.BlockSpec((1, ctx, width), lambda b, t: (b, 0, 0)),
            pl.BlockSpec((1, 1, width), lambda b, t: (b, 0, 0)),
        ],
        out_shape=[
            jax.ShapeDtypeStruct((bsz, seq, width), BF16),
            jax.ShapeDtypeStruct((bsz, ctx, width), F32),
            jax.ShapeDtypeStruct((bsz, 1, width), F32),
        ],
        scratch_shapes=[pltpu.VMEM((tl + ctx, width), F32), pltpu.VMEM((1, width), F32)],
        compiler_params=_params(("parallel", "arbitrary")),
        name="rg_lru",
    )(u, u, conv_ctx, h0, cw, cb, wa_bd, wx_bd, ba, bx, lam)


def _ret_kernel(qk_ref, rot_ref, v_ref, g_ref, cos_ref, sin_ref, dmask_ref, qdec_ref, kdec_ref,
                s0_ref, out_ref, slast_ref, s_sc, *, gchunk):
    n_heads, dk, dv = s_sc.shape
    hdk = n_heads * dk

    @pl.when(pl.program_id(1) == 0)
    def _():
        s_sc[...] = s0_ref[0]

    qkr = qk_ref[0] * cos_ref[...] + rot_ref[0] * sin_ref[...]
    v = v_ref[0]
    vdec = v * kdec_ref[...]
    gate = g_ref[0]
    qdec = qdec_ref[...]
    for h in range(n_heads):
        q = qkr[:, h * dk:(h + 1) * dk].astype(BF16)
        k = qkr[:, hdk + h * dk:hdk + (h + 1) * dk].astype(BF16)
        vh = v[:, h * dv:(h + 1) * dv].astype(BF16)
        vd = vdec[:, h * dv:(h + 1) * dv].astype(BF16)
        s_prev = s_sc[h]
        scores = _dot_nt(q, k) * dmask_ref[h]
        o = _dot(scores.astype(BF16), vh) + qdec[:, h * dv:(h + 1) * dv] * _dot(q, s_prev.astype(BF16))
        s_sc[h] = s_prev * gchunk[h] + _dot_tn(k, vd)
        o = o * lax.rsqrt(jnp.mean(o * o, axis=-1, keepdims=True) + EPS)
        gh = gate[:, h * dv:(h + 1) * dv]
        out_ref[0, :, h * dv:(h + 1) * dv] = (o * (gh * _sigmoid(gh))).astype(out_ref.dtype)
    slast_ref[0] = s_sc[...]


def _retention(u, s0, cos_t, sin_t, blocks):
    bsz, seq, _ = u.shape
    _, n_heads, dk, dv = s0.shape
    width = n_heads * dv
    tc = _tile(seq, 256)
    log_g = np.log1p(-np.exp2(-5.0 - np.arange(n_heads, dtype=np.float64)))
    idx = np.arange(tc, dtype=np.float64)
    diff = idx[:, None] - idx[None, :]
    dmask = np.where(diff[None] >= 0, np.exp(np.maximum(diff, 0.0)[None] * log_g[:, None, None]), 0.0)
    qdec = np.repeat(np.exp((idx + 1.0)[:, None] * log_g[None, :]), dv, axis=1)
    kdec = np.repeat(np.exp((tc - 1.0 - idx)[:, None] * log_g[None, :]), dv, axis=1)
    gchunk = tuple(float(g) for g in np.exp(tc * log_g))
    qk_blk, v_blk, g_blk, rot_blk = blocks
    u_spec = lambda blk: pl.BlockSpec((1, tc, width), lambda b, n: (b, n, blk))
    full2 = lambda a: pl.BlockSpec(a.shape, lambda b, n: (0, 0))
    return pl.pallas_call(
        functools.partial(_ret_kernel, gchunk=gchunk),
        grid=(bsz, seq // tc),
        in_specs=[
            u_spec(qk_blk), u_spec(rot_blk), u_spec(v_blk), u_spec(g_blk),
            pl.BlockSpec((tc, width), lambda b, n: (n, 0)),
            pl.BlockSpec((tc, width), lambda b, n: (n, 0)),
            pl.BlockSpec(dmask.shape, lambda b, n: (0, 0, 0)),
            full2(qdec), full2(kdec),
            pl.BlockSpec((1, n_heads, dk, dv), lambda b, n: (b, 0, 0, 0)),
        ],
        out_specs=[
            pl.BlockSpec((1, tc, width), lambda b, n: (b, n, 0)),
            pl.BlockSpec((1, n_heads, dk, dv), lambda b, n: (b, 0, 0, 0)),
        ],
        out_shape=[
            jax.ShapeDtypeStruct((bsz, seq, width), BF16),
            jax.ShapeDtypeStruct((bsz, n_heads, dk, dv), F32),
        ],
        scratch_shapes=[pltpu.VMEM((n_heads, dk, dv), F32)],
        compiler_params=_params(("parallel", "arbitrary")),
        name="retention",
    )(u, u, u, u, cos_t, sin_t, jnp.asarray(dmask, F32), jnp.asarray(qdec, F32),
      jnp.asarray(kdec, F32), s0)


def _latent_kernel(kv_ref, g_ref, cos_ref, sin_ref, ckv_ref, kpe_ref, *, rope_w):
    lora = ckv_ref.shape[2]
    blk = kv_ref[0]
    ckv_ref[0] = _rms(blk[:, :lora], g_ref[...])
    kp = blk[:, lora:lora + LANES]
    rolled = pltpu.roll(kp, LANES - rope_w, 1)
    kpe_ref[0] = kp * cos_ref[...] + rolled * sin_ref[...]


def _latent(u, g_kv, cos_p, sin_p, kv_blk, blk_w, rope_w):
    bsz, seq, _ = u.shape
    lora = g_kv.shape[1]
    tm = _tile(seq, 512)
    return pl.pallas_call(
        functools.partial(_latent_kernel, rope_w=rope_w),
        grid=(bsz, seq // tm),
        in_specs=[
            pl.BlockSpec((1, tm, blk_w), lambda b, i: (b, i, kv_blk)),
            pl.BlockSpec((1, lora), lambda b, i: (0, 0)),
            pl.BlockSpec((tm, LANES), lambda b, i: (i, 0)),
            pl.BlockSpec((tm, LANES), lambda b, i: (i, 0)),
        ],
        out_specs=[
            pl.BlockSpec((1, tm, lora), lambda b, i: (b, i, 0)),
            pl.BlockSpec((1, tm, LANES), lambda b, i: (b, i, 0)),
        ],
        out_shape=[
            jax.ShapeDtypeStruct((bsz, seq, lora), F32),
            jax.ShapeDtypeStruct((bsz, seq, LANES), F32),
        ],
        compiler_params=_params(("parallel", "parallel")),
        name="latent_norm_rope",
    )(u, g_kv, cos_p, sin_p)


def _q_proj_kernel(ql_ref, g_ref, w_ref, cos_ref, sin_ref, q_ref, *, rope_w):
    cq = _rms(ql_ref[0], g_ref[...]).astype(BF16)
    cos = cos_ref[...]
    sin = sin_ref[...]
    for h in range(w_ref.shape[0]):
        r = _dot(cq, w_ref[h])
        rolled = pltpu.roll(r, r.shape[1] - rope_w, 1)
        q_ref[0, h] = (r * cos + rolled * sin).astype(q_ref.dtype)


def _q_proj(u, g_q, w_uq_ext, cos_q, sin_q, ql_blk, rope_w):
    bsz, seq, _ = u.shape
    n_heads, q_lora, width = w_uq_ext.shape
    tm = _tile(seq, 512)
    return pl.pallas_call(
        functools.partial(_q_proj_kernel, rope_w=rope_w),
        grid=(bsz, seq // tm),
        in_specs=[
            pl.BlockSpec((1, tm, q_lora), lambda b, i: (b, i, ql_blk)),
            pl.BlockSpec((1, q_lora), lambda b, i: (0, 0)),
            pl.BlockSpec((n_heads, q_lora, width), lambda b, i: (0, 0, 0)),
            pl.BlockSpec((tm, width), lambda b, i: (i, 0)),
            pl.BlockSpec((tm, width), lambda b, i: (i, 0)),
        ],
        out_specs=pl.BlockSpec((1, n_heads, tm, width), lambda b, i: (b, 0, i, 0)),
        out_shape=jax.ShapeDtypeStruct((bsz, n_heads, seq, width), BF16),
        compiler_params=_params(("parallel", "parallel")),
        name="mla_q_proj",
    )(u, g_q, w_uq_ext, cos_q, sin_q)


def _kv_proj_kernel(ckv_ref, kpe_ref, wk_ref, wv_ref, k_ref, v_ref):
    ckv = ckv_ref[0].astype(BF16)
    kpe = kpe_ref[0].astype(k_ref.dtype)
    nope = wk_ref.shape[2]
    for h in range(wk_ref.shape[0]):
        k_ref[0, h, :, :nope] = _dot(ckv, wk_ref[h]).astype(k_ref.dtype)
        k_ref[0, h, :, nope:] = kpe
        v_ref[0, h] = _dot(ckv, wv_ref[h]).astype(v_ref.dtype)


def _kv_proj(ckv, kpe, w_uk, w_uv):
    bsz, klen, lora = ckv.shape
    n_heads, _, nope = w_uk.shape
    vdim = w_uv.shape[2]
    tm = _tile(klen, 512) if klen % 512 == 0 else klen
    return pl.pallas_call(
        _kv_proj_kernel,
        grid=(bsz, klen // tm),
        in_specs=[
            pl.BlockSpec((1, tm, lora), lambda b, i: (b, i, 0)),
            pl.BlockSpec((1, tm, LANES), lambda b, i: (b, i, 0)),
            pl.BlockSpec((n_heads, lora, nope), lambda b, i: (0, 0, 0)),
            pl.BlockSpec((n_heads, lora, vdim), lambda b, i: (0, 0, 0)),
        ],
        out_specs=[
            pl.BlockSpec((1, n_heads, tm, nope + LANES), lambda b, i: (b, 0, i, 0)),
            pl.BlockSpec((1, n_heads, tm, vdim), lambda b, i: (b, 0, i, 0)),
        ],
        out_shape=[
            jax.ShapeDtypeStruct((bsz, n_heads, klen, nope + LANES), BF16),
            jax.ShapeDtypeStruct((bsz, n_heads, klen, vdim), BF16),
        ],
        compiler_params=_params(("parallel", "parallel")),
        name="mla_kv_proj",
    )(ckv, kpe, w_uk, w_uv)


def _attn_kernel(q_ref, k_ref, v_ref, o_ref, m_sc, l_sc, acc_sc, *, q_pos0, scale, tk):
    qi = pl.program_id(2)
    tq = q_ref.shape[2]
    nk = k_ref.shape[2] // tk
    q = q_ref[0, 0]
    m_sc[...] = jnp.full(m_sc.shape, NEG_BIG, F32)
    l_sc[...] = jnp.zeros(l_sc.shape, F32)
    acc_sc[...] = jnp.zeros(acc_sc.shape, F32)

    def step(j, masked):
        start = pl.multiple_of(j * tk, tk)
        s = _dot_nt(q, k_ref[0, 0, pl.ds(start, tk), :]) * scale
        if masked:
            q_chunk = (q_pos0 + qi * tq + lax.broadcasted_iota(jnp.int32, (tq, tk), 0)) // CHUNK
            k_chunk = (start + lax.broadcasted_iota(jnp.int32, (tq, tk), 1)) // CHUNK
            s = jnp.where(k_chunk <= q_chunk, s, NEG_BIG)
        m_prev = m_sc[...]
        m_new = jnp.maximum(m_prev, jnp.max(s, axis=-1, keepdims=True))
        alpha = jnp.exp(m_prev - m_new)
        p = jnp.exp(s - m_new)
        l_sc[...] = alpha * l_sc[...] + jnp.sum(p, axis=-1, keepdims=True)
        acc_sc[...] = alpha * acc_sc[...] + _dot(p.astype(BF16), v_ref[0, 0, pl.ds(start, tk), :])
        m_sc[...] = m_new

    first_q_chunk = (q_pos0 + qi * tq) // CHUNK
    last_q_chunk = (q_pos0 + qi * tq + tq - 1) // CHUNK
    n_full = jnp.minimum(((first_q_chunk + 1) * CHUNK) // tk, nk)
    n_seen = jnp.minimum((last_q_chunk * CHUNK + CHUNK - 1) // tk + 1, nk)
    lax.fori_loop(0, n_full, lambda j, c: (step(j, False), c)[1], 0)
    lax.fori_loop(n_full, n_seen, lambda j, c: (step(j, True), c)[1], 0)
    o_ref[0] = (acc_sc[...] / l_sc[...]).astype(o_ref.dtype)


def _attention(q, k, v, q_pos0, qk_dim):
    bsz, n_heads, seq, width = q.shape
    klen = k.shape[2]
    vdim = v.shape[3]
    tq = _tile(seq, 512)
    tk = _tile(klen, 512) if klen % 512 == 0 else klen
    return pl.pallas_call(
        functools.partial(_attn_kernel, q_pos0=q_pos0, scale=float(qk_dim) ** -0.5, tk=tk),
        grid=(bsz, n_heads, seq // tq),
        in_specs=[
            pl.BlockSpec((1, 1, tq, width), lambda b, h, qi: (b, h, qi, 0)),
            pl.BlockSpec((1, 1, klen, width), lambda b, h, qi: (b, h, 0, 0)),
            pl.BlockSpec((1, 1, klen, vdim), lambda b, h, qi: (b, h, 0, 0)),
        ],
        out_specs=pl.BlockSpec((1, tq, vdim), lambda b, h, qi: (b, qi, h)),
        out_shape=jax.ShapeDtypeStruct((bsz, seq, n_heads * vdim), BF16),
        scratch_shapes=[pltpu.VMEM((tq, 1), F32), pltpu.VMEM((tq, 1), F32), pltpu.VMEM((tq, vdim), F32)],
        compiler_params=_params(("parallel", "parallel", "arbitrary")),
        name="mla_attention",
    )(q, k, v)


def _out_proj_kernel(a_ref, b_ref, c_ref, x_ref, mod_ref, gpost_ref, gpre_ref, wa_ref, wb_ref, wc_ref,
                     wrh_ref, wrl_ref, br_ref, x1_ref, h2_ref, gate_ref, idx_ref):
    mix = _dot(a_ref[0], wa_ref[...]) + _dot(b_ref[0], wb_ref[...]) + _dot(c_ref[0], wc_ref[...])
    mod = mod_ref[0]
    x1 = x_ref[0] + mod[2:3] * _rms(mix, gpost_ref[...])
    x1_ref[0] = x1
    h2 = _rms(x1, gpre_ref[...]) * (1.0 + mod[4:5]) + mod[3:4]
    h2_ref[0] = h2
    hi = h2.astype(BF16)
    lo = (h2 - hi.astype(F32)).astype(BF16)
    logits = _dot(hi, wrh_ref[...]) + (_dot(hi, wrl_ref[...]) + _dot(lo, wrh_ref[...])) + br_ref[...]
    lane = lax.broadcasted_iota(jnp.int32, logits.shape, 1)
    lane_f = lane.astype(F32)
    gates = jnp.zeros(logits.shape, F32)
    idxs = jnp.zeros(logits.shape, F32)
    top0 = None
    denom = None
    for k in range(TOP_K):
        m = jnp.max(logits, axis=-1, keepdims=True)
        sel = jnp.min(jnp.where(logits == m, lane_f, float(LANES)), axis=-1, keepdims=True)
        if k == 0:
            top0 = m
            e = jnp.ones_like(m)
            denom = e
        else:
            e = jnp.exp(m - top0)
            denom = denom + e
        gates = jnp.where(lane == k, e, gates)
        idxs = jnp.where(lane == k, sel, idxs)
        logits = jnp.where(lane_f == sel, NEG_BIG * 2.0, logits)
    gate_ref[0] = gates / denom
    idx_ref[0] = idxs.astype(jnp.int32)


def _out_proj(a_out, b_out, c_out, x, mod, g_post, g_pre, wa, wb, wc, wr_hi, wr_lo, br):
    bsz, seq, d = x.shape
    tm = _tile(seq, 256)
    row = lambda w: pl.BlockSpec((1, tm, w), lambda b, i: (b, i, 0))
    const = lambda a: pl.BlockSpec(a.shape, lambda b, i: (0, 0))
    return pl.pallas_call(
        _out_proj_kernel,
        grid=(bsz, seq // tm),
        in_specs=[
            row(a_out.shape[2]), row(b_out.shape[2]), row(c_out.shape[2]), row(d),
            pl.BlockSpec((1, 6, d), lambda b, i: (b, 0, 0)),
            const(g_post), const(g_pre), const(wa), const(wb), const(wc),
            const(wr_hi), const(wr_lo), const(br),
        ],
        out_specs=[row(d), row(d), row(LANES), row(LANES)],
        out_shape=[
            jax.ShapeDtypeStruct((bsz, seq, d), F32),
            jax.ShapeDtypeStruct((bsz, seq, d), F32),
            jax.ShapeDtypeStruct((bsz, seq, LANES), F32),
            jax.ShapeDtypeStruct((bsz, seq, LANES), jnp.int32),
        ],
        compiler_params=_params(("parallel", "parallel")),
        name="out_proj_router",
    )(a_out, b_out, c_out, x, mod, g_post, g_pre, wa, wb, wc, wr_hi, wr_lo, br)


def _moe_kernel(be_ref, nu_ref, src_ref, dst_ref, h_hbm, wg_ref, wu_ref, bg_ref, bu_ref, wd_ref, bd_ref,
                rows_hbm, xbuf, xb, obuf, gsem, ssem):
    g = pl.program_id(0)
    f = pl.program_id(1)
    nf = pl.num_programs(1)
    n_used = nu_ref[0]
    tm = xbuf.shape[0]
    chunk = tm // nf
    live = jnp.logical_and(g >= 1, g <= n_used)
    slot = (g + 1) % 2

    def gather_copy(r):
        return pltpu.make_async_copy(h_hbm.at[pl.ds(src_ref[0, r], 1)], xbuf.at[pl.ds(r, 1)], gsem)

    def scatter_copy(r, half):
        return pltpu.make_async_copy(obuf.at[half, pl.ds(r, 1)], rows_hbm.at[pl.ds(dst_ref[0, r], 1)], ssem)

    def wait_gather():
        pltpu.make_async_copy(h_hbm.at[pl.ds(0, tm)], xbuf, gsem).wait()

    def wait_scatter(half):
        pltpu.make_async_copy(obuf.at[half], rows_hbm.at[pl.ds(0, tm)], ssem).wait()

    @pl.when(jnp.logical_and(g == 0, f == 0))
    def _():
        obuf[1] = jnp.zeros(obuf.shape[1:], F32)
        lax.fori_loop(0, tm, lambda r, c: (gather_copy(r).start(), c)[1], 0)

    @pl.when(jnp.logical_and(live, f == 0))
    def _():
        wait_gather()
        xb[...] = xbuf[...].astype(BF16)
        obuf[slot] = jnp.broadcast_to(bd_ref[...], obuf.shape[1:])

    @pl.when(live)
    def _():
        for i in range(chunk):
            r = f * chunk + i
            gather_copy(r).start()
            scatter_copy(r, 1 - slot).start(priority=i % 2)
        x = xb[...]
        gate = _dot(x, wg_ref[...].astype(BF16)) + bg_ref[...]
        up = _dot(x, wu_ref[...].astype(BF16)) + bu_ref[...]
        x_glu = jnp.minimum(gate, SWIGLU_LIMIT)
        x_lin = jnp.clip(up, -SWIGLU_LIMIT, SWIGLU_LIMIT)
        act = x_glu * _sigmoid(SWIGLU_ALPHA * x_glu) * (x_lin + 1.0)
        obuf[slot] += _dot(act.astype(BF16), wd_ref[...].astype(BF16))

    @pl.when(jnp.logical_and(live, f == nf - 1))
    def _():
        wait_scatter(1 - slot)

    @pl.when(jnp.logical_and(g == n_used + 1, f == 0))
    def _():
        wait_gather()
        lax.fori_loop(0, tm, lambda r, c: (scatter_copy(r, 1 - slot).start(), c)[1], 0)
        wait_scatter(1 - slot)


def _moe_ffn(h_all, src_rows, dst_rows, block_e, n_used, w_gu, b_gu, w_dn, b_dn, layer, n_out_rows):
    n_grid, tm = src_rows.shape
    d = h_all.shape[1]
    d_exp = w_dn.shape[2]
    tf = _tile(d_exp, 256)
    nf = d_exp // tf
    assert tm % nf == 0 and h_all.shape[0] >= tm and n_out_rows >= tm
    n_exp = w_gu.shape[1]
    b_gu4 = b_gu.reshape(b_gu.shape[0], n_exp, 1, 2 * d_exp)
    b_dn4 = b_dn.reshape(b_dn.shape[0], n_exp, 1, d)

    def f_idx(g, f, nu):
        return jnp.where(jnp.logical_and(g >= 1, g <= nu[0]), f, nf - 1)

    def e_idx(g, be, nu):
        return be[jnp.clip(g - 1, 0, nu[0] - 1)]

    grid_spec = pltpu.PrefetchScalarGridSpec(
        num_scalar_prefetch=2,
        grid=(n_grid, nf),
        in_specs=[
            pl.BlockSpec((None, 1, tm), lambda g, f, be, nu: (g, 0, 0), memory_space=pltpu.SMEM),
            pl.BlockSpec((None, 1, tm), lambda g, f, be, nu: (jnp.maximum(g - 2, 0), 0, 0), memory_space=pltpu.SMEM),
            pl.BlockSpec(memory_space=pl.ANY),
            pl.BlockSpec((None, None, d, tf), lambda g, f, be, nu: (layer, e_idx(g, be, nu), 0, f_idx(g, f, nu))),
            pl.BlockSpec((None, None, d, tf), lambda g, f, be, nu: (layer, e_idx(g, be, nu), 0, nf + f_idx(g, f, nu))),
            pl.BlockSpec((None, None, 1, tf), lambda g, f, be, nu: (layer, e_idx(g, be, nu), 0, f_idx(g, f, nu))),
            pl.BlockSpec((None, None, 1, tf), lambda g, f, be, nu: (layer, e_idx(g, be, nu), 0, nf + f_idx(g, f, nu))),
            pl.BlockSpec((None, None, tf, d), lambda g, f, be, nu: (layer, e_idx(g, be, nu), f_idx(g, f, nu), 0)),
            pl.BlockSpec((None, None, 1, d), lambda g, f, be, nu: (layer, e_idx(g, be, nu), 0, 0)),
        ],
        out_specs=pl.BlockSpec(memory_space=pl.ANY),
        scratch_shapes=[
            pltpu.VMEM((tm, d), F32),
            pltpu.VMEM((tm, d), BF16),
            pltpu.VMEM((2, tm, d), F32),
            pltpu.SemaphoreType.DMA(()),
            pltpu.SemaphoreType.DMA(()),
        ],
    )
    return pl.pallas_call(
        _moe_kernel,
        grid_spec=grid_spec,
        out_shape=jax.ShapeDtypeStruct((n_out_rows, d), F32),
        compiler_params=_params(("arbitrary", "arbitrary")),
        name="moe_expert_ffn",
    )(block_e, n_used, src_rows.reshape(n_grid, 1, tm), dst_rows.reshape(n_grid, 1, tm), h_all,
      w_gu, w_gu, b_gu4, b_gu4, w_dn, b_dn4)


def _route(top_idx, n_exp, tm, n_blocks):
    n_tok, top_k = top_idx.shape
    n_rows = n_tok * top_k
    flat_e = top_idx.reshape(-1)
    onehot = (flat_e[:, None] == jnp.arange(n_exp, dtype=jnp.int32)[None, :]).astype(jnp.int32)
    csum = jnp.cumsum(onehot, axis=0)
    rank = jnp.sum(csum * onehot, axis=1) - 1
    counts = csum[-1]
    padded = (counts + tm - 1) // tm * tm
    pad_end = jnp.cumsum(padded)
    pad_start = pad_end - padded
    dest = pad_start[flat_e] + rank
    n_used = (pad_end[-1] // tm).astype(jnp.int32).reshape(1)
    block_start = jnp.arange(n_blocks, dtype=jnp.int32) * tm
    block_e = jnp.minimum(jnp.sum((pad_end[None, :] <= block_start[:, None]).astype(jnp.int32), axis=1), n_exp - 1)
    n_pos = (n_blocks + 2) * tm
    row_of_pos = jnp.full((n_pos,), -1, jnp.int32).at[dest].set(jnp.arange(n_rows, dtype=jnp.int32))
    pos = jnp.arange(n_pos, dtype=jnp.int32)
    src_rows = jnp.where(row_of_pos >= 0, row_of_pos // top_k, 0).reshape(n_blocks + 2, tm)
    dst_rows = jnp.where(row_of_pos >= 0, (row_of_pos % top_k) * n_tok + row_of_pos // top_k,
                         n_rows + pos).reshape(n_blocks + 2, tm)
    return src_rows, dst_rows, block_e, n_used, n_rows + n_pos


def _ffn_post_kernel(x1_ref, *refs):
    row_refs, (gate_ref, mod_ref, g_ref, o_ref) = refs[:TOP_K], refs[TOP_K:]
    gates = gate_ref[0]
    ffn = gates[:, 0:1] * row_refs[0][...]
    for k in range(1, TOP_K):
        ffn = ffn + gates[:, k:k + 1] * row_refs[k][...]
    mod = mod_ref[0]
    o_ref[0] = x1_ref[0] + mod[5:6] * _rms(ffn, g_ref[...])


def _ffn_post(x1, rows, gates, mod, g, tok_offset, n_tok):
    bsz, seq, d = x1.shape
    tm = _tile(seq, 256)
    assert tok_offset % tm == 0 and n_tok % tm == 0
    per_b = seq // tm
    row = pl.BlockSpec((1, tm, d), lambda b, i: (b, i, 0))

    def choice_spec(k):
        blk0 = (k * n_tok + tok_offset) // tm
        return pl.BlockSpec((tm, d), lambda b, i: (blk0 + b * per_b + i, 0))

    return pl.pallas_call(
        _ffn_post_kernel,
        grid=(bsz, per_b),
        in_specs=[row] + [choice_spec(k) for k in range(TOP_K)] + [
            pl.BlockSpec((1, tm, LANES), lambda b, i: (b, i, 0)),
            pl.BlockSpec((1, 6, d), lambda b, i: (b, 0, 0)),
            pl.BlockSpec((1, d), lambda b, i: (0, 0)),
        ],
        out_specs=row,
        out_shape=jax.ShapeDtypeStruct((bsz, seq, d), F32),
        compiler_params=_params(("parallel", "parallel")),
        name="ffn_post",
    )(x1, *([rows] * TOP_K), gates, mod, g)


def _rot_half_cols(w, head_dim):
    d = w.shape[0]
    w3 = w.reshape(d, -1, head_dim)
    half = head_dim // 2
    return jnp.concatenate([-w3[..., half:], w3[..., :half]], axis=-1).reshape(d, -1)


def _rope_tables(pos, head_dim):
    half = head_dim // 2
    inv = ROPE_BASE ** (-jnp.arange(half, dtype=F32) / half)
    ang = pos.astype(F32)[:, None] * inv[None, :]
    cos = jnp.cos(ang)
    sin = jnp.sin(ang)
    return jnp.concatenate([cos, cos], axis=1), jnp.concatenate([sin, sin], axis=1)


class _Dims:
    def __init__(self, state_ret, cache_ckv, cache_kpe, lru_w_a, w_uk, w_uv, conv_w, w_router, w_down):
        _, _, self.ret_heads, self.ret_dk, self.ret_dv = state_ret.shape
        self.kv_lora = cache_ckv.shape[-1]
        self.qk_rope = cache_kpe.shape[-1]
        self.mix_a = conv_w.shape[-1]
        self.conv_w = conv_w.shape[1]
        self.mla_heads, self.qk_nope = w_uk.shape[2], w_uk.shape[3]
        self.v_head = w_uv.shape[3]
        self.n_exp = w_router.shape[-1]
        self.mix_b = self.ret_heads * self.ret_dv
        self.ret_qk = self.ret_heads * self.ret_dk


def _layer_weights(l, dm, w_in, conv_w, conv_b, lru_w_a, lru_b_a, lru_w_x, lru_b_x, lru_lam, g_q_norm, w_uq,
                   g_kv_norm, w_uk, w_uv, w_out, w_router, b_router, g_mix_pre, g_mix_post, g_ffn_pre, g_ffn_post):
    d = w_in.shape[1]
    q_lora = g_q_norm.shape[1]
    sizes = (dm.mix_a, dm.mix_a, dm.ret_qk, dm.ret_qk, dm.mix_b, dm.mix_b, q_lora, dm.kv_lora, dm.qk_rope)
    offs = np.concatenate([[0], np.cumsum(sizes)])
    wl = w_in[l]
    seg = [wl[:, offs[i]:offs[i + 1]] for i in range(len(sizes))]
    w_xa, w_ya, w_qr, w_kr, w_vr, w_gr, w_ql, w_kvl, w_kpe = seg
    blk = dm.mix_b
    kv_cols = dm.kv_lora + 2 * dm.qk_rope
    kv_pad = (-kv_cols) % blk
    cols = [w_xa, w_ya, jnp.concatenate([w_qr, w_kr], axis=1), w_vr, w_gr, w_ql,
            jnp.concatenate([w_kvl, w_kpe, _rot_half_cols(w_kpe, dm.qk_rope), jnp.zeros((d, kv_pad), F32)], axis=1),
            jnp.concatenate([_rot_half_cols(w_qr, dm.ret_dk), _rot_half_cols(w_kr, dm.ret_dk)], axis=1)]
    assert all(c.shape[1] == blk for c in cols), [c.shape for c in cols]
    w_ext = jnp.concatenate(cols, axis=1).astype(BF16)

    n_lru = lru_w_a.shape[1]
    eye = jnp.eye(n_lru, dtype=F32)
    block_diag = lambda w: jnp.einsum("nde,nm->ndme", w, eye).reshape(dm.mix_a, dm.mix_a).astype(BF16)

    qk_dim = dm.qk_nope + dm.qk_rope
    wq3 = w_uq[l].reshape(q_lora, dm.mla_heads, qk_dim)
    pe = wq3[..., dm.qk_nope:]
    half = dm.qk_rope // 2
    pe_rot = jnp.concatenate([-pe[..., half:], pe[..., :half]], axis=-1)
    w_uq_ext = jnp.concatenate([wq3, pe_rot], axis=-1).transpose(1, 0, 2).astype(BF16)

    wo = w_out[l].astype(BF16)
    n_exp = dm.n_exp
    wr = jnp.concatenate([w_router[l], jnp.zeros((d, LANES - n_exp), F32)], axis=1)
    wr_hi = wr.astype(BF16)
    wr_lo = (wr - wr_hi.astype(F32)).astype(BF16)
    br = jnp.concatenate([b_router[l], jnp.full((LANES - n_exp,), NEG_BIG, F32)]).reshape(1, LANES)
    row = lambda v: v[l].reshape(1, -1)
    return dict(
        w_ext=w_ext, conv_w=conv_w[l], conv_b=row(conv_b),
        wa_bd=block_diag(lru_w_a[l]), wx_bd=block_diag(lru_w_x[l]),
        ba=row(lru_b_a), bx=row(lru_b_x), lam=row(lru_lam),
        g_q=row(g_q_norm), w_uq_ext=w_uq_ext, g_kv=row(g_kv_norm),
        w_uk=w_uk[l].transpose(1, 0, 2).astype(BF16), w_uv=w_uv[l].transpose(1, 0, 2).astype(BF16),
        wo_a=wo[:dm.mix_a], wo_b=wo[dm.mix_a:dm.mix_a + dm.mix_b], wo_c=wo[dm.mix_a + dm.mix_b:],
        wr_hi=wr_hi, wr_lo=wr_lo, br=br,
        g_mix_pre=row(g_mix_pre), g_mix_post=row(g_mix_post), g_ffn_pre=row(g_ffn_pre), g_ffn_post=row(g_ffn_post),
    )


def _mixer(x, mod, wl, dm, conv_buf, h0, s0, past_ckv, past_kpe):
    bsz, seq, d = x.shape
    past_len = 0 if past_ckv is None else past_ckv.shape[1]
    pos = past_len + jnp.arange(seq, dtype=jnp.int32)
    u = _in_proj(x, mod, wl["g_mix_pre"], wl["w_ext"])

    ctx_rows = 8
    n_ctx = dm.conv_w - 1
    conv_ctx = jnp.concatenate([jnp.zeros((bsz, ctx_rows - n_ctx, dm.mix_a), F32), conv_buf], axis=1)
    a_out, conv_new, h_new = _rg_lru(u, conv_ctx, h0.reshape(bsz, 1, dm.mix_a), wl["conv_w"], wl["conv_b"],
                                     wl["wa_bd"], wl["wx_bd"], wl["ba"], wl["bx"], wl["lam"])

    cos_k, sin_k = _rope_tables(pos, dm.ret_dk)
    k_scale = float(dm.ret_dk) ** -0.5
    scale_row = jnp.concatenate([jnp.ones((dm.ret_qk,), F32), jnp.full((dm.ret_qk,), k_scale, F32)])[None, :]
    cos_t = jnp.tile(cos_k, (1, 2 * dm.ret_heads)) * scale_row
    sin_t = jnp.tile(sin_k, (1, 2 * dm.ret_heads)) * scale_row
    b_out, s_new = _retention(u, s0, cos_t, sin_t, blocks=(2, 3, 4, 7))

    cos_r, sin_r = _rope_tables(pos, dm.qk_rope)
    lane_pad = lambda t, left, right, fill: jnp.concatenate(
        [jnp.full((seq, left), fill, F32), t, jnp.zeros((seq, right), F32)], axis=1)
    ckv_new, kpe_new = _latent(u, wl["g_kv"], lane_pad(cos_r, 0, LANES - dm.qk_rope, 0.0),
                               lane_pad(sin_r, 0, LANES - dm.qk_rope, 0.0), kv_blk=6, blk_w=dm.mix_b, rope_w=dm.qk_rope)
    q = _q_proj(u, wl["g_q"], wl["w_uq_ext"], lane_pad(cos_r, dm.qk_nope, dm.qk_rope, 1.0),
                lane_pad(sin_r, dm.qk_nope, dm.qk_rope, 0.0), ql_blk=5, rope_w=dm.qk_rope)
    if past_ckv is None:
        ckv_all, kpe_all = ckv_new, kpe_new
    else:
        past_kpe_pad = jnp.concatenate([past_kpe, jnp.zeros(past_kpe.shape[:2] + (LANES - dm.qk_rope,), F32)], axis=2)
        ckv_all = jnp.concatenate([past_ckv, ckv_new], axis=1)
        kpe_all = jnp.concatenate([past_kpe_pad, kpe_new], axis=1)
    k, v = _kv_proj(ckv_all, kpe_all, wl["w_uk"], wl["w_uv"])
    c_out = _attention(q, k, v, past_len, dm.qk_nope + dm.qk_rope)

    x1, h2, gates, idx = _out_proj(a_out, b_out, c_out, x, mod, wl["g_mix_post"], wl["g_ffn_pre"],
                                   wl["wo_a"], wl["wo_b"], wl["wo_c"], wl["wr_hi"], wl["wr_lo"], wl["br"])
    states = (ckv_new, kpe_new[:, :, :dm.qk_rope], conv_new[:, ctx_rows - n_ctx:], h_new.reshape(bsz, dm.mix_a), s_new)
    return x1, h2, gates, idx[:, :, :TOP_K], states


def kernel(x_prompt, x_sample, c_prompt, c_sample, cache_ckv, cache_kpe, state_conv, state_lru, state_ret, w_ada, b_ada, g_mix_pre, g_mix_post, g_ffn_pre, g_ffn_post, w_in, conv_w, conv_b, lru_w_a, lru_b_a, lru_w_x, lru_b_x, lru_lam, g_q_norm, w_uq, g_kv_norm, w_uk, w_uv, w_out, w_router, b_router, w_gate_up, b_gate_up, w_down, b_down):
    depth = w_in.shape[0]
    dm = _Dims(state_ret, cache_ckv, cache_kpe, lru_w_a, w_uk, w_uv, conv_w, w_router, w_down)
    bp, lp, d = x_prompt.shape
    bs, ls, _ = x_sample.shape
    n_tok = bp * lp + bs * ls
    moe_tm = min(1024, max(16, 1 << int(np.log2((n_tok * TOP_K) // dm.n_exp))))
    n_blocks = -(-(n_tok * TOP_K) // moe_tm) + dm.n_exp

    xp, xs = x_prompt, x_sample
    p_states, s_states = [], []
    for l in range(depth):
        wl = _layer_weights(l, dm, w_in, conv_w, conv_b, lru_w_a, lru_b_a, lru_w_x, lru_b_x, lru_lam, g_q_norm, w_uq,
                            g_kv_norm, w_uk, w_uv, w_out, w_router, b_router, g_mix_pre, g_mix_post, g_ffn_pre, g_ffn_post)
        mod_p = _ada_mod(c_prompt, w_ada, b_ada, l).reshape(bp, 6, d)
        mod_s = _ada_mod(c_sample, w_ada, b_ada, l).reshape(bs, 6, d)
        zeros = lambda *s: jnp.zeros(s, F32)
        x1p, h2p, gp, ip, st_p = _mixer(
            xp, mod_p, wl, dm, zeros(bp, dm.conv_w - 1, dm.mix_a), zeros(bp, dm.mix_a),
            zeros(bp, dm.ret_heads, dm.ret_dk, dm.ret_dv), None, None)
        x1s, h2s, gs, is_, st_s = _mixer(
            xs, mod_s, wl, dm, state_conv[l], state_lru[l], state_ret[l], cache_ckv[l], cache_kpe[l])
        p_states.append(st_p)
        s_states.append(st_s)

        h2 = jnp.concatenate([h2p.reshape(-1, d), h2s.reshape(-1, d)], axis=0)
        top_idx = jnp.concatenate([ip.reshape(-1, TOP_K), is_.reshape(-1, TOP_K)], axis=0)
        src_rows, dst_rows, block_e, n_used, n_out_rows = _route(top_idx, dm.n_exp, moe_tm, n_blocks)
        rows = _moe_ffn(h2, src_rows, dst_rows, block_e, n_used, w_gate_up, b_gate_up, w_down, b_down, l, n_out_rows)
        xp = _ffn_post(x1p, rows, gp, mod_p, wl["g_ffn_post"], 0, n_tok)
        xs = _ffn_post(x1s, rows, gs, mod_s, wl["g_ffn_post"], bp * lp, n_tok)

    stack = lambda sts: tuple(jnp.stack(t, axis=0) for t in zip(*sts))
    p_ckv, p_kpe, p_conv, p_lru, p_ret = stack(p_states)
    s_ckv, s_kpe, s_conv, s_lru, s_ret = stack(s_states)
    return (xp, xs, p_ckv, p_kpe, p_conv, p_lru, p_ret, s_ckv, s_kpe, s_conv, s_lru, s_ret)
```

```python
import functools

import numpy as np
import jax
import jax.numpy as jnp
from jax import lax
from jax.experimental import pallas as pl
from jax.experimental.pallas import tpu as pltpu

CHUNK = 64
EPS = 1e-6
ROPE_BASE = 10000.0
RG_C = 8.0
TOP_K = 4
SWIGLU_LIMIT = 7.0
SWIGLU_ALPHA = 1.702
NEG_BIG = -1e30
GELU_C = float(np.sqrt(2.0 / np.pi))

LANES = 128
VMEM_LIMIT_BYTES = 56 * 1024 * 1024

F32 = jnp.float32
BF16 = jnp.bfloat16


def _params(semantics):
    return pltpu.CompilerParams(dimension_semantics=semantics, vmem_limit_bytes=VMEM_LIMIT_BYTES)


def _tile(n, pref):
    if n <= pref:
        return n
    t = pref
    while n % t:
        t //= 2
    return t


def _rms(x, g):
    return x * lax.rsqrt(jnp.mean(x * x, axis=-1, keepdims=True) + EPS) * g


def _dot(a, b):
    return jnp.dot(a, b, preferred_element_type=F32)


def _dot_nt(a, b):
    return lax.dot_general(a, b, (((1,), (1,)), ((), ())), preferred_element_type=F32)


def _dot_tn(a, b):
    return lax.dot_general(a, b, (((0,), (0,)), ((), ())), preferred_element_type=F32)


def _sigmoid(x):
    return 1.0 / (1.0 + jnp.exp(-x))


def _pack_bf16_halves(x):
    n = x.shape[1] // 2
    bits = lax.bitcast_convert_type(x.astype(BF16).astype(F32), jnp.uint32)
    return (bits[:, :n] >> 16) | (bits[:, n:] & jnp.uint32(0xFFFF0000))


def _unpack_bf16_halves(w):
    lo = lax.bitcast_convert_type(w << 16, F32)
    hi = lax.bitcast_convert_type(w & jnp.uint32(0xFFFF0000), F32)
    return lo, hi


def _ada_kernel(c_ref, w_ref, b_ref, o_ref):
    c = c_ref[...]
    s = (c * _sigmoid(c)).astype(BF16)
    o_ref[...] = _dot(s, w_ref[...].astype(BF16)) + b_ref[...]


def _ada_mod(c, w_ada, b_ada, layer):
    bsz, d = c.shape
    n = w_ada.shape[-1]
    tn = _tile(n, 1024)
    return pl.pallas_call(
        _ada_kernel,
        grid=(n // tn,),
        in_specs=[
            pl.BlockSpec((bsz, d), lambda j: (0, 0)),
            pl.BlockSpec((None, d, tn), lambda j: (layer, 0, j)),
            pl.BlockSpec((None, 1, tn), lambda j: (layer, 0, j)),
        ],
        out_specs=pl.BlockSpec((bsz, tn), lambda j: (0, j)),
        out_shape=jax.ShapeDtypeStruct((bsz, n), F32),
        compiler_params=_params(("arbitrary",)),
        name="ada_mod",
    )(c, w_ada, b_ada.reshape(b_ada.shape[0], 1, n))


def _in_proj_kernel(x_ref, mod_ref, g_ref, w_ref, u_ref):
    mod = mod_ref[0]
    h = _rms(x_ref[0], g_ref[...]) * (1.0 + mod[1:2]) + mod[0:1]
    u_ref[0] = _dot(h.astype(BF16), w_ref[...])


def _in_proj(x, mod, g, w_ext):
    bsz, seq, d = x.shape
    n = w_ext.shape[1]
    tm = _tile(seq, 256)
    return pl.pallas_call(
        _in_proj_kernel,
        grid=(bsz, seq // tm),
        in_specs=[
            pl.BlockSpec((1, tm, d), lambda b, i: (b, i, 0)),
            pl.BlockSpec((1, 6, d), lambda b, i: (b, 0, 0)),
            pl.BlockSpec((1, d), lambda b, i: (0, 0)),
            pl.BlockSpec((d, n), lambda b, i: (0, 0)),
        ],
        out_specs=pl.BlockSpec((1, tm, n), lambda b, i: (b, i, 0)),
        out_shape=jax.ShapeDtypeStruct((bsz, seq, n), F32),
        compiler_params=_params(("parallel", "parallel")),
        name="in_proj",
    )(x, mod, g, w_ext)


def _lru_kernel(xa_ref, ya_ref, cbuf_ref, h0_ref, cw_ref, cb_ref, wa_ref, wx_ref, ba_ref, bx_ref,
                lam_ref, out_ref, cnew_ref, hlast_ref, xbuf, hcar):
    tl = xa_ref.shape[1]
    width = xa_ref.shape[2]
    ctx = xbuf.shape[0] - tl

    @pl.when(pl.program_id(1) == 0)
    def _():
        xbuf[0:ctx, :] = cbuf_ref[0]
        hcar[...] = h0_ref[0]

    xa = xa_ref[0]
    xbuf[ctx:ctx + tl, :] = xa
    cw = cw_ref[...]
    n_tap = cw.shape[0]
    xc = cb_ref[...] + cw[n_tap - 1:n_tap] * xa
    for k in range(n_tap - 1):
        off = ctx - (n_tap - 1) + k
        xc = xc + cw[k:k + 1] * xbuf[off:off + tl, :]
    new_ctx = xbuf[tl:tl + ctx, :]
    xbuf[0:ctx, :] = new_ctx
    cnew_ref[0] = new_ctx

    xcb = xc.astype(BF16)
    r = _sigmoid(_dot(xcb, wa_ref[...]) + ba_ref[...])
    gi = _sigmoid(_dot(xcb, wx_ref[...]) + bx_ref[...])
    z = -lam_ref[...]
    softplus = jnp.maximum(z, 0.0) + jnp.log1p(jnp.exp(-jnp.abs(z)))
    log_a = (-RG_C) * r * softplus
    a = jnp.exp(log_a)
    b = jnp.sqrt(1.0 - jnp.exp(2.0 * log_a)) * gi * xc

    row = lax.broadcasted_iota(jnp.int32, (tl, width), 0)
    s = 1
    while s < tl:
        a_sh = pltpu.roll(a, s, 0)
        b_sh = pltpu.roll(b, s, 0)
        valid = row >= s
        b = jnp.where(valid, a * b_sh + b, b)
        a = jnp.where(valid, a * a_sh, a)
        s *= 2
    h = a * hcar[...] + b
    h_last = h[tl - 1:tl, :]
    hcar[...] = h_last
    hlast_ref[0] = h_last

    ya = ya_ref[0]
    gelu = 0.5 * ya * (1.0 + jnp.tanh(GELU_C * (ya + 0.044715 * (ya * ya * ya))))
    out_ref[0] = (h * gelu).astype(out_ref.dtype)


def _rg_lru(u, conv_ctx, h0, cw, cb, wa_bd, wx_bd, ba, bx, lam):
    bsz, seq, _ = u.shape
    width = cw.shape[1]
    tl = _tile(seq, 512)
    ctx = conv_ctx.shape[1]
    xa_blk = 0
    ya_blk = 1
    vec = lambda: pl.BlockSpec((1, width), lambda b, t: (0, 0))
    return pl.pallas_call(
        _lru_kernel,
        grid=(bsz, seq // tl),
        in_specs=[
            pl.BlockSpec((1, tl, width), lambda b, t: (b, t, xa_blk)),
            pl.BlockSpec((1, tl, width), lambda b, t: (b, t, ya_blk)),
            pl.BlockSpec((1, ctx, width), lambda b, t: (b, 0, 0)),
            pl.BlockSpec((1, 1, width), lambda b, t: (b, 0, 0)),
            pl.BlockSpec(cw.shape, lambda b, t: (0, 0)),
            vec(),
            pl.BlockSpec((width, width), lambda b, t: (0, 0)),
            pl.BlockSpec((width, width), lambda b, t: (0, 0)),
            vec(), vec(), vec(),
        ],
        out_specs=[
            pl.BlockSpec((1, tl, width), lambda b, t: (b, t, 0)),
            pl.BlockSpec((1, ctx, width), lambda b, t: (b, 0, 0)),
            pl.BlockSpec((1, 1, width), lambda b, t: (b, 0, 0)),
        ],
        out_shape=[
            jax.ShapeDtypeStruct((bsz, seq, width), BF16),
            jax.ShapeDtypeStruct((bsz, ctx, width), F32),
            jax.ShapeDtypeStruct((bsz, 1, width), F32),
        ],
        scratch_shapes=[pltpu.VMEM((tl + ctx, width), F32), pltpu.VMEM((1, width), F32)],
        compiler_params=_params(("parallel", "arbitrary")),
        name="rg_lru",
    )(u, u, conv_ctx, h0, cw, cb, wa_bd, wx_bd, ba, bx, lam)


def _ret_kernel(qk_ref, rot_ref, v_ref, g_ref, cos_ref, sin_ref, dmask_ref, qdec_ref, kdec_ref,
                s0_ref, out_ref, slast_ref, s_sc, *, gchunk):
    n_heads, dk, dv = s_sc.shape
    hdk = n_heads * dk

    @pl.when(pl.program_id(1) == 0)
    def _():
        s_sc[...] = s0_ref[0]

    qkr = qk_ref[0] * cos_ref[...] + rot_ref[0] * sin_ref[...]
    v = v_ref[0]
    vdec = v * kdec_ref[...]
    gate = g_ref[0]
    qdec = qdec_ref[...]
    for h in range(n_heads):
        q = qkr[:, h * dk:(h + 1) * dk].astype(BF16)
        k = qkr[:, hdk + h * dk:hdk + (h + 1) * dk].astype(BF16)
        vh = v[:, h * dv:(h + 1) * dv].astype(BF16)
        vd = vdec[:, h * dv:(h + 1) * dv].astype(BF16)
        s_prev = s_sc[h]
        scores = _dot_nt(q, k) * dmask_ref[h]
        o = _dot(scores.astype(BF16), vh) + qdec[:, h * dv:(h + 1) * dv] * _dot(q, s_prev.astype(BF16))
        s_sc[h] = s_prev * gchunk[h] + _dot_tn(k, vd)
        o = o * lax.rsqrt(jnp.mean(o * o, axis=-1, keepdims=True) + EPS)
        gh = gate[:, h * dv:(h + 1) * dv]
        out_ref[0, :, h * dv:(h + 1) * dv] = (o * (gh * _sigmoid(gh))).astype(out_ref.dtype)
    slast_ref[0] = s_sc[...]


def _retention(u, s0, cos_t, sin_t, blocks):
    bsz, seq, _ = u.shape
    _, n_heads, dk, dv = s0.shape
    width = n_heads * dv
    tc = _tile(seq, 256)
    log_g = np.log1p(-np.exp2(-5.0 - np.arange(n_heads, dtype=np.float64)))
    idx = np.arange(tc, dtype=np.float64)
    diff = idx[:, None] - idx[None, :]
    dmask = np.where(diff[None] >= 0, np.exp(np.maximum(diff, 0.0)[None] * log_g[:, None, None]), 0.0)
    qdec = np.repeat(np.exp((idx + 1.0)[:, None] * log_g[None, :]), dv, axis=1)
    kdec = np.repeat(np.exp((tc - 1.0 - idx)[:, None] * log_g[None, :]), dv, axis=1)
    gchunk = tuple(float(g) for g in np.exp(tc * log_g))
    qk_blk, v_blk, g_blk, rot_blk = blocks
    u_spec = lambda blk: pl.BlockSpec((1, tc, width), lambda b, n: (b, n, blk))
    full2 = lambda a: pl.BlockSpec(a.shape, lambda b, n: (0, 0))
    return pl.pallas_call(
        functools.partial(_ret_kernel, gchunk=gchunk),
        grid=(bsz, seq // tc),
        in_specs=[
            u_spec(qk_blk), u_spec(rot_blk), u_spec(v_blk), u_spec(g_blk),
            pl.BlockSpec((tc, width), lambda b, n: (n, 0)),
            pl.BlockSpec((tc, width), lambda b, n: (n, 0)),
            pl.BlockSpec(dmask.shape, lambda b, n: (0, 0, 0)),
            full2(qdec), full2(kdec),
            pl.BlockSpec((1, n_heads, dk, dv), lambda b, n: (b, 0, 0, 0)),
        ],
        out_specs=[
            pl.BlockSpec((1, tc, width), lambda b, n: (b, n, 0)),
            pl.BlockSpec((1, n_heads, dk, dv), lambda b, n: (b, 0, 0, 0)),
        ],
        out_shape=[
            jax.ShapeDtypeStruct((bsz, seq, width), BF16),
            jax.ShapeDtypeStruct((bsz, n_heads, dk, dv), F32),
        ],
        scratch_shapes=[pltpu.VMEM((n_heads, dk, dv), F32)],
        compiler_params=_params(("parallel", "arbitrary")),
        name="retention",
    )(u, u, u, u, cos_t, sin_t, jnp.asarray(dmask, F32), jnp.asarray(qdec, F32),
      jnp.asarray(kdec, F32), s0)


def _latent_kernel(kv_ref, g_ref, cos_ref, sin_ref, ckv_ref, kpe_ref, *, rope_w):
    lora = ckv_ref.shape[2]
    blk = kv_ref[0]
    ckv_ref[0] = _rms(blk[:, :lora], g_ref[...])
    kp = blk[:, lora:lora + LANES]
    rolled = pltpu.roll(kp, LANES - rope_w, 1)
    kpe_ref[0] = kp * cos_ref[...] + rolled * sin_ref[...]


def _latent(u, g_kv, cos_p, sin_p, kv_blk, blk_w, rope_w):
    bsz, seq, _ = u.shape
    lora = g_kv.shape[1]
    tm = _tile(seq, 512)
    return pl.pallas_call(
        functools.partial(_latent_kernel, rope_w=rope_w),
        grid=(bsz, seq // tm),
        in_specs=[
            pl.BlockSpec((1, tm, blk_w), lambda b, i: (b, i, kv_blk)),
            pl.BlockSpec((1, lora), lambda b, i: (0, 0)),
            pl.BlockSpec((tm, LANES), lambda b, i: (i, 0)),
            pl.BlockSpec((tm, LANES), lambda b, i: (i, 0)),
        ],
        out_specs=[
            pl.BlockSpec((1, tm, lora), lambda b, i: (b, i, 0)),
            pl.BlockSpec((1, tm, LANES), lambda b, i: (b, i, 0)),
        ],
        out_shape=[
            jax.ShapeDtypeStruct((bsz, seq, lora), F32),
            jax.ShapeDtypeStruct((bsz, seq, LANES), F32),
        ],
        compiler_params=_params(("parallel", "parallel")),
        name="latent_norm_rope",
    )(u, g_kv, cos_p, sin_p)


def _q_proj_kernel(ql_ref, g_ref, w_ref, cos_ref, sin_ref, q_ref, *, rope_w):
    cq = _rms(ql_ref[0], g_ref[...]).astype(BF16)
    cos = cos_ref[...]
    sin = sin_ref[...]
    for h in range(w_ref.shape[0]):
        r = _dot(cq, w_ref[h])
        rolled = pltpu.roll(r, r.shape[1] - rope_w, 1)
        q_ref[0, h] = (r * cos + rolled * sin).astype(q_ref.dtype)


def _q_proj(u, g_q, w_uq_ext, cos_q, sin_q, ql_blk, rope_w):
    bsz, seq, _ = u.shape
    n_heads, q_lora, width = w_uq_ext.shape
    tm = _tile(seq, 512)
    return pl.pallas_call(
        functools.partial(_q_proj_kernel, rope_w=rope_w),
        grid=(bsz, seq // tm),
        in_specs=[
            pl.BlockSpec((1, tm, q_lora), lambda b, i: (b, i, ql_blk)),
            pl.BlockSpec((1, q_lora), lambda b, i: (0, 0)),
            pl.BlockSpec((n_heads, q_lora, width), lambda b, i: (0, 0, 0)),
            pl.BlockSpec((tm, width), lambda b, i: (i, 0)),
            pl.BlockSpec((tm, width), lambda b, i: (i, 0)),
        ],
        out_specs=pl.BlockSpec((1, n_heads, tm, width), lambda b, i: (b, 0, i, 0)),
        out_shape=jax.ShapeDtypeStruct((bsz, n_heads, seq, width), BF16),
        compiler_params=_params(("parallel", "parallel")),
        name="mla_q_proj",
    )(u, g_q, w_uq_ext, cos_q, sin_q)


def _kv_proj_kernel(ckv_ref, kpe_ref, wk_ref, wv_ref, k_ref, v_ref):
    ckv = ckv_ref[0].astype(BF16)
    kpe = kpe_ref[0].astype(k_ref.dtype)
    nope = wk_ref.shape[2]
    for h in range(wk_ref.shape[0]):
        k_ref[0, h, :, :nope] = _dot(ckv, wk_ref[h]).astype(k_ref.dtype)
        k_ref[0, h, :, nope:] = kpe
        v_ref[0, h] = _dot(ckv, wv_ref[h]).astype(v_ref.dtype)


def _kv_proj(ckv, kpe, w_uk, w_uv):
    bsz, klen, lora = ckv.shape
    n_heads, _, nope = w_uk.shape
    vdim = w_uv.shape[2]
    tm = _tile(klen, 512) if klen % 512 == 0 else klen
    return pl.pallas_call(
        _kv_proj_kernel,
        grid=(bsz, klen // tm),
        in_specs=[
            pl.BlockSpec((1, tm, lora), lambda b, i: (b, i, 0)),
            pl.BlockSpec((1, tm, LANES), lambda b, i: (b, i, 0)),
            pl.BlockSpec((n_heads, lora, nope), lambda b, i: (0, 0, 0)),
            pl.BlockSpec((n_heads, lora, vdim), lambda b, i: (0, 0, 0)),
        ],
        out_specs=[
            pl.BlockSpec((1, n_heads, tm, nope + LANES), lambda b, i: (b, 0, i, 0)),
            pl.BlockSpec((1, n_heads, tm, vdim), lambda b, i: (b, 0, i, 0)),
        ],
        out_shape=[
            jax.ShapeDtypeStruct((bsz, n_heads, klen, nope + LANES), BF16),
            jax.ShapeDtypeStruct((bsz, n_heads, klen, vdim), BF16),
        ],
        compiler_params=_params(("parallel", "parallel")),
        name="mla_kv_proj",
    )(ckv, kpe, w_uk, w_uv)


def _attn_kernel(q_ref, k_ref, v_ref, o_ref, m_sc, l_sc, acc_sc, *, q_pos0, scale, tk):
    qi = pl.program_id(2)
    tq = q_ref.shape[2]
    nk = k_ref.shape[2] // tk
    q = q_ref[0, 0]
    m_sc[...] = jnp.full(m_sc.shape, NEG_BIG, F32)
    l_sc[...] = jnp.zeros(l_sc.shape, F32)
    acc_sc[...] = jnp.zeros(acc_sc.shape, F32)

    def step(j, masked):
        start = pl.multiple_of(j * tk, tk)
        s = _dot_nt(q, k_ref[0, 0, pl.ds(start, tk), :]) * scale
        if masked:
            q_chunk = (q_pos0 + qi * tq + lax.broadcasted_iota(jnp.int32, (tq, tk), 0)) // CHUNK
            k_chunk = (start + lax.broadcasted_iota(jnp.int32, (tq, tk), 1)) // CHUNK
            s = jnp.where(k_chunk <= q_chunk, s, NEG_BIG)
        m_prev = m_sc[...]
        m_new = jnp.maximum(m_prev, jnp.max(s, axis=-1, keepdims=True))
        alpha = jnp.exp(m_prev - m_new)
        p = jnp.exp(s - m_new)
        l_sc[...] = alpha * l_sc[...] + jnp.sum(p, axis=-1, keepdims=True)
        acc_sc[...] = alpha * acc_sc[...] + _dot(p.astype(BF16), v_ref[0, 0, pl.ds(start, tk), :])
        m_sc[...] = m_new

    first_q_chunk = (q_pos0 + qi * tq) // CHUNK
    last_q_chunk = (q_pos0 + qi * tq + tq - 1) // CHUNK
    n_full = jnp.minimum(((first_q_chunk + 1) * CHUNK) // tk, nk)
    n_seen = jnp.minimum((last_q_chunk * CHUNK + CHUNK - 1) // tk + 1, nk)
    lax.fori_loop(0, n_full, lambda j, c: (step(j, False), c)[1], 0)
    lax.fori_loop(n_full, n_seen, lambda j, c: (step(j, True), c)[1], 0)
    o_ref[0] = (acc_sc[...] / l_sc[...]).astype(o_ref.dtype)


def _attention(q, k, v, q_pos0, qk_dim):
    bsz, n_heads, seq, width = q.shape
    klen = k.shape[2]
    vdim = v.shape[3]
    tq = _tile(seq, 512)
    tk = _tile(klen, 512) if klen % 512 == 0 else klen
    return pl.pallas_call(
        functools.partial(_attn_kernel, q_pos0=q_pos0, scale=float(qk_dim) ** -0.5, tk=tk),
        grid=(bsz, n_heads, seq // tq),
        in_specs=[
            pl.BlockSpec((1, 1, tq, width), lambda b, h, qi: (b, h, qi, 0)),
            pl.BlockSpec((1, 1, klen, width), lambda b, h, qi: (b, h, 0, 0)),
            pl.BlockSpec((1, 1, klen, vdim), lambda b, h, qi: (b, h, 0, 0)),
        ],
        out_specs=pl.BlockSpec((1, tq, vdim), lambda b, h, qi: (b, qi, h)),
        out_shape=jax.ShapeDtypeStruct((bsz, seq, n_heads * vdim), BF16),
        scratch_shapes=[pltpu.VMEM((tq, 1), F32), pltpu.VMEM((tq, 1), F32), pltpu.VMEM((tq, vdim), F32)],
        compiler_params=_params(("parallel", "parallel", "arbitrary")),
        name="mla_attention",
    )(q, k, v)


def _out_proj_kernel(a_ref, b_ref, c_ref, x_ref, mod_ref, gpost_ref, gpre_ref, wa_ref, wb_ref, wc_ref,
                     wrh_ref, wrl_ref, br_ref, x1_ref, h2_ref, gate_ref, idx_ref):
    mix = _dot(a_ref[0], wa_ref[...]) + _dot(b_ref[0], wb_ref[...]) + _dot(c_ref[0], wc_ref[...])
    mod = mod_ref[0]
    x1 = x_ref[0] + mod[2:3] * _rms(mix, gpost_ref[...])
    x1_ref[0] = x1
    h2 = _rms(x1, gpre_ref[...]) * (1.0 + mod[4:5]) + mod[3:4]
    h2_ref[0] = _pack_bf16_halves(h2)
    hi = h2.astype(BF16)
    lo = (h2 - hi.astype(F32)).astype(BF16)
    logits = _dot(hi, wrh_ref[...]) + (_dot(hi, wrl_ref[...]) + _dot(lo, wrh_ref[...])) + br_ref[...]
    lane = lax.broadcasted_iota(jnp.int32, logits.shape, 1)
    lane_f = lane.astype(F32)
    gates = jnp.zeros(logits.shape, F32)
    idxs = jnp.zeros(logits.shape, F32)
    top0 = None
    denom = None
    for k in range(TOP_K):
        m = jnp.max(logits, axis=-1, keepdims=True)
        sel = jnp.min(jnp.where(logits == m, lane_f, float(LANES)), axis=-1, keepdims=True)
        if k == 0:
            top0 = m
            e = jnp.ones_like(m)
            denom = e
        else:
            e = jnp.exp(m - top0)
            denom = denom + e
        gates = jnp.where(lane == k, e, gates)
        idxs = jnp.where(lane == k, sel, idxs)
        logits = jnp.where(lane_f == sel, NEG_BIG * 2.0, logits)
    gate_ref[0] = gates / denom
    idx_ref[0] = idxs.astype(jnp.int32)


def _out_proj(a_out, b_out, c_out, x, mod, g_post, g_pre, wa, wb, wc, wr_hi, wr_lo, br):
    bsz, seq, d = x.shape
    tm = _tile(seq, 256)
    row = lambda w: pl.BlockSpec((1, tm, w), lambda b, i: (b, i, 0))
    const = lambda a: pl.BlockSpec(a.shape, lambda b, i: (0, 0))
    return pl.pallas_call(
        _out_proj_kernel,
        grid=(bsz, seq // tm),
        in_specs=[
            row(a_out.shape[2]), row(b_out.shape[2]), row(c_out.shape[2]), row(d),
            pl.BlockSpec((1, 6, d), lambda b, i: (b, 0, 0)),
            const(g_post), const(g_pre), const(wa), const(wb), const(wc),
            const(wr_hi), const(wr_lo), const(br),
        ],
        out_specs=[row(d), row(d // 2), row(LANES), row(LANES)],
        out_shape=[
            jax.ShapeDtypeStruct((bsz, seq, d), F32),
            jax.ShapeDtypeStruct((bsz, seq, d // 2), jnp.uint32),
            jax.ShapeDtypeStruct((bsz, seq, LANES), F32),
            jax.ShapeDtypeStruct((bsz, seq, LANES), jnp.int32),
        ],
        compiler_params=_params(("parallel", "parallel")),
        name="out_proj_router",
    )(a_out, b_out, c_out, x, mod, g_post, g_pre, wa, wb, wc, wr_hi, wr_lo, br)


def _moe_kernel(be_ref, nu_ref, src_ref, dst_ref, h_hbm, wg_ref, wu_ref, bg_ref, bu_ref, wd_ref, bd_ref,
                rows_hbm, xbuf, xb, acc, pbuf, gsem, ssem):
    g = pl.program_id(0)
    f = pl.program_id(1)
    nf = pl.num_programs(1)
    n_used = nu_ref[0]
    tm, half = xbuf.shape
    chunk = tm // nf
    live = jnp.logical_and(g >= 1, g <= n_used)

    def gather_copy(r):
        return pltpu.make_async_copy(h_hbm.at[pl.ds(src_ref[0, r], 1)], xbuf.at[pl.ds(r, 1)], gsem)

    def scatter_copy(r):
        return pltpu.make_async_copy(pbuf.at[pl.ds(r, 1)], rows_hbm.at[pl.ds(dst_ref[0, r], 1)], ssem)

    def wait_gather():
        pltpu.make_async_copy(h_hbm.at[pl.ds(0, tm)], xbuf, gsem).wait()

    def wait_scatter():
        pltpu.make_async_copy(pbuf, rows_hbm.at[pl.ds(0, tm)], ssem).wait()

    @pl.when(jnp.logical_and(g == 0, f == 0))
    def _():
        pbuf[...] = jnp.zeros(pbuf.shape, pbuf.dtype)
        lax.fori_loop(0, tm, lambda r, c: (gather_copy(r).start(), c)[1], 0)

    @pl.when(jnp.logical_and(live, f == 0))
    def _():
        wait_gather()
        lo, hi = _unpack_bf16_halves(xbuf[...])
        xb[:, :half] = lo.astype(BF16)
        xb[:, half:] = hi.astype(BF16)
        acc[...] = jnp.broadcast_to(bd_ref[...], acc.shape)

    @pl.when(live)
    def _():
        for i in range(chunk):
            r = f * chunk + i
            gather_copy(r).start()
            scatter_copy(r).start()
        x = xb[...]
        gate = _dot(x, wg_ref[...].astype(BF16)) + bg_ref[...]
        up = _dot(x, wu_ref[...].astype(BF16)) + bu_ref[...]
        x_glu = jnp.minimum(gate, SWIGLU_LIMIT)
        x_lin = jnp.clip(up, -SWIGLU_LIMIT, SWIGLU_LIMIT)
        act = x_glu * _sigmoid(SWIGLU_ALPHA * x_glu) * (x_lin + 1.0)
        acc[...] += _dot(act.astype(BF16), wd_ref[...].astype(BF16))

    @pl.when(jnp.logical_and(live, f == nf - 1))
    def _():
        wait_scatter()
        pbuf[...] = _pack_bf16_halves(acc[...])

    @pl.when(jnp.logical_and(g == n_used + 1, f == 0))
    def _():
        wait_gather()
        lax.fori_loop(0, tm, lambda r, c: (scatter_copy(r).start(), c)[1], 0)
        wait_scatter()


def _moe_ffn(h_all, src_rows, dst_rows, block_e, n_used, w_gu, b_gu, w_dn, b_dn, layer, n_out_rows):
    n_grid, tm = src_rows.shape
    half = h_all.shape[1]
    d = 2 * half
    d_exp = w_dn.shape[2]
    tf = _tile(d_exp, 256)
    nf = d_exp // tf
    assert tm % nf == 0 and h_all.shape[0] >= tm and n_out_rows >= tm
    n_exp = w_gu.shape[1]
    b_gu4 = b_gu.reshape(b_gu.shape[0], n_exp, 1, 2 * d_exp)
    b_dn4 = b_dn.reshape(b_dn.shape[0], n_exp, 1, d)

    def f_idx(g, f, nu):
        return jnp.where(jnp.logical_and(g >= 1, g <= nu[0]), f, nf - 1)

    def e_idx(g, be, nu):
        return be[jnp.clip(g - 1, 0, nu[0] - 1)]

    grid_spec = pltpu.PrefetchScalarGridSpec(
        num_scalar_prefetch=2,
        grid=(n_grid, nf),
        in_specs=[
            pl.BlockSpec((None, 1, tm), lambda g, f, be, nu: (g, 0, 0), memory_space=pltpu.SMEM),
            pl.BlockSpec((None, 1, tm), lambda g, f, be, nu: (jnp.maximum(g - 2, 0), 0, 0), memory_space=pltpu.SMEM),
            pl.BlockSpec(memory_space=pl.ANY),
            pl.BlockSpec((None, None, d, tf), lambda g, f, be, nu: (layer, e_idx(g, be, nu), 0, f_idx(g, f, nu))),
            pl.BlockSpec((None, None, d, tf), lambda g, f, be, nu: (layer, e_idx(g, be, nu), 0, nf + f_idx(g, f, nu))),
            pl.BlockSpec((None, None, 1, tf), lambda g, f, be, nu: (layer, e_idx(g, be, nu), 0, f_idx(g, f, nu))),
            pl.BlockSpec((None, None, 1, tf), lambda g, f, be, nu: (layer, e_idx(g, be, nu), 0, nf + f_idx(g, f, nu))),
            pl.BlockSpec((None, None, tf, d), lambda g, f, be, nu: (layer, e_idx(g, be, nu), f_idx(g, f, nu), 0)),
            pl.BlockSpec((None, None, 1, d), lambda g, f, be, nu: (layer, e_idx(g, be, nu), 0, 0)),
        ],
        out_specs=pl.BlockSpec(memory_space=pl.ANY),
        scratch_shapes=[
            pltpu.VMEM((tm, half), jnp.uint32),
            pltpu.VMEM((tm, d), BF16),
            pltpu.VMEM((tm, d), F32),
            pltpu.VMEM((tm, half), jnp.uint32),
            pltpu.SemaphoreType.DMA(()),
            pltpu.SemaphoreType.DMA(()),
        ],
    )
    return pl.pallas_call(
        _moe_kernel,
        grid_spec=grid_spec,
        out_shape=jax.ShapeDtypeStruct((n_out_rows, half), jnp.uint32),
        compiler_params=_params(("arbitrary", "arbitrary")),
        name="moe_expert_ffn",
    )(block_e, n_used, src_rows.reshape(n_grid, 1, tm), dst_rows.reshape(n_grid, 1, tm), h_all,
      w_gu, w_gu, b_gu4, b_gu4, w_dn, b_dn4)


def _route(top_idx, n_exp, tm, n_blocks):
    n_tok, top_k = top_idx.shape
    n_rows = n_tok * top_k
    flat_e = top_idx.reshape(-1)
    onehot = (flat_e[:, None] == jnp.arange(n_exp, dtype=jnp.int32)[None, :]).astype(jnp.int32)
    csum = jnp.cumsum(onehot, axis=0)
    rank = jnp.sum(csum * onehot, axis=1) - 1
    counts = csum[-1]
    padded = (counts + tm - 1) // tm * tm
    pad_end = jnp.cumsum(padded)
    pad_start = pad_end - padded
    dest = pad_start[flat_e] + rank
    n_used = (pad_end[-1] // tm).astype(jnp.int32).reshape(1)
    block_start = jnp.arange(n_blocks, dtype=jnp.int32) * tm
    block_e = jnp.minimum(jnp.sum((pad_end[None, :] <= block_start[:, None]).astype(jnp.int32), axis=1), n_exp - 1)
    n_pos = (n_blocks + 2) * tm
    row_of_pos = jnp.full((n_pos,), -1, jnp.int32).at[dest].set(jnp.arange(n_rows, dtype=jnp.int32))
    pos = jnp.arange(n_pos, dtype=jnp.int32)
    src_rows = jnp.where(row_of_pos >= 0, row_of_pos // top_k, 0).reshape(n_blocks + 2, tm)
    dst_rows = jnp.where(row_of_pos >= 0, (row_of_pos % top_k) * n_tok + row_of_pos // top_k,
                         n_rows + pos).reshape(n_blocks + 2, tm)
    return src_rows, dst_rows, block_e, n_used, n_rows + n_pos


def _ffn_post_kernel(x1_ref, *refs):
    row_refs, (gate_ref, mod_ref, g_ref, o_ref) = refs[:TOP_K], refs[TOP_K:]
    gates = gate_ref[0]
    half = row_refs[0].shape[1]
    ffn_lo = ffn_hi = None
    for k in range(TOP_K):
        lo, hi = _unpack_bf16_halves(row_refs[k][...])
        gk = gates[:, k:k + 1]
        ffn_lo = gk * lo if k == 0 else ffn_lo + gk * lo
        ffn_hi = gk * hi if k == 0 else ffn_hi + gk * hi
    ssq = jnp.sum(ffn_lo * ffn_lo, axis=-1, keepdims=True) + jnp.sum(ffn_hi * ffn_hi, axis=-1, keepdims=True)
    inv = lax.rsqrt(ssq / (2 * half) + EPS)
    gt = mod_ref[0][5:6]
    g = g_ref[...]
    o_ref[0, :, :half] = x1_ref[0, :, :half] + gt[:, :half] * (ffn_lo * inv * g[:, :half])
    o_ref[0, :, half:] = x1_ref[0, :, half:] + gt[:, half:] * (ffn_hi * inv * g[:, half:])


def _ffn_post(x1, rows, gates, mod, g, tok_offset, n_tok):
    bsz, seq, d = x1.shape
    tm = _tile(seq, 256)
    assert tok_offset % tm == 0 and n_tok % tm == 0
    per_b = seq // tm
    row = pl.BlockSpec((1, tm, d), lambda b, i: (b, i, 0))

    def choice_spec(k):
        blk0 = (k * n_tok + tok_offset) // tm
        return pl.BlockSpec((tm, d // 2), lambda b, i: (blk0 + b * per_b + i, 0))

    return pl.pallas_call(
        _ffn_post_kernel,
        grid=(bsz, per_b),
        in_specs=[row] + [choice_spec(k) for k in range(TOP_K)] + [
            pl.BlockSpec((1, tm, LANES), lambda b, i: (b, i, 0)),
            pl.BlockSpec((1, 6, d), lambda b, i: (b, 0, 0)),
            pl.BlockSpec((1, d), lambda b, i: (0, 0)),
        ],
        out_specs=row,
        out_shape=jax.ShapeDtypeStruct((bsz, seq, d), F32),
        compiler_params=_params(("parallel", "parallel")),
        name="ffn_post",
    )(x1, *([rows] * TOP_K), gates, mod, g)


def _rot_half_cols(w, head_dim):
    d = w.shape[0]
    w3 = w.reshape(d, -1, head_dim)
    half = head_dim // 2
    return jnp.concatenate([-w3[..., half:], w3[..., :half]], axis=-1).reshape(d, -1)


def _rope_tables(pos, head_dim):
    half = head_dim // 2
    inv = ROPE_BASE ** (-jnp.arange(half, dtype=F32) / half)
    ang = pos.astype(F32)[:, None] * inv[None, :]
    cos = jnp.cos(ang)
    sin = jnp.sin(ang)
    return jnp.concatenate([cos, cos], axis=1), jnp.concatenate([sin, sin], axis=1)


class _Dims:
    def __init__(self, state_ret, cache_ckv, cache_kpe, lru_w_a, w_uk, w_uv, conv_w, w_router, w_down):
        _, _, self.ret_heads, self.ret_dk, self.ret_dv = state_ret.shape
        self.kv_lora = cache_ckv.shape[-1]
        self.qk_rope = cache_kpe.shape[-1]
        self.mix_a = conv_w.shape[-1]
        self.conv_w = conv_w.shape[1]
        self.mla_heads, self.qk_nope = w_uk.shape[2], w_uk.shape[3]
        self.v_head = w_uv.shape[3]
        self.n_exp = w_router.shape[-1]
        self.mix_b = self.ret_heads * self.ret_dv
        self.ret_qk = self.ret_heads * self.ret_dk


def _layer_weights(l, dm, w_in, conv_w, conv_b, lru_w_a, lru_b_a, lru_w_x, lru_b_x, lru_lam, g_q_norm, w_uq,
                   g_kv_norm, w_uk, w_uv, w_out, w_router, b_router, g_mix_pre, g_mix_post, g_ffn_pre, g_ffn_post):
    d = w_in.shape[1]
    q_lora = g_q_norm.shape[1]
    sizes = (dm.mix_a, dm.mix_a, dm.ret_qk, dm.ret_qk, dm.mix_b, dm.mix_b, q_lora, dm.kv_lora, dm.qk_rope)
    offs = np.concatenate([[0], np.cumsum(sizes)])
    wl = w_in[l]
    seg = [wl[:, offs[i]:offs[i + 1]] for i in range(len(sizes))]
    w_xa, w_ya, w_qr, w_kr, w_vr, w_gr, w_ql, w_kvl, w_kpe = seg
    blk = dm.mix_b
    kv_cols = dm.kv_lora + 2 * dm.qk_rope
    kv_pad = (-kv_cols) % blk
    cols = [w_xa, w_ya, jnp.concatenate([w_qr, w_kr], axis=1), w_vr, w_gr, w_ql,
            jnp.concatenate([w_kvl, w_kpe, _rot_half_cols(w_kpe, dm.qk_rope), jnp.zeros((d, kv_pad), F32)], axis=1),
            jnp.concatenate([_rot_half_cols(w_qr, dm.ret_dk), _rot_half_cols(w_kr, dm.ret_dk)], axis=1)]
    assert all(c.shape[1] == blk for c in cols), [c.shape for c in cols]
    w_ext = jnp.concatenate(cols, axis=1).astype(BF16)

    n_lru = lru_w_a.shape[1]
    eye = jnp.eye(n_lru, dtype=F32)
    block_diag = lambda w: jnp.einsum("nde,nm->ndme", w, eye).reshape(dm.mix_a, dm.mix_a).astype(BF16)

    qk_dim = dm.qk_nope + dm.qk_rope
    wq3 = w_uq[l].reshape(q_lora, dm.mla_heads, qk_dim)
    pe = wq3[..., dm.qk_nope:]
    half = dm.qk_rope // 2
    pe_rot = jnp.concatenate([-pe[..., half:], pe[..., :half]], axis=-1)
    w_uq_ext = jnp.concatenate([wq3, pe_rot], axis=-1).transpose(1, 0, 2).astype(BF16)

    wo = w_out[l].astype(BF16)
    n_exp = dm.n_exp
    wr = jnp.concatenate([w_router[l], jnp.zeros((d, LANES - n_exp), F32)], axis=1)
    wr_hi = wr.astype(BF16)
    wr_lo = (wr - wr_hi.astype(F32)).astype(BF16)
    br = jnp.concatenate([b_router[l], jnp.full((LANES - n_exp,), NEG_BIG, F32)]).reshape(1, LANES)
    row = lambda v: v[l].reshape(1, -1)
    return dict(
        w_ext=w_ext, conv_w=conv_w[l], conv_b=row(conv_b),
        wa_bd=block_diag(lru_w_a[l]), wx_bd=block_diag(lru_w_x[l]),
        ba=row(lru_b_a), bx=row(lru_b_x), lam=row(lru_lam),
        g_q=row(g_q_norm), w_uq_ext=w_uq_ext, g_kv=row(g_kv_norm),
        w_uk=w_uk[l].transpose(1, 0, 2).astype(BF16), w_uv=w_uv[l].transpose(1, 0, 2).astype(BF16),
        wo_a=wo[:dm.mix_a], wo_b=wo[dm.mix_a:dm.mix_a + dm.mix_b], wo_c=wo[dm.mix_a + dm.mix_b:],
        wr_hi=wr_hi, wr_lo=wr_lo, br=br,
        g_mix_pre=row(g_mix_pre), g_mix_post=row(g_mix_post), g_ffn_pre=row(g_ffn_pre), g_ffn_post=row(g_ffn_post),
    )


def _mixer(x, mod, wl, dm, conv_buf, h0, s0, past_ckv, past_kpe):
    bsz, seq, d = x.shape
    past_len = 0 if past_ckv is None else past_ckv.shape[1]
    pos = past_len + jnp.arange(seq, dtype=jnp.int32)
    u = _in_proj(x, mod, wl["g_mix_pre"], wl["w_ext"])

    ctx_rows = 8
    n_ctx = dm.conv_w - 1
    conv_ctx = jnp.concatenate([jnp.zeros((bsz, ctx_rows - n_ctx, dm.mix_a), F32), conv_buf], axis=1)
    a_out, conv_new, h_new = _rg_lru(u, conv_ctx, h0.reshape(bsz, 1, dm.mix_a), wl["conv_w"], wl["conv_b"],
                                     wl["wa_bd"], wl["wx_bd"], wl["ba"], wl["bx"], wl["lam"])

    cos_k, sin_k = _rope_tables(pos, dm.ret_dk)
    k_scale = float(dm.ret_dk) ** -0.5
    scale_row = jnp.concatenate([jnp.ones((dm.ret_qk,), F32), jnp.full((dm.ret_qk,), k_scale, F32)])[None, :]
    cos_t = jnp.tile(cos_k, (1, 2 * dm.ret_heads)) * scale_row
    sin_t = jnp.tile(sin_k, (1, 2 * dm.ret_heads)) * scale_row
    b_out, s_new = _retention(u, s0, cos_t, sin_t, blocks=(2, 3, 4, 7))

    cos_r, sin_r = _rope_tables(pos, dm.qk_rope)
    lane_pad = lambda t, left, right, fill: jnp.concatenate(
        [jnp.full((seq, left), fill, F32), t, jnp.zeros((seq, right), F32)], axis=1)
    ckv_new, kpe_new = _latent(u, wl["g_kv"], lane_pad(cos_r, 0, LANES - dm.qk_rope, 0.0),
                               lane_pad(sin_r, 0, LANES - dm.qk_rope, 0.0), kv_blk=6, blk_w=dm.mix_b, rope_w=dm.qk_rope)
    q = _q_proj(u, wl["g_q"], wl["w_uq_ext"], lane_pad(cos_r, dm.qk_nope, dm.qk_rope, 1.0),
                lane_pad(sin_r, dm.qk_nope, dm.qk_rope, 0.0), ql_blk=5, rope_w=dm.qk_rope)
    if past_ckv is None:
        ckv_all, kpe_all = ckv_new, kpe_new
    else:
        past_kpe_pad = jnp.concatenate([past_kpe, jnp.zeros(past_kpe.shape[:2] + (LANES - dm.qk_rope,), F32)], axis=2)
        ckv_all = jnp.concatenate([past_ckv, ckv_new], axis=1)
        kpe_all = jnp.concatenate([past_kpe_pad, kpe_new], axis=1)
    k, v = _kv_proj(ckv_all, kpe_all, wl["w_uk"], wl["w_uv"])
    c_out = _attention(q, k, v, past_len, dm.qk_nope + dm.qk_rope)

    x1, h2, gates, idx = _out_proj(a_out, b_out, c_out, x, mod, wl["g_mix_post"], wl["g_ffn_pre"],
                                   wl["wo_a"], wl["wo_b"], wl["wo_c"], wl["wr_hi"], wl["wr_lo"], wl["br"])
    states = (ckv_new, kpe_new[:, :, :dm.qk_rope], conv_new[:, ctx_rows - n_ctx:], h_new.reshape(bsz, dm.mix_a), s_new)
    return x1, h2, gates, idx[:, :, :TOP_K], states


def kernel(x_prompt, x_sample, c_prompt, c_sample, cache_ckv, cache_kpe, state_conv, state_lru, state_ret, w_ada, b_ada, g_mix_pre, g_mix_post, g_ffn_pre, g_ffn_post, w_in, conv_w, conv_b, lru_w_a, lru_b_a, lru_w_x, lru_b_x, lru_lam, g_q_norm, w_uq, g_kv_norm, w_uk, w_uv, w_out, w_router, b_router, w_gate_up, b_gate_up, w_down, b_down):
    depth = w_in.shape[0]
    dm = _Dims(state_ret, cache_ckv, cache_kpe, lru_w_a, w_uk, w_uv, conv_w, w_router, w_down)
    bp, lp, d = x_prompt.shape
    bs, ls, _ = x_sample.shape
    n_tok = bp * lp + bs * ls
    moe_tm = min(1024, max(16, 1 << int(np.log2((n_tok * TOP_K) // dm.n_exp))))
    n_blocks = -(-(n_tok * TOP_K) // moe_tm) + dm.n_exp

    xp, xs = x_prompt, x_sample
    p_states, s_states = [], []
    for l in range(depth):
        wl = _layer_weights(l, dm, w_in, conv_w, conv_b, lru_w_a, lru_b_a, lru_w_x, lru_b_x, lru_lam, g_q_norm, w_uq,
                            g_kv_norm, w_uk, w_uv, w_out, w_router, b_router, g_mix_pre, g_mix_post, g_ffn_pre, g_ffn_post)
        mod_p = _ada_mod(c_prompt, w_ada, b_ada, l).reshape(bp, 6, d)
        mod_s = _ada_mod(c_sample, w_ada, b_ada, l).reshape(bs, 6, d)
        zeros = lambda *s: jnp.zeros(s, F32)
        x1p, h2p, gp, ip, st_p = _mixer(
            xp, mod_p, wl, dm, zeros(bp, dm.conv_w - 1, dm.mix_a), zeros(bp, dm.mix_a),
            zeros(bp, dm.ret_heads, dm.ret_dk, dm.ret_dv), None, None)
        x1s, h2s, gs, is_, st_s = _mixer(
            xs, mod_s, wl, dm, state_conv[l], state_lru[l], state_ret[l], cache_ckv[l], cache_kpe[l])
        p_states.append(st_p)
        s_states.append(st_s)

        h2 = jnp.concatenate([h2p.reshape(-1, d // 2), h2s.reshape(-1, d // 2)], axis=0)
        top_idx = jnp.concatenate([ip.reshape(-1, TOP_K), is_.reshape(-1, TOP_K)], axis=0)
        src_rows, dst_rows, block_e, n_used, n_out_rows = _route(top_idx, dm.n_exp, moe_tm, n_blocks)
        rows = _moe_ffn(h2, src_rows, dst_rows, block_e, n_used, w_gate_up, b_gate_up, w_down, b_down, l, n_out_rows)
        xp = _ffn_post(x1p, rows, gp, mod_p, wl["g_ffn_post"], 0, n_tok)
        xs = _ffn_post(x1s, rows, gs, mod_s, wl["g_ffn_post"], bp * lp, n_tok)

    stack = lambda sts: tuple(jnp.stack(t, axis=0) for t in zip(*sts))
    p_ckv, p_kpe, p_conv, p_lru, p_ret = stack(p_states)
    s_ckv, s_kpe, s_conv, s_lru, s_ret = stack(s_states)
    return (xp, xs, p_ckv, p_kpe, p_conv, p_lru, p_ret, s_ckv, s_kpe, s_conv, s_lru, s_ret)
```

```python
import functools

import numpy as np
import jax
import jax.numpy as jnp
from jax import lax
from jax.experimental import pallas as pl
from jax.experimental.pallas import tpu as pltpu

CHUNK = 64
EPS = 1e-6
ROPE_BASE = 10000.0
RG_C = 8.0
TOP_K = 4
SWIGLU_LIMIT = 7.0
SWIGLU_ALPHA = 1.702
NEG_BIG = -1e30
GELU_C = float(np.sqrt(2.0 / np.pi))

LANES = 128
VMEM_LIMIT_BYTES = 56 * 1024 * 1024

F32 = jnp.float32
BF16 = jnp.bfloat16


def _params(semantics):
    return pltpu.CompilerParams(dimension_semantics=semantics, vmem_limit_bytes=VMEM_LIMIT_BYTES)


def _tile(n, pref):
    if n <= pref:
        return n
    t = pref
    while n % t:
        t //= 2
    return t


def _rms(x, g):
    return x * lax.rsqrt(jnp.mean(x * x, axis=-1, keepdims=True) + EPS) * g


def _dot(a, b):
    return jnp.dot(a, b, preferred_element_type=F32)


def _dot_nt(a, b):
    return lax.dot_general(a, b, (((1,), (1,)), ((), ())), preferred_element_type=F32)


def _dot_tn(a, b):
    return lax.dot_general(a, b, (((0,), (0,)), ((), ())), preferred_element_type=F32)


def _sigmoid(x):
    return 1.0 / (1.0 + jnp.exp(-x))


def _pack_bf16_halves(x):
    n = x.shape[1] // 2
    bits = lax.bitcast_convert_type(x.astype(BF16).astype(F32), jnp.uint32)
    return (bits[:, :n] >> 16) | (bits[:, n:] & jnp.uint32(0xFFFF0000))


def _unpack_bf16_halves(w):
    lo = lax.bitcast_convert_type(w << 16, F32)
    hi = lax.bitcast_convert_type(w & jnp.uint32(0xFFFF0000), F32)
    return lo, hi


def _ada_kernel(c_ref, w_ref, b_ref, o_ref):
    c = c_ref[...]
    s = (c * _sigmoid(c)).astype(BF16)
    o_ref[...] = _dot(s, w_ref[...].astype(BF16)) + b_ref[...]


def _ada_mod(c, w_ada, b_ada, layer):
    bsz, d = c.shape
    n = w_ada.shape[-1]
    tn = _tile(n, 1024)
    return pl.pallas_call(
        _ada_kernel,
        grid=(n // tn,),
        in_specs=[
            pl.BlockSpec((bsz, d), lambda j: (0, 0)),
            pl.BlockSpec((None, d, tn), lambda j: (layer, 0, j)),
            pl.BlockSpec((None, 1, tn), lambda j: (layer, 0, j)),
        ],
        out_specs=pl.BlockSpec((bsz, tn), lambda j: (0, j)),
        out_shape=jax.ShapeDtypeStruct((bsz, n), F32),
        compiler_params=_params(("arbitrary",)),
        name="ada_mod",
    )(c, w_ada, b_ada.reshape(b_ada.shape[0], 1, n))


def _in_proj_kernel(x_ref, mod_ref, g_ref, w_ref, u_ref):
    mod = mod_ref[0]
    h = _rms(x_ref[0], g_ref[...]) * (1.0 + mod[1:2]) + mod[0:1]
    u_ref[0] = _dot(h.astype(BF16), w_ref[...])


def _in_proj(x, mod, g, w_ext):
    bsz, seq, d = x.shape
    n = w_ext.shape[1]
    tm = _tile(seq, 256)
    return pl.pallas_call(
        _in_proj_kernel,
        grid=(bsz, seq // tm),
        in_specs=[
            pl.BlockSpec((1, tm, d), lambda b, i: (b, i, 0)),
            pl.BlockSpec((1, 6, d), lambda b, i: (b, 0, 0)),
            pl.BlockSpec((1, d), lambda b, i: (0, 0)),
            pl.BlockSpec((d, n), lambda b, i: (0, 0)),
        ],
        out_specs=pl.BlockSpec((1, tm, n), lambda b, i: (b, i, 0)),
        out_shape=jax.ShapeDtypeStruct((bsz, seq, n), F32),
        compiler_params=_params(("parallel", "parallel")),
        name="in_proj",
    )(x, mod, g, w_ext)


def _lru_kernel(xa_ref, ya_ref, cbuf_ref, h0_ref, cw_ref, cb_ref, wa_ref, wx_ref, ba_ref, bx_ref,
                lam_ref, out_ref, cnew_ref, hlast_ref, xbuf, hcar):
    tl = xa_ref.shape[1]
    width = xa_ref.shape[2]
    ctx = xbuf.shape[0] - tl

    @pl.when(pl.program_id(1) == 0)
    def _():
        xbuf[0:ctx, :] = cbuf_ref[0]
        hcar[...] = h0_ref[0]

    xa = xa_ref[0]
    xbuf[ctx:ctx + tl, :] = xa
    cw = cw_ref[...]
    n_tap = cw.shape[0]
    xc = cb_ref[...] + cw[n_tap - 1:n_tap] * xa
    for k in range(n_tap - 1):
        off = ctx - (n_tap - 1) + k
        xc = xc + cw[k:k + 1] * xbuf[off:off + tl, :]
    new_ctx = xbuf[tl:tl + ctx, :]
    xbuf[0:ctx, :] = new_ctx
    cnew_ref[0] = new_ctx

    xcb = xc.astype(BF16)
    r = _sigmoid(_dot(xcb, wa_ref[...]) + ba_ref[...])
    gi = _sigmoid(_dot(xcb, wx_ref[...]) + bx_ref[...])
    z = -lam_ref[...]
    softplus = jnp.maximum(z, 0.0) + jnp.log1p(jnp.exp(-jnp.abs(z)))
    log_a = (-RG_C) * r * softplus
    a = jnp.exp(log_a)
    b = jnp.sqrt(1.0 - jnp.exp(2.0 * log_a)) * gi * xc

    row = lax.broadcasted_iota(jnp.int32, (tl, width), 0)
    s = 1
    while s < tl:
        a_sh = pltpu.roll(a, s, 0)
        b_sh = pltpu.roll(b, s, 0)
        valid = row >= s
        b = jnp.where(valid, a * b_sh + b, b)
        a = jnp.where(valid, a * a_sh, a)
        s *= 2
    h = a * hcar[...] + b
    h_last = h[tl - 1:tl, :]
    hcar[...] = h_last
    hlast_ref[0] = h_last

    ya = ya_ref[0]
    gelu = 0.5 * ya * (1.0 + jnp.tanh(GELU_C * (ya + 0.044715 * (ya * ya * ya))))
    out_ref[0] = (h * gelu).astype(out_ref.dtype)


def _rg_lru(u, conv_ctx, h0, cw, cb, wa_bd, wx_bd, ba, bx, lam):
    bsz, seq, _ = u.shape
    width = cw.shape[1]
    tl = _tile(seq, 512)
    ctx = conv_ctx.shape[1]
    xa_blk = 0
    ya_blk = 1
    vec = lambda: pl.BlockSpec((1, width), lambda b, t: (0, 0))
    return pl.pallas_call(
        _lru_kernel,
        grid=(bsz, seq // tl),
        in_specs=[
            pl.BlockSpec((1, tl, width), lambda b, t: (b, t, xa_blk)),
            pl.BlockSpec((1, tl, width), lambda b, t: (b, t, ya_blk)),
            pl.BlockSpec((1, ctx, width), lambda b, t: (b, 0, 0)),
            pl.BlockSpec((1, 1, width), lambda b, t: (b, 0, 0)),
            pl.BlockSpec(cw.shape, lambda b, t: (0, 0)),
            vec(),
            pl.BlockSpec((width, width), lambda b, t: (0, 0)),
            pl.BlockSpec((width, width), lambda b, t: (0, 0)),
            vec(), vec(), vec(),
        ],
        out_specs=[
            pl.BlockSpec((1, tl, width), lambda b, t: (b, t, 0)),
            pl.BlockSpec((1, ctx, width), lambda b, t: (b, 0, 0)),
            pl.BlockSpec((1, 1, width), lambda b, t: (b, 0, 0)),
        ],
        out_shape=[
            jax.ShapeDtypeStruct((bsz, seq, width), BF16),
            jax.ShapeDtypeStruct((bsz, ctx, width), F32),
            jax.ShapeDtypeStruct((bsz, 1, width), F32),
        ],
        scratch_shapes=[pltpu.VMEM((tl + ctx, width), F32), pltpu.VMEM((1, width), F32)],
        compiler_params=_params(("parallel", "arbitrary")),
        name="rg_lru",
    )(u, u, conv_ctx, h0, cw, cb, wa_bd, wx_bd, ba, bx, lam)


def _ret_kernel(qk_ref, rot_ref, v_ref, g_ref, cos_ref, sin_ref, dmask_ref, qdec_ref, kdec_ref,
                s0_ref, out_ref, slast_ref, s_sc, *, gchunk):
    n_heads, dk, dv = s_sc.shape
    hdk = n_heads * dk

    @pl.when(pl.program_id(1) == 0)
    def _():
        s_sc[...] = s0_ref[0]

    qkr = qk_ref[0] * cos_ref[...] + rot_ref[0] * sin_ref[...]
    v = v_ref[0]
    vdec = v * kdec_ref[...]
    gate = g_ref[0]
    qdec = qdec_ref[...]
    for h in range(n_heads):
        q = qkr[:, h * dk:(h + 1) * dk].astype(BF16)
        k = qkr[:, hdk + h * dk:hdk + (h + 1) * dk].astype(BF16)
        vh = v[:, h * dv:(h + 1) * dv].astype(BF16)
        vd = vdec[:, h * dv:(h + 1) * dv].astype(BF16)
        s_prev = s_sc[h]
        scores = _dot_nt(q, k) * dmask_ref[h]
        o = _dot(scores.astype(BF16), vh) + qdec[:, h * dv:(h + 1) * dv] * _dot(q, s_prev.astype(BF16))
        s_sc[h] = s_prev * gchunk[h] + _dot_tn(k, vd)
        o = o * lax.rsqrt(jnp.mean(o * o, axis=-1, keepdims=True) + EPS)
        gh = gate[:, h * dv:(h + 1) * dv]
        out_ref[0, :, h * dv:(h + 1) * dv] = (o * (gh * _sigmoid(gh))).astype(out_ref.dtype)
    slast_ref[0] = s_sc[...]


def _retention(u, s0, cos_t, sin_t, blocks):
    bsz, seq, _ = u.shape
    _, n_heads, dk, dv = s0.shape
    width = n_heads * dv
    tc = _tile(seq, 256)
    log_g = np.log1p(-np.exp2(-5.0 - np.arange(n_heads, dtype=np.float64)))
    idx = np.arange(tc, dtype=np.float64)
    diff = idx[:, None] - idx[None, :]
    dmask = np.where(diff[None] >= 0, np.exp(np.maximum(diff, 0.0)[None] * log_g[:, None, None]), 0.0)
    qdec = np.repeat(np.exp((idx + 1.0)[:, None] * log_g[None, :]), dv, axis=1)
    kdec = np.repeat(np.exp((tc - 1.0 - idx)[:, None] * log_g[None, :]), dv, axis=1)
    gchunk = tuple(float(g) for g in np.exp(tc * log_g))
    qk_blk, v_blk, g_blk, rot_blk = blocks
    u_spec = lambda blk: pl.BlockSpec((1, tc, width), lambda b, n: (b, n, blk))
    full2 = lambda a: pl.BlockSpec(a.shape, lambda b, n: (0, 0))
    return pl.pallas_call(
        functools.partial(_ret_kernel, gchunk=gchunk),
        grid=(bsz, seq // tc),
        in_specs=[
            u_spec(qk_blk), u_spec(rot_blk), u_spec(v_blk), u_spec(g_blk),
            pl.BlockSpec((tc, width), lambda b, n: (n, 0)),
            pl.BlockSpec((tc, width), lambda b, n: (n, 0)),
            pl.BlockSpec(dmask.shape, lambda b, n: (0, 0, 0)),
            full2(qdec), full2(kdec),
            pl.BlockSpec((1, n_heads, dk, dv), lambda b, n: (b, 0, 0, 0)),
        ],
        out_specs=[
            pl.BlockSpec((1, tc, width), lambda b, n: (b, n, 0)),
            pl.BlockSpec((1, n_heads, dk, dv), lambda b, n: (b, 0, 0, 0)),
        ],
        out_shape=[
            jax.ShapeDtypeStruct((bsz, seq, width), BF16),
            jax.ShapeDtypeStruct((bsz, n_heads, dk, dv), F32),
        ],
        scratch_shapes=[pltpu.VMEM((n_heads, dk, dv), F32)],
        compiler_params=_params(("parallel", "arbitrary")),
        name="retention",
    )(u, u, u, u, cos_t, sin_t, jnp.asarray(dmask, F32), jnp.asarray(qdec, F32),
      jnp.asarray(kdec, F32), s0)


def _latent_kernel(kv_ref, g_ref, cos_ref, sin_ref, ckv_ref, kpe_ref, *, rope_w):
    lora = ckv_ref.shape[2]
    blk = kv_ref[0]
    ckv_ref[0] = _rms(blk[:, :lora], g_ref[...])
    kp = blk[:, lora:lora + LANES]
    rolled = pltpu.roll(kp, LANES - rope_w, 1)
    kpe_ref[0] = kp * cos_ref[...] + rolled * sin_ref[...]


def _latent(u, g_kv, cos_p, sin_p, kv_blk, blk_w, rope_w):
    bsz, seq, _ = u.shape
    lora = g_kv.shape[1]
    tm = _tile(seq, 512)
    return pl.pallas_call(
        functools.partial(_latent_kernel, rope_w=rope_w),
        grid=(bsz, seq // tm),
        in_specs=[
            pl.BlockSpec((1, tm, blk_w), lambda b, i: (b, i, kv_blk)),
            pl.BlockSpec((1, lora), lambda b, i: (0, 0)),
            pl.BlockSpec((tm, LANES), lambda b, i: (i, 0)),
            pl.BlockSpec((tm, LANES), lambda b, i: (i, 0)),
        ],
        out_specs=[
            pl.BlockSpec((1, tm, lora), lambda b, i: (b, i, 0)),
            pl.BlockSpec((1, tm, LANES), lambda b, i: (b, i, 0)),
        ],
        out_shape=[
            jax.ShapeDtypeStruct((bsz, seq, lora), F32),
            jax.ShapeDtypeStruct((bsz, seq, LANES), F32),
        ],
        compiler_params=_params(("parallel", "parallel")),
        name="latent_norm_rope",
    )(u, g_kv, cos_p, sin_p)


def _q_proj_kernel(ql_ref, g_ref, w_ref, cos_ref, sin_ref, q_ref, *, rope_w):
    cq = _rms(ql_ref[0], g_ref[...]).astype(BF16)
    cos = cos_ref[...]
    sin = sin_ref[...]
    for h in range(w_ref.shape[0]):
        r = _dot(cq, w_ref[h])
        rolled = pltpu.roll(r, r.shape[1] - rope_w, 1)
        q_ref[0, h] = (r * cos + rolled * sin).astype(q_ref.dtype)


def _q_proj(u, g_q, w_uq_ext, cos_q, sin_q, ql_blk, rope_w):
    bsz, seq, _ = u.shape
    n_heads, q_lora, width = w_uq_ext.shape
    tm = _tile(seq, 512)
    return pl.pallas_call(
        functools.partial(_q_proj_kernel, rope_w=rope_w),
        grid=(bsz, seq // tm),
        in_specs=[
            pl.BlockSpec((1, tm, q_lora), lambda b, i: (b, i, ql_blk)),
            pl.BlockSpec((1, q_lora), lambda b, i: (0, 0)),
            pl.BlockSpec((n_heads, q_lora, width), lambda b, i: (0, 0, 0)),
            pl.BlockSpec((tm, width), lambda b, i: (i, 0)),
            pl.BlockSpec((tm, width), lambda b, i: (i, 0)),
        ],
        out_specs=pl.BlockSpec((1, n_heads, tm, width), lambda b, i: (b, 0, i, 0)),
        out_shape=jax.ShapeDtypeStruct((bsz, n_heads, seq, width), BF16),
        compiler_params=_params(("parallel", "parallel")),
        name="mla_q_proj",
    )(u, g_q, w_uq_ext, cos_q, sin_q)


def _kv_proj_kernel(ckv_ref, kpe_ref, wk_ref, wv_ref, k_ref, v_ref):
    ckv = ckv_ref[0].astype(BF16)
    kpe = kpe_ref[0].astype(k_ref.dtype)
    nope = wk_ref.shape[2]
    for h in range(wk_ref.shape[0]):
        k_ref[0, h, :, :nope] = _dot(ckv, wk_ref[h]).astype(k_ref.dtype)
        k_ref[0, h, :, nope:] = kpe
        v_ref[0, h] = _dot(ckv, wv_ref[h]).astype(v_ref.dtype)


def _kv_proj(ckv, kpe, w_uk, w_uv):
    bsz, klen, lora = ckv.shape
    n_heads, _, nope = w_uk.shape
    vdim = w_uv.shape[2]
    tm = _tile(klen, 512) if klen % 512 == 0 else klen
    return pl.pallas_call(
        _kv_proj_kernel,
        grid=(bsz, klen // tm),
        in_specs=[
            pl.BlockSpec((1, tm, lora), lambda b, i: (b, i, 0)),
            pl.BlockSpec((1, tm, LANES), lambda b, i: (b, i, 0)),
            pl.BlockSpec((n_heads, lora, nope), lambda b, i: (0, 0, 0)),
            pl.BlockSpec((n_heads, lora, vdim), lambda b, i: (0, 0, 0)),
        ],
        out_specs=[
            pl.BlockSpec((1, n_heads, tm, nope + LANES), lambda b, i: (b, 0, i, 0)),
            pl.BlockSpec((1, n_heads, tm, vdim), lambda b, i: (b, 0, i, 0)),
        ],
        out_shape=[
            jax.ShapeDtypeStruct((bsz, n_heads, klen, nope + LANES), BF16),
            jax.ShapeDtypeStruct((bsz, n_heads, klen, vdim), BF16),
        ],
        compiler_params=_params(("parallel", "parallel")),
        name="mla_kv_proj",
    )(ckv, kpe, w_uk, w_uv)


def _attn_kernel(q_ref, k_ref, v_ref, o_ref, m_sc, l_sc, acc_sc, s_sc, *, q_pos0, scale, tk):
    qi = pl.program_id(2)
    tq = q_ref.shape[2]
    nk = k_ref.shape[2] // tk
    q = q_ref[0, 0]
    m_sc[...] = jnp.full(m_sc.shape, NEG_BIG, F32)
    l_sc[...] = jnp.zeros(l_sc.shape, F32)
    acc_sc[...] = jnp.zeros(acc_sc.shape, F32)

    last_q_chunk = (q_pos0 + qi * tq + tq - 1) // CHUNK
    n_seen = jnp.minimum((last_q_chunk * CHUNK + CHUNK - 1) // tk + 1, nk)
    q_chunk = (q_pos0 + qi * tq + lax.broadcasted_iota(jnp.int32, (tq, 1), 0)) // CHUNK
    k_chunk0 = lax.broadcasted_iota(jnp.int32, (1, tk), 1) // CHUNK

    def scores(j):
        start = pl.multiple_of(j * tk, tk)
        return _dot_nt(q, k_ref[0, 0, pl.ds(start, tk), :])

    s_sc[0] = scores(0)

    def step(j, carry):
        slot = j % 2
        s = s_sc[slot]
        s_sc[1 - slot] = scores(jnp.minimum(j + 1, n_seen - 1))
        start = pl.multiple_of(j * tk, tk)
        s = jnp.where(k_chunk0 + start // CHUNK <= q_chunk, s * scale, NEG_BIG)
        m_prev = m_sc[...]
        m_new = jnp.maximum(m_prev, jnp.max(s, axis=-1, keepdims=True))
        alpha = jnp.exp(m_prev - m_new)
        p = jnp.exp(s - m_new)
        l_sc[...] = alpha * l_sc[...] + jnp.sum(p, axis=-1, keepdims=True)
        acc_sc[...] = alpha * acc_sc[...] + _dot(p.astype(BF16), v_ref[0, 0, pl.ds(start, tk), :])
        m_sc[...] = m_new
        return carry

    lax.fori_loop(0, n_seen, step, 0)
    o_ref[0] = (acc_sc[...] / l_sc[...]).astype(o_ref.dtype)


def _attention(q, k, v, q_pos0, qk_dim):
    bsz, n_heads, seq, width = q.shape
    klen = k.shape[2]
    vdim = v.shape[3]
    tq = _tile(seq, 512)
    tk = _tile(klen, 512) if klen % 512 == 0 else klen
    assert tk == klen or tk % CHUNK == 0
    return pl.pallas_call(
        functools.partial(_attn_kernel, q_pos0=q_pos0, scale=float(qk_dim) ** -0.5, tk=tk),
        grid=(bsz, n_heads, seq // tq),
        in_specs=[
            pl.BlockSpec((1, 1, tq, width), lambda b, h, qi: (b, h, qi, 0)),
            pl.BlockSpec((1, 1, klen, width), lambda b, h, qi: (b, h, 0, 0)),
            pl.BlockSpec((1, 1, klen, vdim), lambda b, h, qi: (b, h, 0, 0)),
        ],
        out_specs=pl.BlockSpec((1, tq, vdim), lambda b, h, qi: (b, qi, h)),
        out_shape=jax.ShapeDtypeStruct((bsz, seq, n_heads * vdim), BF16),
        scratch_shapes=[pltpu.VMEM((tq, 1), F32), pltpu.VMEM((tq, 1), F32), pltpu.VMEM((tq, vdim), F32),
                        pltpu.VMEM((2, tq, tk), F32)],
        compiler_params=_params(("parallel", "parallel", "arbitrary")),
        name="mla_attention",
    )(q, k, v)


def _out_proj_kernel(a_ref, b_ref, c_ref, x_ref, mod_ref, gpost_ref, gpre_ref, wa_ref, wb_ref, wc_ref,
                     wrh_ref, wrl_ref, br_ref, x1_ref, h2_ref, gate_ref, idx_ref):
    mix = _dot(a_ref[0], wa_ref[...]) + _dot(b_ref[0], wb_ref[...]) + _dot(c_ref[0], wc_ref[...])
    mod = mod_ref[0]
    x1 = x_ref[0] + mod[2:3] * _rms(mix, gpost_ref[...])
    x1_ref[0] = x1
    h2 = _rms(x1, gpre_ref[...]) * (1.0 + mod[4:5]) + mod[3:4]
    h2_ref[0] = _pack_bf16_halves(h2)
    hi = h2.astype(BF16)
    lo = (h2 - hi.astype(F32)).astype(BF16)
    logits = _dot(hi, wrh_ref[...]) + (_dot(hi, wrl_ref[...]) + _dot(lo, wrh_ref[...])) + br_ref[...]
    lane = lax.broadcasted_iota(jnp.int32, logits.shape, 1)
    lane_f = lane.astype(F32)
    gates = jnp.zeros(logits.shape, F32)
    idxs = jnp.zeros(logits.shape, F32)
    top0 = None
    denom = None
    for k in range(TOP_K):
        m = jnp.max(logits, axis=-1, keepdims=True)
        sel = jnp.min(jnp.where(logits == m, lane_f, float(LANES)), axis=-1, keepdims=True)
        if k == 0:
            top0 = m
            e = jnp.ones_like(m)
            denom = e
        else:
            e = jnp.exp(m - top0)
            denom = denom + e
        gates = jnp.where(lane == k, e, gates)
        idxs = jnp.where(lane == k, sel, idxs)
        logits = jnp.where(lane_f == sel, NEG_BIG * 2.0, logits)
    gate_ref[0] = gates / denom
    idx_ref[0] = idxs.astype(jnp.int32)


def _out_proj(a_out, b_out, c_out, x, mod, g_post, g_pre, wa, wb, wc, wr_hi, wr_lo, br):
    bsz, seq, d = x.shape
    tm = _tile(seq, 256)
    row = lambda w: pl.BlockSpec((1, tm, w), lambda b, i: (b, i, 0))
    const = lambda a: pl.BlockSpec(a.shape, lambda b, i: (0, 0))
    return pl.pallas_call(
        _out_proj_kernel,
        grid=(bsz, seq // tm),
        in_specs=[
            row(a_out.shape[2]), row(b_out.shape[2]), row(c_out.shape[2]), row(d),
            pl.BlockSpec((1, 6, d), lambda b, i: (b, 0, 0)),
            const(g_post), const(g_pre), const(wa), const(wb), const(wc),
            const(wr_hi), const(wr_lo), const(br),
        ],
        out_specs=[row(d), row(d // 2), row(LANES), row(LANES)],
        out_shape=[
            jax.ShapeDtypeStruct((bsz, seq, d), F32),
            jax.ShapeDtypeStruct((bsz, seq, d // 2), jnp.uint32),
            jax.ShapeDtypeStruct((bsz, seq, LANES), F32),
            jax.ShapeDtypeStruct((bsz, seq, LANES), jnp.int32),
        ],
        compiler_params=_params(("parallel", "parallel")),
        name="out_proj_router",
    )(a_out, b_out, c_out, x, mod, g_post, g_pre, wa, wb, wc, wr_hi, wr_lo, br)


def _moe_kernel(be_ref, nu_ref, src_ref, dst_ref, h_hbm, wg_ref, wu_ref, bg_ref, bu_ref, wd_ref, bd_ref,
                rows_hbm, xbuf, xb, acc, pbuf, gsem, ssem):
    g = pl.program_id(0)
    f = pl.program_id(1)
    nf = pl.num_programs(1)
    n_used = nu_ref[0]
    tm, half = xbuf.shape
    chunk = tm // nf
    live = jnp.logical_and(g >= 1, g <= n_used)

    def gather_copy(r):
        return pltpu.make_async_copy(h_hbm.at[pl.ds(src_ref[0, r], 1)], xbuf.at[pl.ds(r, 1)], gsem)

    def scatter_copy(r):
        return pltpu.make_async_copy(pbuf.at[pl.ds(r, 1)], rows_hbm.at[pl.ds(dst_ref[0, r], 1)], ssem)

    def wait_gather():
        pltpu.make_async_copy(h_hbm.at[pl.ds(0, tm)], xbuf, gsem).wait()

    def wait_scatter():
        pltpu.make_async_copy(pbuf, rows_hbm.at[pl.ds(0, tm)], ssem).wait()

    @pl.when(jnp.logical_and(g == 0, f == 0))
    def _():
        pbuf[...] = jnp.zeros(pbuf.shape, pbuf.dtype)
        lax.fori_loop(0, tm, lambda r, c: (gather_copy(r).start(), c)[1], 0)

    @pl.when(jnp.logical_and(live, f == 0))
    def _():
        wait_gather()
        lo, hi = _unpack_bf16_halves(xbuf[...])
        xb[:, :half] = lo.astype(BF16)
        xb[:, half:] = hi.astype(BF16)
        acc[...] = jnp.broadcast_to(bd_ref[...], acc.shape)

    @pl.when(live)
    def _():
        for i in range(chunk):
            r = f * chunk + i
            gather_copy(r).start()
            scatter_copy(r).start()
        x = xb[...]
        gate = _dot(x, wg_ref[...]) + bg_ref[...]
        up = _dot(x, wu_ref[...]) + bu_ref[...]
        x_glu = jnp.minimum(gate, SWIGLU_LIMIT)
        x_lin = jnp.clip(up, -SWIGLU_LIMIT, SWIGLU_LIMIT)
        act = x_glu * _sigmoid(SWIGLU_ALPHA * x_glu) * (x_lin + 1.0)
        acc[...] += _dot(act.astype(BF16), wd_ref[...])

    @pl.when(jnp.logical_and(live, f == nf - 1))
    def _():
        wait_scatter()
        pbuf[...] = _pack_bf16_halves(acc[...])

    @pl.when(jnp.logical_and(g == n_used + 1, f == 0))
    def _():
        wait_gather()
        lax.fori_loop(0, tm, lambda r, c: (scatter_copy(r).start(), c)[1], 0)
        wait_scatter()


def _cast_kernel(w_ref, o_ref):
    o_ref[...] = w_ref[...].astype(o_ref.dtype)


def _layer_bf16(w, layer):
    _, n_exp, rows, cols = w.shape
    tr = _tile(rows, 256)
    return pl.pallas_call(
        _cast_kernel,
        grid=(n_exp, rows // tr),
        in_specs=[pl.BlockSpec((None, None, tr, cols), lambda e, i: (layer, e, i, 0))],
        out_specs=pl.BlockSpec((None, tr, cols), lambda e, i: (e, i, 0)),
        out_shape=jax.ShapeDtypeStruct((n_exp, rows, cols), BF16),
        compiler_params=_params(("parallel", "parallel")),
        name="expert_weights_bf16",
    )(w)


def _moe_ffn(h_all, src_rows, dst_rows, block_e, n_used, w_gu, b_gu, w_dn, b_dn, layer, n_out_rows):
    n_grid, tm = src_rows.shape
    half = h_all.shape[1]
    d = 2 * half
    d_exp = w_dn.shape[1]
    tf = _tile(d_exp, 512)
    nf = d_exp // tf
    assert tm % nf == 0 and h_all.shape[0] >= tm and n_out_rows >= tm
    n_exp = w_gu.shape[0]
    b_gu4 = b_gu.reshape(b_gu.shape[0], n_exp, 1, 2 * d_exp)
    b_dn4 = b_dn.reshape(b_dn.shape[0], n_exp, 1, d)

    def f_idx(g, f, nu):
        return jnp.where(jnp.logical_and(g >= 1, g <= nu[0]), f, nf - 1)

    def e_idx(g, be, nu):
        return be[jnp.clip(g - 1, 0, nu[0] - 1)]

    grid_spec = pltpu.PrefetchScalarGridSpec(
        num_scalar_prefetch=2,
        grid=(n_grid, nf),
        in_specs=[
            pl.BlockSpec((None, 1, tm), lambda g, f, be, nu: (g, 0, 0), memory_space=pltpu.SMEM),
            pl.BlockSpec((None, 1, tm), lambda g, f, be, nu: (jnp.maximum(g - 2, 0), 0, 0), memory_space=pltpu.SMEM),
            pl.BlockSpec(memory_space=pl.ANY),
            pl.BlockSpec((None, d, tf), lambda g, f, be, nu: (e_idx(g, be, nu), 0, f_idx(g, f, nu))),
            pl.BlockSpec((None, d, tf), lambda g, f, be, nu: (e_idx(g, be, nu), 0, nf + f_idx(g, f, nu))),
            pl.BlockSpec((None, None, 1, tf), lambda g, f, be, nu: (layer, e_idx(g, be, nu), 0, f_idx(g, f, nu))),
            pl.BlockSpec((None, None, 1, tf), lambda g, f, be, nu: (layer, e_idx(g, be, nu), 0, nf + f_idx(g, f, nu))),
            pl.BlockSpec((None, tf, d), lambda g, f, be, nu: (e_idx(g, be, nu), f_idx(g, f, nu), 0)),
            pl.BlockSpec((None, None, 1, d), lambda g, f, be, nu: (layer, e_idx(g, be, nu), 0, 0)),
        ],
        out_specs=pl.BlockSpec(memory_space=pl.ANY),
        scratch_shapes=[
            pltpu.VMEM((tm, half), jnp.uint32),
            pltpu.VMEM((tm, d), BF16),
            pltpu.VMEM((tm, d), F32),
            pltpu.VMEM((tm, half), jnp.uint32),
            pltpu.SemaphoreType.DMA(()),
            pltpu.SemaphoreType.DMA(()),
        ],
    )
    return pl.pallas_call(
        _moe_kernel,
        grid_spec=grid_spec,
        out_shape=jax.ShapeDtypeStruct((n_out_rows, half), jnp.uint32),
        compiler_params=_params(("arbitrary", "arbitrary")),
        name="moe_expert_ffn",
    )(block_e, n_used, src_rows.reshape(n_grid, 1, tm), dst_rows.reshape(n_grid, 1, tm), h_all,
      w_gu, w_gu, b_gu4, b_gu4, w_dn, b_dn4)


def _route(top_idx, n_exp, tm, n_blocks):
    n_tok, top_k = top_idx.shape
    n_rows = n_tok * top_k
    flat_e = top_idx.reshape(-1)
    onehot = (flat_e[:, None] == jnp.arange(n_exp, dtype=jnp.int32)[None, :]).astype(jnp.int32)
    csum = jnp.cumsum(onehot, axis=0)
    rank = jnp.sum(csum * onehot, axis=1) - 1
    counts = csum[-1]
    padded = (counts + tm - 1) // tm * tm
    pad_end = jnp.cumsum(padded)
    pad_start = pad_end - padded
    dest = pad_start[flat_e] + rank
    n_used = (pad_end[-1] // tm).astype(jnp.int32).reshape(1)
    block_start = jnp.arange(n_blocks, dtype=jnp.int32) * tm
    block_e = jnp.minimum(jnp.sum((pad_end[None, :] <= block_start[:, None]).astype(jnp.int32), axis=1), n_exp - 1)
    n_pos = (n_blocks + 2) * tm
    row_of_pos = jnp.full((n_pos,), -1, jnp.int32).at[dest].set(jnp.arange(n_rows, dtype=jnp.int32))
    pos = jnp.arange(n_pos, dtype=jnp.int32)
    src_rows = jnp.where(row_of_pos >= 0, row_of_pos // top_k, 0).reshape(n_blocks + 2, tm)
    dst_rows = jnp.where(row_of_pos >= 0, (row_of_pos % top_k) * n_tok + row_of_pos // top_k,
                         n_rows + pos).reshape(n_blocks + 2, tm)
    return src_rows, dst_rows, block_e, n_used, n_rows + n_pos


def _ffn_post_kernel(x1_ref, *refs):
    row_refs, (gate_ref, mod_ref, g_ref, o_ref) = refs[:TOP_K], refs[TOP_K:]
    gates = gate_ref[0]
    half = row_refs[0].shape[1]
    ffn_lo = ffn_hi = None
    for k in range(TOP_K):
        lo, hi = _unpack_bf16_halves(row_refs[k][...])
        gk = gates[:, k:k + 1]
        ffn_lo = gk * lo if k == 0 else ffn_lo + gk * lo
        ffn_hi = gk * hi if k == 0 else ffn_hi + gk * hi
    ssq = jnp.sum(ffn_lo * ffn_lo, axis=-1, keepdims=True) + jnp.sum(ffn_hi * ffn_hi, axis=-1, keepdims=True)
    inv = lax.rsqrt(ssq / (2 * half) + EPS)
    gt = mod_ref[0][5:6]
    g = g_ref[...]
    o_ref[0, :, :half] = x1_ref[0, :, :half] + gt[:, :half] * (ffn_lo * inv * g[:, :half])
    o_ref[0, :, half:] = x1_ref[0, :, half:] + gt[:, half:] * (ffn_hi * inv * g[:, half:])


def _ffn_post(x1, rows, gates, mod, g, tok_offset, n_tok):
    bsz, seq, d = x1.shape
    tm = _tile(seq, 256)
    assert tok_offset % tm == 0 and n_tok % tm == 0
    per_b = seq // tm
    row = pl.BlockSpec((1, tm, d), lambda b, i: (b, i, 0))

    def choice_spec(k):
        blk0 = (k * n_tok + tok_offset) // tm
        return pl.BlockSpec((tm, d // 2), lambda b, i: (blk0 + b * per_b + i, 0))

    return pl.pallas_call(
        _ffn_post_kernel,
        grid=(bsz, per_b),
        in_specs=[row] + [choice_spec(k) for k in range(TOP_K)] + [
            pl.BlockSpec((1, tm, LANES), lambda b, i: (b, i, 0)),
            pl.BlockSpec((1, 6, d), lambda b, i: (b, 0, 0)),
            pl.BlockSpec((1, d), lambda b, i: (0, 0)),
        ],
        out_specs=row,
        out_shape=jax.ShapeDtypeStruct((bsz, seq, d), F32),
        compiler_params=_params(("parallel", "parallel")),
        name="ffn_post",
    )(x1, *([rows] * TOP_K), gates, mod, g)


def _rot_half_cols(w, head_dim):
    d = w.shape[0]
    w3 = w.reshape(d, -1, head_dim)
    half = head_dim // 2
    return jnp.concatenate([-w3[..., half:], w3[..., :half]], axis=-1).reshape(d, -1)


def _rope_tables(pos, head_dim):
    half = head_dim // 2
    inv = ROPE_BASE ** (-jnp.arange(half, dtype=F32) / half)
    ang = pos.astype(F32)[:, None] * inv[None, :]
    cos = jnp.cos(ang)
    sin = jnp.sin(ang)
    return jnp.concatenate([cos, cos], axis=1), jnp.concatenate([sin, sin], axis=1)


class _Dims:
    def __init__(self, state_ret, cache_ckv, cache_kpe, lru_w_a, w_uk, w_uv, conv_w, w_router, w_down):
        _, _, self.ret_heads, self.ret_dk, self.ret_dv = state_ret.shape
        self.kv_lora = cache_ckv.shape[-1]
        self.qk_rope = cache_kpe.shape[-1]
        self.mix_a = conv_w.shape[-1]
        self.conv_w = conv_w.shape[1]
        self.mla_heads, self.qk_nope = w_uk.shape[2], w_uk.shape[3]
        self.v_head = w_uv.shape[3]
        self.n_exp = w_router.shape[-1]
        self.mix_b = self.ret_heads * self.ret_dv
        self.ret_qk = self.ret_heads * self.ret_dk


def _layer_weights(l, dm, w_in, conv_w, conv_b, lru_w_a, lru_b_a, lru_w_x, lru_b_x, lru_lam, g_q_norm, w_uq,
                   g_kv_norm, w_uk, w_uv, w_out, w_router, b_router, g_mix_pre, g_mix_post, g_ffn_pre, g_ffn_post):
    d = w_in.shape[1]
    q_lora = g_q_norm.shape[1]
    sizes = (dm.mix_a, dm.mix_a, dm.ret_qk, dm.ret_qk, dm.mix_b, dm.mix_b, q_lora, dm.kv_lora, dm.qk_rope)
    offs = np.concatenate([[0], np.cumsum(sizes)])
    wl = w_in[l]
    seg = [wl[:, offs[i]:offs[i + 1]] for i in range(len(sizes))]
    w_xa, w_ya, w_qr, w_kr, w_vr, w_gr, w_ql, w_kvl, w_kpe = seg
    blk = dm.mix_b
    kv_cols = dm.kv_lora + 2 * dm.qk_rope
    kv_pad = (-kv_cols) % blk
    cols = [w_xa, w_ya, jnp.concatenate([w_qr, w_kr], axis=1), w_vr, w_gr, w_ql,
            jnp.concatenate([w_kvl, w_kpe, _rot_half_cols(w_kpe, dm.qk_rope), jnp.zeros((d, kv_pad), F32)], axis=1),
            jnp.concatenate([_rot_half_cols(w_qr, dm.ret_dk), _rot_half_cols(w_kr, dm.ret_dk)], axis=1)]
    assert all(c.shape[1] == blk for c in cols), [c.shape for c in cols]
    w_ext = jnp.concatenate(cols, axis=1).astype(BF16)

    n_lru = lru_w_a.shape[1]
    eye = jnp.eye(n_lru, dtype=F32)
    block_diag = lambda w: jnp.einsum("nde,nm->ndme", w, eye).reshape(dm.mix_a, dm.mix_a).astype(BF16)

    qk_dim = dm.qk_nope + dm.qk_rope
    wq3 = w_uq[l].reshape(q_lora, dm.mla_heads, qk_dim)
    pe = wq3[..., dm.qk_nope:]
    half = dm.qk_rope // 2
    pe_rot = jnp.concatenate([-pe[..., half:], pe[..., :half]], axis=-1)
    w_uq_ext = jnp.concatenate([wq3, pe_rot], axis=-1).transpose(1, 0, 2).astype(BF16)

    wo = w_out[l].astype(BF16)
    n_exp = dm.n_exp
    wr = jnp.concatenate([w_router[l], jnp.zeros((d, LANES - n_exp), F32)], axis=1)
    wr_hi = wr.astype(BF16)
    wr_lo = (wr - wr_hi.astype(F32)).astype(BF16)
    br = jnp.concatenate([b_router[l], jnp.full((LANES - n_exp,), NEG_BIG, F32)]).reshape(1, LANES)
    row = lambda v: v[l].reshape(1, -1)
    return dict(
        w_ext=w_ext, conv_w=conv_w[l], conv_b=row(conv_b),
        wa_bd=block_diag(lru_w_a[l]), wx_bd=block_diag(lru_w_x[l]),
        ba=row(lru_b_a), bx=row(lru_b_x), lam=row(lru_lam),
        g_q=row(g_q_norm), w_uq_ext=w_uq_ext, g_kv=row(g_kv_norm),
        w_uk=w_uk[l].transpose(1, 0, 2).astype(BF16), w_uv=w_uv[l].transpose(1, 0, 2).astype(BF16),
        wo_a=wo[:dm.mix_a], wo_b=wo[dm.mix_a:dm.mix_a + dm.mix_b], wo_c=wo[dm.mix_a + dm.mix_b:],
        wr_hi=wr_hi, wr_lo=wr_lo, br=br,
        g_mix_pre=row(g_mix_pre), g_mix_post=row(g_mix_post), g_ffn_pre=row(g_ffn_pre), g_ffn_post=row(g_ffn_post),
    )


def _mixer(x, mod, wl, dm, conv_buf, h0, s0, past_ckv, past_kpe):
    bsz, seq, d = x.shape
    past_len = 0 if past_ckv is None else past_ckv.shape[1]
    pos = past_len + jnp.arange(seq, dtype=jnp.int32)
    u = _in_proj(x, mod, wl["g_mix_pre"], wl["w_ext"])

    ctx_rows = 8
    n_ctx = dm.conv_w - 1
    conv_ctx = jnp.concatenate([jnp.zeros((bsz, ctx_rows - n_ctx, dm.mix_a), F32), conv_buf], axis=1)
    a_out, conv_new, h_new = _rg_lru(u, conv_ctx, h0.reshape(bsz, 1, dm.mix_a), wl["conv_w"], wl["conv_b"],
                                     wl["wa_bd"], wl["wx_bd"], wl["ba"], wl["bx"], wl["lam"])

    cos_k, sin_k = _rope_tables(pos, dm.ret_dk)
    k_scale = float(dm.ret_dk) ** -0.5
    scale_row = jnp.concatenate([jnp.ones((dm.ret_qk,), F32), jnp.full((dm.ret_qk,), k_scale, F32)])[None, :]
    cos_t = jnp.tile(cos_k, (1, 2 * dm.ret_heads)) * scale_row
    sin_t = jnp.tile(sin_k, (1, 2 * dm.ret_heads)) * scale_row
    b_out, s_new = _retention(u, s0, cos_t, sin_t, blocks=(2, 3, 4, 7))

    cos_r, sin_r = _rope_tables(pos, dm.qk_rope)
    lane_pad = lambda t, left, right, fill: jnp.concatenate(
        [jnp.full((seq, left), fill, F32), t, jnp.zeros((seq, right), F32)], axis=1)
    ckv_new, kpe_new = _latent(u, wl["g_kv"], lane_pad(cos_r, 0, LANES - dm.qk_rope, 0.0),
                               lane_pad(sin_r, 0, LANES - dm.qk_rope, 0.0), kv_blk=6, blk_w=dm.mix_b, rope_w=dm.qk_rope)
    q = _q_proj(u, wl["g_q"], wl["w_uq_ext"], lane_pad(cos_r, dm.qk_nope, dm.qk_rope, 1.0),
                lane_pad(sin_r, dm.qk_nope, dm.qk_rope, 0.0), ql_blk=5, rope_w=dm.qk_rope)
    if past_ckv is None:
        ckv_all, kpe_all = ckv_new, kpe_new
    else:
        past_kpe_pad = jnp.concatenate([past_kpe, jnp.zeros(past_kpe.shape[:2] + (LANES - dm.qk_rope,), F32)], axis=2)
        ckv_all = jnp.concatenate([past_ckv, ckv_new], axis=1)
        kpe_all = jnp.concatenate([past_kpe_pad, kpe_new], axis=1)
    k, v = _kv_proj(ckv_all, kpe_all, wl["w_uk"], wl["w_uv"])
    c_out = _attention(q, k, v, past_len, dm.qk_nope + dm.qk_rope)

    x1, h2, gates, idx = _out_proj(a_out, b_out, c_out, x, mod, wl["g_mix_post"], wl["g_ffn_pre"],
                                   wl["wo_a"], wl["wo_b"], wl["wo_c"], wl["wr_hi"], wl["wr_lo"], wl["br"])
    states = (ckv_new, kpe_new[:, :, :dm.qk_rope], conv_new[:, ctx_rows - n_ctx:], h_new.reshape(bsz, dm.mix_a), s_new)
    return x1, h2, gates, idx[:, :, :TOP_K], states


def kernel(x_prompt, x_sample, c_prompt, c_sample, cache_ckv, cache_kpe, state_conv, state_lru, state_ret, w_ada, b_ada, g_mix_pre, g_mix_post, g_ffn_pre, g_ffn_post, w_in, conv_w, conv_b, lru_w_a, lru_b_a, lru_w_x, lru_b_x, lru_lam, g_q_norm, w_uq, g_kv_norm, w_uk, w_uv, w_out, w_router, b_router, w_gate_up, b_gate_up, w_down, b_down):
    depth = w_in.shape[0]
    dm = _Dims(state_ret, cache_ckv, cache_kpe, lru_w_a, w_uk, w_uv, conv_w, w_router, w_down)
    bp, lp, d = x_prompt.shape
    bs, ls, _ = x_sample.shape
    n_tok = bp * lp + bs * ls
    moe_tm = min(1024, max(16, 1 << int(np.log2((n_tok * TOP_K) // dm.n_exp))))
    n_blocks = -(-(n_tok * TOP_K) // moe_tm) + dm.n_exp

    xp, xs = x_prompt, x_sample
    p_states, s_states = [], []
    for l in range(depth):
        wl = _layer_weights(l, dm, w_in, conv_w, conv_b, lru_w_a, lru_b_a, lru_w_x, lru_b_x, lru_lam, g_q_norm, w_uq,
                            g_kv_norm, w_uk, w_uv, w_out, w_router, b_router, g_mix_pre, g_mix_post, g_ffn_pre, g_ffn_post)
        mod_p = _ada_mod(c_prompt, w_ada, b_ada, l).reshape(bp, 6, d)
        mod_s = _ada_mod(c_sample, w_ada, b_ada, l).reshape(bs, 6, d)
        zeros = lambda *s: jnp.zeros(s, F32)
        x1p, h2p, gp, ip, st_p = _mixer(
            xp, mod_p, wl, dm, zeros(bp, dm.conv_w - 1, dm.mix_a), zeros(bp, dm.mix_a),
            zeros(bp, dm.ret_heads, dm.ret_dk, dm.ret_dv), None, None)
        x1s, h2s, gs, is_, st_s = _mixer(
            xs, mod_s, wl, dm, state_conv[l], state_lru[l], state_ret[l], cache_ckv[l], cache_kpe[l])
        p_states.append(st_p)
        s_states.append(st_s)

        h2 = jnp.concatenate([h2p.reshape(-1, d // 2), h2s.reshape(-1, d // 2)], axis=0)
        top_idx = jnp.concatenate([ip.reshape(-1, TOP_K), is_.reshape(-1, TOP_K)], axis=0)
        src_rows, dst_rows, block_e, n_used, n_out_rows = _route(top_idx, dm.n_exp, moe_tm, n_blocks)
        rows = _moe_ffn(h2, src_rows, dst_rows, block_e, n_used, _layer_bf16(w_gate_up, l), b_gate_up,
                        _layer_bf16(w_down, l), b_down, l, n_out_rows)
        xp = _ffn_post(x1p, rows, gp, mod_p, wl["g_ffn_post"], 0, n_tok)
        xs = _ffn_post(x1s, rows, gs, mod_s, wl["g_ffn_post"], bp * lp, n_tok)

    stack = lambda sts: tuple(jnp.stack(t, axis=0) for t in zip(*sts))
    p_ckv, p_kpe, p_conv, p_lru, p_ret = stack(p_states)
    s_ckv, s_kpe, s_conv, s_lru, s_ret = stack(s_states)
    return (xp, xs, p_ckv, p_kpe, p_conv, p_lru, p_ret, s_ckv, s_kpe, s_conv, s_lru, s_ret)
```

```python
import functools

import numpy as np
import jax
import jax.numpy as jnp
from jax import lax
from jax.experimental import pallas as pl
from jax.experimental.pallas import tpu as pltpu

CHUNK = 64
EPS = 1e-6
ROPE_BASE = 10000.0
RG_C = 8.0
TOP_K = 4
SWIGLU_LIMIT = 7.0
SWIGLU_ALPHA = 1.702
NEG_BIG = -1e30
GELU_C = float(np.sqrt(2.0 / np.pi))

LANES = 128
VMEM_LIMIT_BYTES = 56 * 1024 * 1024

F32 = jnp.float32
BF16 = jnp.bfloat16


def _params(semantics):
    return pltpu.CompilerParams(dimension_semantics=semantics, vmem_limit_bytes=VMEM_LIMIT_BYTES)


def _tile(n, pref):
    if n <= pref:
        return n
    t = pref
    while n % t:
        t //= 2
    return t


def _rms(x, g):
    return x * lax.rsqrt(jnp.mean(x * x, axis=-1, keepdims=True) + EPS) * g


def _dot(a, b):
    return jnp.dot(a, b, preferred_element_type=F32)


def _dot_nt(a, b):
    return lax.dot_general(a, b, (((1,), (1,)), ((), ())), preferred_element_type=F32)


def _dot_tn(a, b):
    return lax.dot_general(a, b, (((0,), (0,)), ((), ())), preferred_element_type=F32)


def _sigmoid(x):
    return 1.0 / (1.0 + jnp.exp(-x))


def _pack_bf16_halves(x):
    n = x.shape[1] // 2
    bits = lax.bitcast_convert_type(x.astype(BF16).astype(F32), jnp.uint32)
    return (bits[:, :n] >> 16) | (bits[:, n:] & jnp.uint32(0xFFFF0000))


def _unpack_bf16_halves(w):
    lo = lax.bitcast_convert_type(w << 16, F32)
    hi = lax.bitcast_convert_type(w & jnp.uint32(0xFFFF0000), F32)
    return lo, hi


def _ada_kernel(c_ref, w_ref, b_ref, o_ref):
    c = c_ref[...]
    s = (c * _sigmoid(c)).astype(BF16)
    o_ref[...] = _dot(s, w_ref[...].astype(BF16)) + b_ref[...]


def _ada_mod(c, w_ada, b_ada, layer):
    bsz, d = c.shape
    n = w_ada.shape[-1]
    tn = _tile(n, 1024)
    return pl.pallas_call(
        _ada_kernel,
        grid=(n // tn,),
        in_specs=[
            pl.BlockSpec((bsz, d), lambda j: (0, 0)),
            pl.BlockSpec((None, d, tn), lambda j: (layer, 0, j)),
            pl.BlockSpec((None, 1, tn), lambda j: (layer, 0, j)),
        ],
        out_specs=pl.BlockSpec((bsz, tn), lambda j: (0, j)),
        out_shape=jax.ShapeDtypeStruct((bsz, n), F32),
        compiler_params=_params(("arbitrary",)),
        name="ada_mod",
    )(c, w_ada, b_ada.reshape(b_ada.shape[0], 1, n))


def _in_proj_kernel(x_ref, mod_ref, g_ref, w_ref, u_ref):
    mod = mod_ref[0]
    h = _rms(x_ref[0], g_ref[...]) * (1.0 + mod[1:2]) + mod[0:1]
    u_ref[0] = _dot(h.astype(BF16), w_ref[...])


def _in_proj(x, mod, g, w_ext):
    bsz, seq, d = x.shape
    n = w_ext.shape[1]
    tm = _tile(seq, 256)
    return pl.pallas_call(
        _in_proj_kernel,
        grid=(bsz, seq // tm),
        in_specs=[
            pl.BlockSpec((1, tm, d), lambda b, i: (b, i, 0)),
            pl.BlockSpec((1, 6, d), lambda b, i: (b, 0, 0)),
            pl.BlockSpec((1, d), lambda b, i: (0, 0)),
            pl.BlockSpec((d, n), lambda b, i: (0, 0)),
        ],
        out_specs=pl.BlockSpec((1, tm, n), lambda b, i: (b, i, 0)),
        out_shape=jax.ShapeDtypeStruct((bsz, seq, n), F32),
        compiler_params=_params(("parallel", "parallel")),
        name="in_proj",
    )(x, mod, g, w_ext)


def _lru_kernel(xa_ref, ya_ref, cbuf_ref, h0_ref, cw_ref, cb_ref, wa_ref, wx_ref, ba_ref, bx_ref,
                lam_ref, out_ref, cnew_ref, hlast_ref, xbuf, hcar):
    tl = xa_ref.shape[1]
    width = xa_ref.shape[2]
    ctx = xbuf.shape[0] - tl

    @pl.when(pl.program_id(1) == 0)
    def _():
        xbuf[0:ctx, :] = cbuf_ref[0]
        hcar[...] = h0_ref[0]

    xa = xa_ref[0]
    xbuf[ctx:ctx + tl, :] = xa
    cw = cw_ref[...]
    n_tap = cw.shape[0]
    xc = cb_ref[...] + cw[n_tap - 1:n_tap] * xa
    for k in range(n_tap - 1):
        off = ctx - (n_tap - 1) + k
        xc = xc + cw[k:k + 1] * xbuf[off:off + tl, :]
    new_ctx = xbuf[tl:tl + ctx, :]
    xbuf[0:ctx, :] = new_ctx
    cnew_ref[0] = new_ctx

    xcb = xc.astype(BF16)
    r = _sigmoid(_dot(xcb, wa_ref[...]) + ba_ref[...])
    gi = _sigmoid(_dot(xcb, wx_ref[...]) + bx_ref[...])
    z = -lam_ref[...]
    softplus = jnp.maximum(z, 0.0) + jnp.log1p(jnp.exp(-jnp.abs(z)))
    log_a = (-RG_C) * r * softplus
    a = jnp.exp(log_a)
    b = jnp.sqrt(1.0 - jnp.exp(2.0 * log_a)) * gi * xc

    row = lax.broadcasted_iota(jnp.int32, (tl, width), 0)
    s = 1
    while s < tl:
        a_sh = pltpu.roll(a, s, 0)
        b_sh = pltpu.roll(b, s, 0)
        valid = row >= s
        b = jnp.where(valid, a * b_sh + b, b)
        a = jnp.where(valid, a * a_sh, a)
        s *= 2
    h = a * hcar[...] + b
    h_last = h[tl - 1:tl, :]
    hcar[...] = h_last
    hlast_ref[0] = h_last

    ya = ya_ref[0]
    gelu = 0.5 * ya * (1.0 + jnp.tanh(GELU_C * (ya + 0.044715 * (ya * ya * ya))))
    out_ref[0] = (h * gelu).astype(out_ref.dtype)


def _rg_lru(u, conv_ctx, h0, cw, cb, wa_bd, wx_bd, ba, bx, lam):
    bsz, seq, _ = u.shape
    width = cw.shape[1]
    tl = _tile(seq, 512)
    ctx = conv_ctx.shape[1]
    xa_blk = 0
    ya_blk = 1
    vec = lambda: pl.BlockSpec((1, width), lambda b, t: (0, 0))
    return pl.pallas_call(
        _lru_kernel,
        grid=(bsz, seq // tl),
        in_specs=[
            pl.BlockSpec((1, tl, width), lambda b, t: (b, t, xa_blk)),
            pl.BlockSpec((1, tl, width), lambda b, t: (b, t, ya_blk)),
            pl.BlockSpec((1, ctx, width), lambda b, t: (b, 0, 0)),
            pl.BlockSpec((1, 1, width), lambda b, t: (b, 0, 0)),
            pl.BlockSpec(cw.shape, lambda b, t: (0, 0)),
            vec(),
            pl.BlockSpec((width, width), lambda b, t: (0, 0)),
            pl.BlockSpec((width, width), lambda b, t: (0, 0)),
            vec(), vec(), vec(),
        ],
        out_specs=[
            pl.BlockSpec((1, tl, width), lambda b, t: (b, t, 0)),
            pl.BlockSpec((1, ctx, width), lambda b, t: (b, 0, 0)),
            pl.BlockSpec((1, 1, width), lambda b, t: (b, 0, 0)),
        ],
        out_shape=[
            jax.ShapeDtypeStruct((bsz, seq, width), BF16),
            jax.ShapeDtypeStruct((bsz, ctx, width), F32),
            jax.ShapeDtypeStruct((bsz, 1, width), F32),
        ],
        scratch_shapes=[pltpu.VMEM((tl + ctx, width), F32), pltpu.VMEM((1, width), F32)],
        compiler_params=_params(("parallel", "arbitrary")),
        name="rg_lru",
    )(u, u, conv_ctx, h0, cw, cb, wa_bd, wx_bd, ba, bx, lam)


def _ret_kernel(qk_ref, rot_ref, v_ref, g_ref, cos_ref, sin_ref, dmask_ref, qdec_ref, kdec_ref,
                s0_ref, out_ref, slast_ref, s_sc, *, gchunk):
    n_heads, dk, dv = s_sc.shape
    hdk = n_heads * dk

    @pl.when(pl.program_id(1) == 0)
    def _():
        s_sc[...] = s0_ref[0]

    qkr = qk_ref[0] * cos_ref[...] + rot_ref[0] * sin_ref[...]
    v = v_ref[0]
    vdec = v * kdec_ref[...]
    gate = g_ref[0]
    qdec = qdec_ref[...]
    for h in range(n_heads):
        q = qkr[:, h * dk:(h + 1) * dk].astype(BF16)
        k = qkr[:, hdk + h * dk:hdk + (h + 1) * dk].astype(BF16)
        vh = v[:, h * dv:(h + 1) * dv].astype(BF16)
        vd = vdec[:, h * dv:(h + 1) * dv].astype(BF16)
        s_prev = s_sc[h]
        scores = _dot_nt(q, k) * dmask_ref[h]
        o = _dot(scores.astype(BF16), vh) + qdec[:, h * dv:(h + 1) * dv] * _dot(q, s_prev.astype(BF16))
        s_sc[h] = s_prev * gchunk[h] + _dot_tn(k, vd)
        o = o * lax.rsqrt(jnp.mean(o * o, axis=-1, keepdims=True) + EPS)
        gh = gate[:, h * dv:(h + 1) * dv]
        out_ref[0, :, h * dv:(h + 1) * dv] = (o * (gh * _sigmoid(gh))).astype(out_ref.dtype)
    slast_ref[0] = s_sc[...]


def _retention(u, s0, cos_t, sin_t, blocks):
    bsz, seq, _ = u.shape
    _, n_heads, dk, dv = s0.shape
    width = n_heads * dv
    tc = _tile(seq, 256)
    log_g = np.log1p(-np.exp2(-5.0 - np.arange(n_heads, dtype=np.float64)))
    idx = np.arange(tc, dtype=np.float64)
    diff = idx[:, None] - idx[None, :]
    dmask = np.where(diff[None] >= 0, np.exp(np.maximum(diff, 0.0)[None] * log_g[:, None, None]), 0.0)
    qdec = np.repeat(np.exp((idx + 1.0)[:, None] * log_g[None, :]), dv, axis=1)
    kdec = np.repeat(np.exp((tc - 1.0 - idx)[:, None] * log_g[None, :]), dv, axis=1)
    gchunk = tuple(float(g) for g in np.exp(tc * log_g))
    qk_blk, v_blk, g_blk, rot_blk = blocks
    u_spec = lambda blk: pl.BlockSpec((1, tc, width), lambda b, n: (b, n, blk))
    full2 = lambda a: pl.BlockSpec(a.shape, lambda b, n: (0, 0))
    return pl.pallas_call(
        functools.partial(_ret_kernel, gchunk=gchunk),
        grid=(bsz, seq // tc),
        in_specs=[
            u_spec(qk_blk), u_spec(rot_blk), u_spec(v_blk), u_spec(g_blk),
            pl.BlockSpec((tc, width), lambda b, n: (n, 0)),
            pl.BlockSpec((tc, width), lambda b, n: (n, 0)),
            pl.BlockSpec(dmask.shape, lambda b, n: (0, 0, 0)),
            full2(qdec), full2(kdec),
            pl.BlockSpec((1, n_heads, dk, dv), lambda b, n: (b, 0, 0, 0)),
        ],
        out_specs=[
            pl.BlockSpec((1, tc, width), lambda b, n: (b, n, 0)),
            pl.BlockSpec((1, n_heads, dk, dv), lambda b, n: (b, 0, 0, 0)),
        ],
        out_shape=[
            jax.ShapeDtypeStruct((bsz, seq, width), BF16),
            jax.ShapeDtypeStruct((bsz, n_heads, dk, dv), F32),
        ],
        scratch_shapes=[pltpu.VMEM((n_heads, dk, dv), F32)],
        compiler_params=_params(("parallel", "arbitrary")),
        name="retention",
    )(u, u, u, u, cos_t, sin_t, jnp.asarray(dmask, F32), jnp.asarray(qdec, F32),
      jnp.asarray(kdec, F32), s0)


def _latent_kernel(kv_ref, g_ref, cos_ref, sin_ref, ckv_ref, kpe_ref, *, rope_w):
    lora = ckv_ref.shape[2]
    blk = kv_ref[0]
    ckv_ref[0] = _rms(blk[:, :lora], g_ref[...])
    kp = blk[:, lora:lora + LANES]
    rolled = pltpu.roll(kp, LANES - rope_w, 1)
    kpe_ref[0] = kp * cos_ref[...] + rolled * sin_ref[...]


def _latent(u, g_kv, cos_p, sin_p, kv_blk, blk_w, rope_w):
    bsz, seq, _ = u.shape
    lora = g_kv.shape[1]
    tm = _tile(seq, 512)
    return pl.pallas_call(
        functools.partial(_latent_kernel, rope_w=rope_w),
        grid=(bsz, seq // tm),
        in_specs=[
            pl.BlockSpec((1, tm, blk_w), lambda b, i: (b, i, kv_blk)),
            pl.BlockSpec((1, lora), lambda b, i: (0, 0)),
            pl.BlockSpec((tm, LANES), lambda b, i: (i, 0)),
            pl.BlockSpec((tm, LANES), lambda b, i: (i, 0)),
        ],
        out_specs=[
            pl.BlockSpec((1, tm, lora), lambda b, i: (b, i, 0)),
            pl.BlockSpec((1, tm, LANES), lambda b, i: (b, i, 0)),
        ],
        out_shape=[
            jax.ShapeDtypeStruct((bsz, seq, lora), F32),
            jax.ShapeDtypeStruct((bsz, seq, LANES), F32),
        ],
        compiler_params=_params(("parallel", "parallel")),
        name="latent_norm_rope",
    )(u, g_kv, cos_p, sin_p)


def _q_proj_kernel(ql_ref, g_ref, w_ref, cos_ref, sin_ref, q_ref, *, rope_w):
    cq = _rms(ql_ref[0], g_ref[...]).astype(BF16)
    cos = cos_ref[...]
    sin = sin_ref[...]
    for h in range(w_ref.shape[0]):
        r = _dot(cq, w_ref[h])
        rolled = pltpu.roll(r, r.shape[1] - rope_w, 1)
        q_ref[0, h] = (r * cos + rolled * sin).astype(q_ref.dtype)


def _q_proj(u, g_q, w_uq_ext, cos_q, sin_q, ql_blk, rope_w):
    bsz, seq, _ = u.shape
    n_heads, q_lora, width = w_uq_ext.shape
    tm = _tile(seq, 512)
    return pl.pallas_call(
        functools.partial(_q_proj_kernel, rope_w=rope_w),
        grid=(bsz, seq // tm),
        in_specs=[
            pl.BlockSpec((1, tm, q_lora), lambda b, i: (b, i, ql_blk)),
            pl.BlockSpec((1, q_lora), lambda b, i: (0, 0)),
            pl.BlockSpec((n_heads, q_lora, width), lambda b, i: (0, 0, 0)),
            pl.BlockSpec((tm, width), lambda b, i: (i, 0)),
            pl.BlockSpec((tm, width), lambda b, i: (i, 0)),
        ],
        out_specs=pl.BlockSpec((1, n_heads, tm, width), lambda b, i: (b, 0, i, 0)),
        out_shape=jax.ShapeDtypeStruct((bsz, n_heads, seq, width), BF16),
        compiler_params=_params(("parallel", "parallel")),
        name="mla_q_proj",
    )(u, g_q, w_uq_ext, cos_q, sin_q)


def _kv_proj_kernel(ckv_ref, kpe_ref, wk_ref, wv_ref, k_ref, v_ref):
    ckv = ckv_ref[0].astype(BF16)
    kpe = kpe_ref[0].astype(k_ref.dtype)
    nope = wk_ref.shape[2]
    for h in range(wk_ref.shape[0]):
        k_ref[0, h, :, :nope] = _dot(ckv, wk_ref[h]).astype(k_ref.dtype)
        k_ref[0, h, :, nope:] = kpe
        v_ref[0, h] = _dot(ckv, wv_ref[h]).astype(v_ref.dtype)


def _kv_proj(ckv, kpe, w_uk, w_uv):
    bsz, klen, lora = ckv.shape
    n_heads, _, nope = w_uk.shape
    vdim = w_uv.shape[2]
    tm = _tile(klen, 512) if klen % 512 == 0 else klen
    return pl.pallas_call(
        _kv_proj_kernel,
        grid=(bsz, klen // tm),
        in_specs=[
            pl.BlockSpec((1, tm, lora), lambda b, i: (b, i, 0)),
            pl.BlockSpec((1, tm, LANES), lambda b, i: (b, i, 0)),
            pl.BlockSpec((n_heads, lora, nope), lambda b, i: (0, 0, 0)),
            pl.BlockSpec((n_heads, lora, vdim), lambda b, i: (0, 0, 0)),
        ],
        out_specs=[
            pl.BlockSpec((1, n_heads, tm, nope + LANES), lambda b, i: (b, 0, i, 0)),
            pl.BlockSpec((1, n_heads, tm, vdim), lambda b, i: (b, 0, i, 0)),
        ],
        out_shape=[
            jax.ShapeDtypeStruct((bsz, n_heads, klen, nope + LANES), BF16),
            jax.ShapeDtypeStruct((bsz, n_heads, klen, vdim), BF16),
        ],
        compiler_params=_params(("parallel", "parallel")),
        name="mla_kv_proj",
    )(ckv, kpe, w_uk, w_uv)


def _attn_kernel(q_ref, k_ref, v_ref, o_ref, m_sc, l_sc, acc_sc, s_sc, *, q_pos0, scale, tk):
    qi = pl.program_id(2)
    tq = q_ref.shape[2]
    nk = k_ref.shape[2] // tk
    q = q_ref[0, 0]
    m_sc[...] = jnp.full(m_sc.shape, NEG_BIG, F32)
    l_sc[...] = jnp.zeros(l_sc.shape, F32)
    acc_sc[...] = jnp.zeros(acc_sc.shape, F32)

    last_q_chunk = (q_pos0 + qi * tq + tq - 1) // CHUNK
    n_seen = jnp.minimum((last_q_chunk * CHUNK + CHUNK - 1) // tk + 1, nk)
    q_chunk = (q_pos0 + qi * tq + lax.broadcasted_iota(jnp.int32, (tq, 1), 0)) // CHUNK
    k_chunk0 = lax.broadcasted_iota(jnp.int32, (1, tk), 1) // CHUNK

    def scores(j):
        start = pl.multiple_of(j * tk, tk)
        return _dot_nt(q, k_ref[0, 0, pl.ds(start, tk), :])

    s_sc[0] = scores(0)

    def step(j, carry):
        slot = j % 2
        s = s_sc[slot]
        s_sc[1 - slot] = scores(jnp.minimum(j + 1, n_seen - 1))
        start = pl.multiple_of(j * tk, tk)
        s = jnp.where(k_chunk0 + start // CHUNK <= q_chunk, s * scale, NEG_BIG)
        m_prev = m_sc[...]
        m_new = jnp.maximum(m_prev, jnp.max(s, axis=-1, keepdims=True))
        alpha = jnp.exp(m_prev - m_new)
        p = jnp.exp(s - m_new)
        l_sc[...] = alpha * l_sc[...] + jnp.sum(p, axis=-1, keepdims=True)
        acc_sc[...] = alpha * acc_sc[...] + _dot(p.astype(BF16), v_ref[0, 0, pl.ds(start, tk), :])
        m_sc[...] = m_new
        return carry

    lax.fori_loop(0, n_seen, step, 0)
    o_ref[0] = (acc_sc[...] / l_sc[...]).astype(o_ref.dtype)


def _attention(q, k, v, q_pos0, qk_dim):
    bsz, n_heads, seq, width = q.shape
    klen = k.shape[2]
    vdim = v.shape[3]
    tq = _tile(seq, 512)
    tk = _tile(klen, 512) if klen % 512 == 0 else klen
    assert tk == klen or tk % CHUNK == 0
    return pl.pallas_call(
        functools.partial(_attn_kernel, q_pos0=q_pos0, scale=float(qk_dim) ** -0.5, tk=tk),
        grid=(bsz, n_heads, seq // tq),
        in_specs=[
            pl.BlockSpec((1, 1, tq, width), lambda b, h, qi: (b, h, qi, 0)),
            pl.BlockSpec((1, 1, klen, width), lambda b, h, qi: (b, h, 0, 0)),
            pl.BlockSpec((1, 1, klen, vdim), lambda b, h, qi: (b, h, 0, 0)),
        ],
        out_specs=pl.BlockSpec((1, tq, vdim), lambda b, h, qi: (b, qi, h)),
        out_shape=jax.ShapeDtypeStruct((bsz, seq, n_heads * vdim), BF16),
        scratch_shapes=[pltpu.VMEM((tq, 1), F32), pltpu.VMEM((tq, 1), F32), pltpu.VMEM((tq, vdim), F32),
                        pltpu.VMEM((2, tq, tk), F32)],
        compiler_params=_params(("parallel", "parallel", "arbitrary")),
        name="mla_attention",
    )(q, k, v)


def _out_proj_kernel(a_ref, b_ref, c_ref, x_ref, mod_ref, gpost_ref, gpre_ref, wa_ref, wb_ref, wc_ref,
                     wrh_ref, wrl_ref, br_ref, x1_ref, h2_ref, gate_ref, idx_ref):
    mix = _dot(a_ref[0], wa_ref[...]) + _dot(b_ref[0], wb_ref[...]) + _dot(c_ref[0], wc_ref[...])
    mod = mod_ref[0]
    x1 = x_ref[0] + mod[2:3] * _rms(mix, gpost_ref[...])
    x1_ref[0] = x1
    h2 = _rms(x1, gpre_ref[...]) * (1.0 + mod[4:5]) + mod[3:4]
    h2_ref[0] = _pack_bf16_halves(h2)
    hi = h2.astype(BF16)
    lo = (h2 - hi.astype(F32)).astype(BF16)
    logits = _dot(hi, wrh_ref[...]) + (_dot(hi, wrl_ref[...]) + _dot(lo, wrh_ref[...])) + br_ref[...]
    lane = lax.broadcasted_iota(jnp.int32, logits.shape, 1)
    lane_f = lane.astype(F32)
    gates = jnp.zeros(logits.shape, F32)
    idxs = jnp.zeros(logits.shape, F32)
    top0 = None
    denom = None
    for k in range(TOP_K):
        m = jnp.max(logits, axis=-1, keepdims=True)
        sel = jnp.min(jnp.where(logits == m, lane_f, float(LANES)), axis=-1, keepdims=True)
        if k == 0:
            top0 = m
            e = jnp.ones_like(m)
            denom = e
        else:
            e = jnp.exp(m - top0)
            denom = denom + e
        gates = jnp.where(lane == k, e, gates)
        idxs = jnp.where(lane == k, sel, idxs)
        logits = jnp.where(lane_f == sel, NEG_BIG * 2.0, logits)
    gate_ref[0] = gates / denom
    idx_ref[0] = idxs.astype(jnp.int32)


def _out_proj(a_out, b_out, c_out, x, mod, g_post, g_pre, wa, wb, wc, wr_hi, wr_lo, br):
    bsz, seq, d = x.shape
    tm = _tile(seq, 256)
    row = lambda w: pl.BlockSpec((1, tm, w), lambda b, i: (b, i, 0))
    const = lambda a: pl.BlockSpec(a.shape, lambda b, i: (0, 0))
    return pl.pallas_call(
        _out_proj_kernel,
        grid=(bsz, seq // tm),
        in_specs=[
            row(a_out.shape[2]), row(b_out.shape[2]), row(c_out.shape[2]), row(d),
            pl.BlockSpec((1, 6, d), lambda b, i: (b, 0, 0)),
            const(g_post), const(g_pre), const(wa), const(wb), const(wc),
            const(wr_hi), const(wr_lo), const(br),
        ],
        out_specs=[row(d), row(d // 2), row(LANES), row(LANES)],
        out_shape=[
            jax.ShapeDtypeStruct((bsz, seq, d), F32),
            jax.ShapeDtypeStruct((bsz, seq, d // 2), jnp.uint32),
            jax.ShapeDtypeStruct((bsz, seq, LANES), F32),
            jax.ShapeDtypeStruct((bsz, seq, LANES), jnp.int32),
        ],
        compiler_params=_params(("parallel", "parallel")),
        name="out_proj_router",
    )(a_out, b_out, c_out, x, mod, g_post, g_pre, wa, wb, wc, wr_hi, wr_lo, br)


def _moe_kernel(be_ref, nu_ref, src_ref, dst_ref, h_hbm, wg_ref, wu_ref, bg_ref, bu_ref, wd_ref, bd_ref,
                rows_hbm, xbuf, xb, acc, pbuf, gsem, ssem):
    g = pl.program_id(0)
    f = pl.program_id(1)
    nf = pl.num_programs(1)
    n_used = nu_ref[0]
    tm, half = xbuf.shape
    chunk = tm // nf
    live = jnp.logical_and(g >= 1, g <= n_used)

    def gather_copy(r):
        return pltpu.make_async_copy(h_hbm.at[pl.ds(src_ref[0, r], 1)], xbuf.at[pl.ds(r, 1)], gsem)

    def scatter_copy(r):
        return pltpu.make_async_copy(pbuf.at[pl.ds(r, 1)], rows_hbm.at[pl.ds(dst_ref[0, r], 1)], ssem)

    def wait_gather():
        pltpu.make_async_copy(h_hbm.at[pl.ds(0, tm)], xbuf, gsem).wait()

    def wait_scatter():
        pltpu.make_async_copy(pbuf, rows_hbm.at[pl.ds(0, tm)], ssem).wait()

    @pl.when(jnp.logical_and(g == 0, f == 0))
    def _():
        pbuf[...] = jnp.zeros(pbuf.shape, pbuf.dtype)
        lax.fori_loop(0, tm, lambda r, c: (gather_copy(r).start(), c)[1], 0)

    @pl.when(jnp.logical_and(live, f == 0))
    def _():
        wait_gather()
        lo, hi = _unpack_bf16_halves(xbuf[...])
        xb[:, :half] = lo.astype(BF16)
        xb[:, half:] = hi.astype(BF16)
        acc[...] = jnp.broadcast_to(bd_ref[...], acc.shape)

    @pl.when(live)
    def _():
        row0 = f * chunk
        for i in range(chunk):
            gather_copy(row0 + i).start()
            scatter_copy(row0 + i).start()
        x = xb[...]
        gate = _dot(x, wg_ref[...].astype(BF16)) + bg_ref[...]
        up = _dot(x, wu_ref[...].astype(BF16)) + bu_ref[...]
        x_glu = jnp.minimum(gate, SWIGLU_LIMIT)
        x_lin = jnp.clip(up, -SWIGLU_LIMIT, SWIGLU_LIMIT)
        act = x_glu * _sigmoid(SWIGLU_ALPHA * x_glu) * (x_lin + 1.0)
        acc[...] += _dot(act.astype(BF16), wd_ref[...].astype(BF16))

    @pl.when(jnp.logical_and(live, f == nf - 1))
    def _():
        wait_scatter()
        pbuf[...] = _pack_bf16_halves(acc[...])

    @pl.when(jnp.logical_and(g == n_used + 1, f == 0))
    def _():
        wait_gather()
        lax.fori_loop(0, tm, lambda r, c: (scatter_copy(r).start(), c)[1], 0)
        wait_scatter()


def _moe_ffn(h_all, src_rows, dst_rows, block_e, n_used, w_gu, b_gu, w_dn, b_dn, layer, n_out_rows):
    n_grid, tm = src_rows.shape
    half = h_all.shape[1]
    d = 2 * half
    d_exp = w_dn.shape[2]
    tf = _tile(d_exp, 256)
    nf = d_exp // tf
    assert tm % nf == 0 and h_all.shape[0] >= tm and n_out_rows >= tm
    n_exp = w_gu.shape[1]
    b_gu4 = b_gu.reshape(b_gu.shape[0], n_exp, 1, 2 * d_exp)
    b_dn4 = b_dn.reshape(b_dn.shape[0], n_exp, 1, d)

    def f_idx(g, f, nu):
        return jnp.where(jnp.logical_and(g >= 1, g <= nu[0]), f, nf - 1)

    def e_idx(g, be, nu):
        return be[jnp.clip(g - 1, 0, nu[0] - 1)]

    grid_spec = pltpu.PrefetchScalarGridSpec(
        num_scalar_prefetch=2,
        grid=(n_grid, nf),
        in_specs=[
            pl.BlockSpec((None, 1, tm), lambda g, f, be, nu: (g, 0, 0), memory_space=pltpu.SMEM),
            pl.BlockSpec((None, 1, tm), lambda g, f, be, nu: (jnp.maximum(g - 2, 0), 0, 0), memory_space=pltpu.SMEM),
            pl.BlockSpec(memory_space=pl.ANY),
            pl.BlockSpec((None, None, d, tf), lambda g, f, be, nu: (layer, e_idx(g, be, nu), 0, f_idx(g, f, nu))),
            pl.BlockSpec((None, None, d, tf), lambda g, f, be, nu: (layer, e_idx(g, be, nu), 0, nf + f_idx(g, f, nu))),
            pl.BlockSpec((None, None, 1, tf), lambda g, f, be, nu: (layer, e_idx(g, be, nu), 0, f_idx(g, f, nu))),
            pl.BlockSpec((None, None, 1, tf), lambda g, f, be, nu: (layer, e_idx(g, be, nu), 0, nf + f_idx(g, f, nu))),
            pl.BlockSpec((None, None, tf, d), lambda g, f, be, nu: (layer, e_idx(g, be, nu), f_idx(g, f, nu), 0)),
            pl.BlockSpec((None, None, 1, d), lambda g, f, be, nu: (layer, e_idx(g, be, nu), 0, 0)),
        ],
        out_specs=pl.BlockSpec(memory_space=pl.ANY),
        scratch_shapes=[
            pltpu.VMEM((tm, half), jnp.uint32),
            pltpu.VMEM((tm, d), BF16),
            pltpu.VMEM((tm, d), F32),
            pltpu.VMEM((tm, half), jnp.uint32),
            pltpu.SemaphoreType.DMA(()),
            pltpu.SemaphoreType.DMA(()),
        ],
    )
    return pl.pallas_call(
        _moe_kernel,
        grid_spec=grid_spec,
        out_shape=jax.ShapeDtypeStruct((n_out_rows, half), jnp.uint32),
        compiler_params=_params(("arbitrary", "arbitrary")),
        name="moe_expert_ffn",
    )(block_e, n_used, src_rows.reshape(n_grid, 1, tm), dst_rows.reshape(n_grid, 1, tm), h_all,
      w_gu, w_gu, b_gu4, b_gu4, w_dn, b_dn4)


def _route(top_idx, n_exp, tm, n_blocks):
    n_tok, top_k = top_idx.shape
    n_rows = n_tok * top_k
    flat_e = top_idx.reshape(-1)
    onehot = (flat_e[:, None] == jnp.arange(n_exp, dtype=jnp.int32)[None, :]).astype(jnp.int32)
    csum = jnp.cumsum(onehot, axis=0)
    rank = jnp.sum(csum * onehot, axis=1) - 1
    counts = csum[-1]
    padded = (counts + tm - 1) // tm * tm
    pad_end = jnp.cumsum(padded)
    pad_start = pad_end - padded
    dest = pad_start[flat_e] + rank
    n_used = (pad_end[-1] // tm).astype(jnp.int32).reshape(1)
    block_start = jnp.arange(n_blocks, dtype=jnp.int32) * tm
    block_e = jnp.minimum(jnp.sum((pad_end[None, :] <= block_start[:, None]).astype(jnp.int32), axis=1), n_exp - 1)
    n_pos = (n_blocks + 2) * tm
    row_of_pos = jnp.full((n_pos,), -1, jnp.int32).at[dest].set(jnp.arange(n_rows, dtype=jnp.int32))
    pos = jnp.arange(n_pos, dtype=jnp.int32)
    src_rows = jnp.where(row_of_pos >= 0, row_of_pos // top_k, 0).reshape(n_blocks + 2, tm)
    dst_rows = jnp.where(row_of_pos >= 0, (row_of_pos % top_k) * n_tok + row_of_pos // top_k,
                         n_rows + pos).reshape(n_blocks + 2, tm)
    return src_rows, dst_rows, block_e, n_used, n_rows + n_pos


def _ffn_post_kernel(x1_ref, *refs):
    row_refs, (gate_ref, mod_ref, g_ref, o_ref) = refs[:TOP_K], refs[TOP_K:]
    gates = gate_ref[0]
    half = row_refs[0].shape[1]
    ffn_lo = ffn_hi = None
    for k in range(TOP_K):
        lo, hi = _unpack_bf16_halves(row_refs[k][...])
        gk = gates[:, k:k + 1]
        ffn_lo = gk * lo if k == 0 else ffn_lo + gk * lo
        ffn_hi = gk * hi if k == 0 else ffn_hi + gk * hi
    ssq = jnp.sum(ffn_lo * ffn_lo, axis=-1, keepdims=True) + jnp.sum(ffn_hi * ffn_hi, axis=-1, keepdims=True)
    inv = lax.rsqrt(ssq / (2 * half) + EPS)
    gt = mod_ref[0][5:6]
    g = g_ref[...]
    o_ref[0, :, :half] = x1_ref[0, :, :half] + gt[:, :half] * (ffn_lo * inv * g[:, :half])
    o_ref[0, :, half:] = x1_ref[0, :, half:] + gt[:, half:] * (ffn_hi * inv * g[:, half:])


def _ffn_post(x1, rows, gates, mod, g, tok_offset, n_tok):
    bsz, seq, d = x1.shape
    tm = _tile(seq, 256)
    assert tok_offset % tm == 0 and n_tok % tm == 0
    per_b = seq // tm
    row = pl.BlockSpec((1, tm, d), lambda b, i: (b, i, 0))

    def choice_spec(k):
        blk0 = (k * n_tok + tok_offset) // tm
        return pl.BlockSpec((tm, d // 2), lambda b, i: (blk0 + b * per_b + i, 0))

    return pl.pallas_call(
        _ffn_post_kernel,
        grid=(bsz, per_b),
        in_specs=[row] + [choice_spec(k) for k in range(TOP_K)] + [
            pl.BlockSpec((1, tm, LANES), lambda b, i: (b, i, 0)),
            pl.BlockSpec((1, 6, d), lambda b, i: (b, 0, 0)),
            pl.BlockSpec((1, d), lambda b, i: (0, 0)),
        ],
        out_specs=row,
        out_shape=jax.ShapeDtypeStruct((bsz, seq, d), F32),
        compiler_params=_params(("parallel", "parallel")),
        name="ffn_post",
    )(x1, *([rows] * TOP_K), gates, mod, g)


def _rot_half_cols(w, head_dim):
    d = w.shape[0]
    w3 = w.reshape(d, -1, head_dim)
    half = head_dim // 2
    return jnp.concatenate([-w3[..., half:], w3[..., :half]], axis=-1).reshape(d, -1)


def _rope_tables(pos, head_dim):
    half = head_dim // 2
    inv = ROPE_BASE ** (-jnp.arange(half, dtype=F32) / half)
    ang = pos.astype(F32)[:, None] * inv[None, :]
    cos = jnp.cos(ang)
    sin = jnp.sin(ang)
    return jnp.concatenate([cos, cos], axis=1), jnp.concatenate([sin, sin], axis=1)


class _Dims:
    def __init__(self, state_ret, cache_ckv, cache_kpe, lru_w_a, w_uk, w_uv, conv_w, w_router, w_down):
        _, _, self.ret_heads, self.ret_dk, self.ret_dv = state_ret.shape
        self.kv_lora = cache_ckv.shape[-1]
        self.qk_rope = cache_kpe.shape[-1]
        self.mix_a = conv_w.shape[-1]
        self.conv_w = conv_w.shape[1]
        self.mla_heads, self.qk_nope = w_uk.shape[2], w_uk.shape[3]
        self.v_head = w_uv.shape[3]
        self.n_exp = w_router.shape[-1]
        self.mix_b = self.ret_heads * self.ret_dv
        self.ret_qk = self.ret_heads * self.ret_dk


def _layer_weights(l, dm, w_in, conv_w, conv_b, lru_w_a, lru_b_a, lru_w_x, lru_b_x, lru_lam, g_q_norm, w_uq,
                   g_kv_norm, w_uk, w_uv, w_out, w_router, b_router, g_mix_pre, g_mix_post, g_ffn_pre, g_ffn_post):
    d = w_in.shape[1]
    q_lora = g_q_norm.shape[1]
    sizes = (dm.mix_a, dm.mix_a, dm.ret_qk, dm.ret_qk, dm.mix_b, dm.mix_b, q_lora, dm.kv_lora, dm.qk_rope)
    offs = np.concatenate([[0], np.cumsum(sizes)])
    wl = w_in[l]
    seg = [wl[:, offs[i]:offs[i + 1]] for i in range(len(sizes))]
    w_xa, w_ya, w_qr, w_kr, w_vr, w_gr, w_ql, w_kvl, w_kpe = seg
    blk = dm.mix_b
    kv_cols = dm.kv_lora + 2 * dm.qk_rope
    kv_pad = (-kv_cols) % blk
    cols = [w_xa, w_ya, jnp.concatenate([w_qr, w_kr], axis=1), w_vr, w_gr, w_ql,
            jnp.concatenate([w_kvl, w_kpe, _rot_half_cols(w_kpe, dm.qk_rope), jnp.zeros((d, kv_pad), F32)], axis=1),
            jnp.concatenate([_rot_half_cols(w_qr, dm.ret_dk), _rot_half_cols(w_kr, dm.ret_dk)], axis=1)]
    assert all(c.shape[1] == blk for c in cols), [c.shape for c in cols]
    w_ext = jnp.concatenate(cols, axis=1).astype(BF16)

    n_lru = lru_w_a.shape[1]
    eye = jnp.eye(n_lru, dtype=F32)
    block_diag = lambda w: jnp.einsum("nde,nm->ndme", w, eye).reshape(dm.mix_a, dm.mix_a).astype(BF16)

    qk_dim = dm.qk_nope + dm.qk_rope
    wq3 = w_uq[l].reshape(q_lora, dm.mla_heads, qk_dim)
    pe = wq3[..., dm.qk_nope:]
    half = dm.qk_rope // 2
    pe_rot = jnp.concatenate([-pe[..., half:], pe[..., :half]], axis=-1)
    w_uq_ext = jnp.concatenate([wq3, pe_rot], axis=-1).transpose(1, 0, 2).astype(BF16)

    wo = w_out[l].astype(BF16)
    n_exp = dm.n_exp
    wr = jnp.concatenate([w_router[l], jnp.zeros((d, LANES - n_exp), F32)], axis=1)
    wr_hi = wr.astype(BF16)
    wr_lo = (wr - wr_hi.astype(F32)).astype(BF16)
    br = jnp.concatenate([b_router[l], jnp.full((LANES - n_exp,), NEG_BIG, F32)]).reshape(1, LANES)
    row = lambda v: v[l].reshape(1, -1)
    return dict(
        w_ext=w_ext, conv_w=conv_w[l], conv_b=row(conv_b),
        wa_bd=block_diag(lru_w_a[l]), wx_bd=block_diag(lru_w_x[l]),
        ba=row(lru_b_a), bx=row(lru_b_x), lam=row(lru_lam),
        g_q=row(g_q_norm), w_uq_ext=w_uq_ext, g_kv=row(g_kv_norm),
        w_uk=w_uk[l].transpose(1, 0, 2).astype(BF16), w_uv=w_uv[l].transpose(1, 0, 2).astype(BF16),
        wo_a=wo[:dm.mix_a], wo_b=wo[dm.mix_a:dm.mix_a + dm.mix_b], wo_c=wo[dm.mix_a + dm.mix_b:],
        wr_hi=wr_hi, wr_lo=wr_lo, br=br,
        g_mix_pre=row(g_mix_pre), g_mix_post=row(g_mix_post), g_ffn_pre=row(g_ffn_pre), g_ffn_post=row(g_ffn_post),
    )


def _mixer(x, mod, wl, dm, conv_buf, h0, s0, past_ckv, past_kpe):
    bsz, seq, d = x.shape
    past_len = 0 if past_ckv is None else past_ckv.shape[1]
    pos = past_len + jnp.arange(seq, dtype=jnp.int32)
    u = _in_proj(x, mod, wl["g_mix_pre"], wl["w_ext"])

    ctx_rows = 8
    n_ctx = dm.conv_w - 1
    conv_ctx = jnp.concatenate([jnp.zeros((bsz, ctx_rows - n_ctx, dm.mix_a), F32), conv_buf], axis=1)
    a_out, conv_new, h_new = _rg_lru(u, conv_ctx, h0.reshape(bsz, 1, dm.mix_a), wl["conv_w"], wl["conv_b"],
                                     wl["wa_bd"], wl["wx_bd"], wl["ba"], wl["bx"], wl["lam"])

    cos_k, sin_k = _rope_tables(pos, dm.ret_dk)
    k_scale = float(dm.ret_dk) ** -0.5
    scale_row = jnp.concatenate([jnp.ones((dm.ret_qk,), F32), jnp.full((dm.ret_qk,), k_scale, F32)])[None, :]
    cos_t = jnp.tile(cos_k, (1, 2 * dm.ret_heads)) * scale_row
    sin_t = jnp.tile(sin_k, (1, 2 * dm.ret_heads)) * scale_row
    b_out, s_new = _retention(u, s0, cos_t, sin_t, blocks=(2, 3, 4, 7))

    cos_r, sin_r = _rope_tables(pos, dm.qk_rope)
    lane_pad = lambda t, left, right, fill: jnp.concatenate(
        [jnp.full((seq, left), fill, F32), t, jnp.zeros((seq, right), F32)], axis=1)
    ckv_new, kpe_new = _latent(u, wl["g_kv"], lane_pad(cos_r, 0, LANES - dm.qk_rope, 0.0),
                               lane_pad(sin_r, 0, LANES - dm.qk_rope, 0.0), kv_blk=6, blk_w=dm.mix_b, rope_w=dm.qk_rope)
    q = _q_proj(u, wl["g_q"], wl["w_uq_ext"], lane_pad(cos_r, dm.qk_nope, dm.qk_rope, 1.0),
                lane_pad(sin_r, dm.qk_nope, dm.qk_rope, 0.0), ql_blk=5, rope_w=dm.qk_rope)
    if past_ckv is None:
        ckv_all, kpe_all = ckv_new, kpe_new
    else:
        past_kpe_pad = jnp.concatenate([past_kpe, jnp.zeros(past_kpe.shape[:2] + (LANES - dm.qk_rope,), F32)], axis=2)
        ckv_all = jnp.concatenate([past_ckv, ckv_new], axis=1)
        kpe_all = jnp.concatenate([past_kpe_pad, kpe_new], axis=1)
    k, v = _kv_proj(ckv_all, kpe_all, wl["w_uk"], wl["w_uv"])
    c_out = _attention(q, k, v, past_len, dm.qk_nope + dm.qk_rope)

    x1, h2, gates, idx = _out_proj(a_out, b_out, c_out, x, mod, wl["g_mix_post"], wl["g_ffn_pre"],
                                   wl["wo_a"], wl["wo_b"], wl["wo_c"], wl["wr_hi"], wl["wr_lo"], wl["br"])
    states = (ckv_new, kpe_new[:, :, :dm.qk_rope], conv_new[:, ctx_rows - n_ctx:], h_new.reshape(bsz, dm.mix_a), s_new)
    return x1, h2, gates, idx[:, :, :TOP_K], states


def kernel(x_prompt, x_sample, c_prompt, c_sample, cache_ckv, cache_kpe, state_conv, state_lru, state_ret, w_ada, b_ada, g_mix_pre, g_mix_post, g_ffn_pre, g_ffn_post, w_in, conv_w, conv_b, lru_w_a, lru_b_a, lru_w_x, lru_b_x, lru_lam, g_q_norm, w_uq, g_kv_norm, w_uk, w_uv, w_out, w_router, b_router, w_gate_up, b_gate_up, w_down, b_down):
    depth = w_in.shape[0]
    dm = _Dims(state_ret, cache_ckv, cache_kpe, lru_w_a, w_uk, w_uv, conv_w, w_router, w_down)
    bp, lp, d = x_prompt.shape
    bs, ls, _ = x_sample.shape
    n_tok = bp * lp + bs * ls
    moe_tm = min(1024, max(16, 1 << int(np.log2((n_tok * TOP_K) // dm.n_exp))))
    n_blocks = -(-(n_tok * TOP_K) // moe_tm) + dm.n_exp

    xp, xs = x_prompt, x_sample
    p_states, s_states = [], []
    for l in range(depth):
        wl = _layer_weights(l, dm, w_in, conv_w, conv_b, lru_w_a, lru_b_a, lru_w_x, lru_b_x, lru_lam, g_q_norm, w_uq,
                            g_kv_norm, w_uk, w_uv, w_out, w_router, b_router, g_mix_pre, g_mix_post, g_ffn_pre, g_ffn_post)
        mod_p = _ada_mod(c_prompt, w_ada, b_ada, l).reshape(bp, 6, d)
        mod_s = _ada_mod(c_sample, w_ada, b_ada, l).reshape(bs, 6, d)
        zeros = lambda *s: jnp.zeros(s, F32)
        x1p, h2p, gp, ip, st_p = _mixer(
            xp, mod_p, wl, dm, zeros(bp, dm.conv_w - 1, dm.mix_a), zeros(bp, dm.mix_a),
            zeros(bp, dm.ret_heads, dm.ret_dk, dm.ret_dv), None, None)
        x1s, h2s, gs, is_, st_s = _mixer(
            xs, mod_s, wl, dm, state_conv[l], state_lru[l], state_ret[l], cache_ckv[l], cache_kpe[l])
        p_states.append(st_p)
        s_states.append(st_s)

        h2 = jnp.concatenate([h2p.reshape(-1, d // 2), h2s.reshape(-1, d // 2)], axis=0)
        top_idx = jnp.concatenate([ip.reshape(-1, TOP_K), is_.reshape(-1, TOP_K)], axis=0)
        src_rows, dst_rows, block_e, n_used, n_out_rows = _route(top_idx, dm.n_exp, moe_tm, n_blocks)
        rows = _moe_ffn(h2, src_rows, dst_rows, block_e, n_used, w_gate_up, b_gate_up, w_down, b_down, l, n_out_rows)
        xp = _ffn_post(x1p, rows, gp, mod_p, wl["g_ffn_post"], 0, n_tok)
        xs = _ffn_post(x1s, rows, gs, mod_s, wl["g_ffn_post"], bp * lp, n_tok)

    stack = lambda sts: tuple(jnp.stack(t, axis=0) for t in zip(*sts))
    p_ckv, p_kpe, p_conv, p_lru, p_ret = stack(p_states)
    s_ckv, s_kpe, s_conv, s_lru, s_ret = stack(s_states)
    return (xp, xs, p_ckv, p_kpe, p_conv, p_lru, p_ret, s_ckv, s_kpe, s_conv, s_lru, s_ret)
```

```python
import functools

import numpy as np
import jax
import jax.numpy as jnp
from jax import lax
from jax.experimental import pallas as pl
from jax.experimental.pallas import tpu as pltpu

CHUNK = 64
EPS = 1e-6
ROPE_BASE = 10000.0
RG_C = 8.0
TOP_K = 4
SWIGLU_LIMIT = 7.0
SWIGLU_ALPHA = 1.702
NEG_BIG = -1e30
GELU_C = float(np.sqrt(2.0 / np.pi))

LANES = 128
VMEM_LIMIT_BYTES = 56 * 1024 * 1024

F32 = jnp.float32
BF16 = jnp.bfloat16


def _params(semantics):
    return pltpu.CompilerParams(dimension_semantics=semantics, vmem_limit_bytes=VMEM_LIMIT_BYTES)


def _tile(n, pref):
    if n <= pref:
        return n
    t = pref
    while n % t:
        t //= 2
    return t


def _rms(x, g):
    return x * lax.rsqrt(jnp.mean(x * x, axis=-1, keepdims=True) + EPS) * g


def _dot(a, b):
    return jnp.dot(a, b, preferred_element_type=F32)


def _dot_nt(a, b):
    return lax.dot_general(a, b, (((1,), (1,)), ((), ())), preferred_element_type=F32)


def _dot_tn(a, b):
    return lax.dot_general(a, b, (((0,), (0,)), ((), ())), preferred_element_type=F32)


def _sigmoid(x):
    return 1.0 / (1.0 + jnp.exp(-x))


def _pack_bf16_halves(x):
    n = x.shape[1] // 2
    bits = lax.bitcast_convert_type(x.astype(BF16).astype(F32), jnp.uint32)
    return (bits[:, :n] >> 16) | (bits[:, n:] & jnp.uint32(0xFFFF0000))


def _unpack_bf16_halves(w):
    lo = lax.bitcast_convert_type(w << 16, F32)
    hi = lax.bitcast_convert_type(w & jnp.uint32(0xFFFF0000), F32)
    return lo, hi


def _store_token_tiles(ref, lead, packed):
    tm, n = packed.shape
    nb = n // LANES
    for c in range(nb):
        idx = (pl.ds(c, tm, stride=nb), slice(None))
        ref[lead + idx if lead else idx] = packed[:, c * LANES:(c + 1) * LANES]


def _load_token_tile_block(ref, c, tm, nb):
    return ref[pl.ds(c, tm, stride=nb), :]


def _ada_kernel(c_ref, w_ref, b_ref, o_ref):
    c = c_ref[...]
    s = (c * _sigmoid(c)).astype(BF16)
    o_ref[...] = _dot(s, w_ref[...].astype(BF16)) + b_ref[...]


def _ada_mod(c, w_ada, b_ada, layer):
    bsz, d = c.shape
    n = w_ada.shape[-1]
    tn = _tile(n, 1024)
    return pl.pallas_call(
        _ada_kernel,
        grid=(n // tn,),
        in_specs=[
            pl.BlockSpec((bsz, d), lambda j: (0, 0)),
            pl.BlockSpec((None, d, tn), lambda j: (layer, 0, j)),
            pl.BlockSpec((None, 1, tn), lambda j: (layer, 0, j)),
        ],
        out_specs=pl.BlockSpec((bsz, tn), lambda j: (0, j)),
        out_shape=jax.ShapeDtypeStruct((bsz, n), F32),
        compiler_params=_params(("arbitrary",)),
        name="ada_mod",
    )(c, w_ada, b_ada.reshape(b_ada.shape[0], 1, n))


def _in_proj_kernel(x_ref, mod_ref, g_ref, w_ref, u_ref):
    mod = mod_ref[0]
    h = _rms(x_ref[0], g_ref[...]) * (1.0 + mod[1:2]) + mod[0:1]
    u_ref[0] = _dot(h.astype(BF16), w_ref[...])


def _in_proj(x, mod, g, w_ext):
    bsz, seq, d = x.shape
    n = w_ext.shape[1]
    tm = _tile(seq, 256)
    return pl.pallas_call(
        _in_proj_kernel,
        grid=(bsz, seq // tm),
        in_specs=[
            pl.BlockSpec((1, tm, d), lambda b, i: (b, i, 0)),
            pl.BlockSpec((1, 6, d), lambda b, i: (b, 0, 0)),
            pl.BlockSpec((1, d), lambda b, i: (0, 0)),
            pl.BlockSpec((d, n), lambda b, i: (0, 0)),
        ],
        out_specs=pl.BlockSpec((1, tm, n), lambda b, i: (b, i, 0)),
        out_shape=jax.ShapeDtypeStruct((bsz, seq, n), F32),
        compiler_params=_params(("parallel", "parallel")),
        name="in_proj",
    )(x, mod, g, w_ext)


def _lru_kernel(xa_ref, ya_ref, cbuf_ref, h0_ref, cw_ref, cb_ref, wa_ref, wx_ref, ba_ref, bx_ref,
                lam_ref, out_ref, cnew_ref, hlast_ref, xbuf, hcar):
    tl = xa_ref.shape[1]
    width = xa_ref.shape[2]
    ctx = xbuf.shape[0] - tl

    @pl.when(pl.program_id(1) == 0)
    def _():
        xbuf[0:ctx, :] = cbuf_ref[0]
        hcar[...] = h0_ref[0]

    xa = xa_ref[0]
    xbuf[ctx:ctx + tl, :] = xa
    cw = cw_ref[...]
    n_tap = cw.shape[0]
    xc = cb_ref[...] + cw[n_tap - 1:n_tap] * xa
    for k in range(n_tap - 1):
        off = ctx - (n_tap - 1) + k
        xc = xc + cw[k:k + 1] * xbuf[off:off + tl, :]
    new_ctx = xbuf[tl:tl + ctx, :]
    xbuf[0:ctx, :] = new_ctx
    cnew_ref[0] = new_ctx

    xcb = xc.astype(BF16)
    r = _sigmoid(_dot(xcb, wa_ref[...]) + ba_ref[...])
    gi = _sigmoid(_dot(xcb, wx_ref[...]) + bx_ref[...])
    z = -lam_ref[...]
    softplus = jnp.maximum(z, 0.0) + jnp.log1p(jnp.exp(-jnp.abs(z)))
    log_a = (-RG_C) * r * softplus
    a = jnp.exp(log_a)
    b = jnp.sqrt(1.0 - jnp.exp(2.0 * log_a)) * gi * xc

    row = lax.broadcasted_iota(jnp.int32, (tl, width), 0)
    s = 1
    while s < tl:
        a_sh = pltpu.roll(a, s, 0)
        b_sh = pltpu.roll(b, s, 0)
        valid = row >= s
        b = jnp.where(valid, a * b_sh + b, b)
        a = jnp.where(valid, a * a_sh, a)
        s *= 2
    h = a * hcar[...] + b
    h_last = h[tl - 1:tl, :]
    hcar[...] = h_last
    hlast_ref[0] = h_last

    ya = ya_ref[0]
    gelu = 0.5 * ya * (1.0 + jnp.tanh(GELU_C * (ya + 0.044715 * (ya * ya * ya))))
    out_ref[0] = (h * gelu).astype(out_ref.dtype)


def _rg_lru(u, conv_ctx, h0, cw, cb, wa_bd, wx_bd, ba, bx, lam):
    bsz, seq, _ = u.shape
    width = cw.shape[1]
    tl = _tile(seq, 512)
    ctx = conv_ctx.shape[1]
    xa_blk = 0
    ya_blk = 1
    vec = lambda: pl.BlockSpec((1, width), lambda b, t: (0, 0))
    return pl.pallas_call(
        _lru_kernel,
        grid=(bsz, seq // tl),
        in_specs=[
            pl.BlockSpec((1, tl, width), lambda b, t: (b, t, xa_blk)),
            pl.BlockSpec((1, tl, width), lambda b, t: (b, t, ya_blk)),
            pl.BlockSpec((1, ctx, width), lambda b, t: (b, 0, 0)),
            pl.BlockSpec((1, 1, width), lambda b, t: (b, 0, 0)),
            pl.BlockSpec(cw.shape, lambda b, t: (0, 0)),
            vec(),
            pl.BlockSpec((width, width), lambda b, t: (0, 0)),
            pl.BlockSpec((width, width), lambda b, t: (0, 0)),
            vec(), vec(), vec(),
        ],
        out_specs=[
            pl.BlockSpec((1, tl, width), lambda b, t: (b, t, 0)),
            pl.BlockSpec((1, ctx, width), lambda b, t: (b, 0, 0)),
            pl.BlockSpec((1, 1, width), lambda b, t: (b, 0, 0)),
        ],
        out_shape=[
            jax.ShapeDtypeStruct((bsz, seq, width), BF16),
            jax.ShapeDtypeStruct((bsz, ctx, width), F32),
            jax.ShapeDtypeStruct((bsz, 1, width), F32),
        ],
        scratch_shapes=[pltpu.VMEM((tl + ctx, width), F32), pltpu.VMEM((1, width), F32)],
        compiler_params=_params(("parallel", "arbitrary")),
        name="rg_lru",
    )(u, u, conv_ctx, h0, cw, cb, wa_bd, wx_bd, ba, bx, lam)


def _ret_kernel(qk_ref, rot_ref, v_ref, g_ref, cos_ref, sin_ref, dmask_ref, qdec_ref, kdec_ref,
                s0_ref, out_ref, slast_ref, s_sc, *, gchunk):
    n_heads, dk, dv = s_sc.shape
    hdk = n_heads * dk

    @pl.when(pl.program_id(1) == 0)
    def _():
        s_sc[...] = s0_ref[0]

    qkr = qk_ref[0] * cos_ref[...] + rot_ref[0] * sin_ref[...]
    v = v_ref[0]
    vdec = v * kdec_ref[...]
    gate = g_ref[0]
    qdec = qdec_ref[...]
    for h in range(n_heads):
        q = qkr[:, h * dk:(h + 1) * dk].astype(BF16)
        k = qkr[:, hdk + h * dk:hdk + (h + 1) * dk].astype(BF16)
        vh = v[:, h * dv:(h + 1) * dv].astype(BF16)
        vd = vdec[:, h * dv:(h + 1) * dv].astype(BF16)
        s_prev = s_sc[h]
        scores = _dot_nt(q, k) * dmask_ref[h]
        o = _dot(scores.astype(BF16), vh) + qdec[:, h * dv:(h + 1) * dv] * _dot(q, s_prev.astype(BF16))
        s_sc[h] = s_prev * gchunk[h] + _dot_tn(k, vd)
        o = o * lax.rsqrt(jnp.mean(o * o, axis=-1, keepdims=True) + EPS)
        gh = gate[:, h * dv:(h + 1) * dv]
        out_ref[0, :, h * dv:(h + 1) * dv] = (o * (gh * _sigmoid(gh))).astype(out_ref.dtype)
    slast_ref[0] = s_sc[...]


def _retention(u, s0, cos_t, sin_t, blocks):
    bsz, seq, _ = u.shape
    _, n_heads, dk, dv = s0.shape
    width = n_heads * dv
    tc = _tile(seq, 256)
    log_g = np.log1p(-np.exp2(-5.0 - np.arange(n_heads, dtype=np.float64)))
    idx = np.arange(tc, dtype=np.float64)
    diff = idx[:, None] - idx[None, :]
    dmask = np.where(diff[None] >= 0, np.exp(np.maximum(diff, 0.0)[None] * log_g[:, None, None]), 0.0)
    qdec = np.repeat(np.exp((idx + 1.0)[:, None] * log_g[None, :]), dv, axis=1)
    kdec = np.repeat(np.exp((tc - 1.0 - idx)[:, None] * log_g[None, :]), dv, axis=1)
    gchunk = tuple(float(g) for g in np.exp(tc * log_g))
    qk_blk, v_blk, g_blk, rot_blk = blocks
    u_spec = lambda blk: pl.BlockSpec((1, tc, width), lambda b, n: (b, n, blk))
    full2 = lambda a: pl.BlockSpec(a.shape, lambda b, n: (0, 0))
    return pl.pallas_call(
        functools.partial(_ret_kernel, gchunk=gchunk),
        grid=(bsz, seq // tc),
        in_specs=[
            u_spec(qk_blk), u_spec(rot_blk), u_spec(v_blk), u_spec(g_blk),
            pl.BlockSpec((tc, width), lambda b, n: (n, 0)),
            pl.BlockSpec((tc, width), lambda b, n: (n, 0)),
            pl.BlockSpec(dmask.shape, lambda b, n: (0, 0, 0)),
            full2(qdec), full2(kdec),
            pl.BlockSpec((1, n_heads, dk, dv), lambda b, n: (b, 0, 0, 0)),
        ],
        out_specs=[
            pl.BlockSpec((1, tc, width), lambda b, n: (b, n, 0)),
            pl.BlockSpec((1, n_heads, dk, dv), lambda b, n: (b, 0, 0, 0)),
        ],
        out_shape=[
            jax.ShapeDtypeStruct((bsz, seq, width), BF16),
            jax.ShapeDtypeStruct((bsz, n_heads, dk, dv), F32),
        ],
        scratch_shapes=[pltpu.VMEM((n_heads, dk, dv), F32)],
        compiler_params=_params(("parallel", "arbitrary")),
        name="retention",
    )(u, u, u, u, cos_t, sin_t, jnp.asarray(dmask, F32), jnp.asarray(qdec, F32),
      jnp.asarray(kdec, F32), s0)


def _latent_kernel(kv_ref, g_ref, cos_ref, sin_ref, ckv_ref, kpe_ref, *, rope_w):
    lora = ckv_ref.shape[2]
    blk = kv_ref[0]
    ckv_ref[0] = _rms(blk[:, :lora], g_ref[...])
    kp = blk[:, lora:lora + LANES]
    rolled = pltpu.roll(kp, LANES - rope_w, 1)
    kpe_ref[0] = kp * cos_ref[...] + rolled * sin_ref[...]


def _latent(u, g_kv, cos_p, sin_p, kv_blk, blk_w, rope_w):
    bsz, seq, _ = u.shape
    lora = g_kv.shape[1]
    tm = _tile(seq, 512)
    return pl.pallas_call(
        functools.partial(_latent_kernel, rope_w=rope_w),
        grid=(bsz, seq // tm),
        in_specs=[
            pl.BlockSpec((1, tm, blk_w), lambda b, i: (b, i, kv_blk)),
            pl.BlockSpec((1, lora), lambda b, i: (0, 0)),
            pl.BlockSpec((tm, LANES), lambda b, i: (i, 0)),
            pl.BlockSpec((tm, LANES), lambda b, i: (i, 0)),
        ],
        out_specs=[
            pl.BlockSpec((1, tm, lora), lambda b, i: (b, i, 0)),
            pl.BlockSpec((1, tm, LANES), lambda b, i: (b, i, 0)),
        ],
        out_shape=[
            jax.ShapeDtypeStruct((bsz, seq, lora), F32),
            jax.ShapeDtypeStruct((bsz, seq, LANES), F32),
        ],
        compiler_params=_params(("parallel", "parallel")),
        name="latent_norm_rope",
    )(u, g_kv, cos_p, sin_p)


def _q_proj_kernel(ql_ref, g_ref, w_ref, cos_ref, sin_ref, q_ref, *, rope_w):
    cq = _rms(ql_ref[0], g_ref[...]).astype(BF16)
    cos = cos_ref[...]
    sin = sin_ref[...]
    for h in range(w_ref.shape[0]):
        r = _dot(cq, w_ref[h])
        rolled = pltpu.roll(r, r.shape[1] - rope_w, 1)
        q_ref[0, h] = (r * cos + rolled * sin).astype(q_ref.dtype)


def _q_proj(u, g_q, w_uq_ext, cos_q, sin_q, ql_blk, rope_w):
    bsz, seq, _ = u.shape
    n_heads, q_lora, width = w_uq_ext.shape
    tm = _tile(seq, 512)
    return pl.pallas_call(
        functools.partial(_q_proj_kernel, rope_w=rope_w),
        grid=(bsz, seq // tm),
        in_specs=[
            pl.BlockSpec((1, tm, q_lora), lambda b, i: (b, i, ql_blk)),
            pl.BlockSpec((1, q_lora), lambda b, i: (0, 0)),
            pl.BlockSpec((n_heads, q_lora, width), lambda b, i: (0, 0, 0)),
            pl.BlockSpec((tm, width), lambda b, i: (i, 0)),
            pl.BlockSpec((tm, width), lambda b, i: (i, 0)),
        ],
        out_specs=pl.BlockSpec((1, n_heads, tm, width), lambda b, i: (b, 0, i, 0)),
        out_shape=jax.ShapeDtypeStruct((bsz, n_heads, seq, width), BF16),
        compiler_params=_params(("parallel", "parallel")),
        name="mla_q_proj",
    )(u, g_q, w_uq_ext, cos_q, sin_q)


def _kv_proj_kernel(ckv_ref, kpe_ref, wk_ref, wv_ref, k_ref, v_ref):
    ckv = ckv_ref[0].astype(BF16)
    kpe = kpe_ref[0].astype(k_ref.dtype)
    nope = wk_ref.shape[2]
    for h in range(wk_ref.shape[0]):
        k_ref[0, h, :, :nope] = _dot(ckv, wk_ref[h]).astype(k_ref.dtype)
        k_ref[0, h, :, nope:] = kpe
        v_ref[0, h] = _dot(ckv, wv_ref[h]).astype(v_ref.dtype)


def _kv_proj(ckv, kpe, w_uk, w_uv):
    bsz, klen, lora = ckv.shape
    n_heads, _, nope = w_uk.shape
    vdim = w_uv.shape[2]
    tm = _tile(klen, 512) if klen % 512 == 0 else klen
    return pl.pallas_call(
        _kv_proj_kernel,
        grid=(bsz, klen // tm),
        in_specs=[
            pl.BlockSpec((1, tm, lora), lambda b, i: (b, i, 0)),
            pl.BlockSpec((1, tm, LANES), lambda b, i: (b, i, 0)),
            pl.BlockSpec((n_heads, lora, nope), lambda b, i: (0, 0, 0)),
            pl.BlockSpec((n_heads, lora, vdim), lambda b, i: (0, 0, 0)),
        ],
        out_specs=[
            pl.BlockSpec((1, n_heads, tm, nope + LANES), lambda b, i: (b, 0, i, 0)),
            pl.BlockSpec((1, n_heads, tm, vdim), lambda b, i: (b, 0, i, 0)),
        ],
        out_shape=[
            jax.ShapeDtypeStruct((bsz, n_heads, klen, nope + LANES), BF16),
            jax.ShapeDtypeStruct((bsz, n_heads, klen, vdim), BF16),
        ],
        compiler_params=_params(("parallel", "parallel")),
        name="mla_kv_proj",
    )(ckv, kpe, w_uk, w_uv)


def _attn_kernel(q_ref, k_ref, v_ref, o_ref, m_sc, l_sc, acc_sc, s_sc, *, q_pos0, scale, tk):
    qi = pl.program_id(2)
    tq = q_ref.shape[2]
    nk = k_ref.shape[2] // tk
    q = q_ref[0, 0]
    m_sc[...] = jnp.full(m_sc.shape, NEG_BIG, F32)
    l_sc[...] = jnp.zeros(l_sc.shape, F32)
    acc_sc[...] = jnp.zeros(acc_sc.shape, F32)

    last_q_chunk = (q_pos0 + qi * tq + tq - 1) // CHUNK
    n_seen = jnp.minimum((last_q_chunk * CHUNK + CHUNK - 1) // tk + 1, nk)
    q_chunk = (q_pos0 + qi * tq + lax.broadcasted_iota(jnp.int32, (tq, 1), 0)) // CHUNK
    k_chunk0 = lax.broadcasted_iota(jnp.int32, (1, tk), 1) // CHUNK

    def scores(j):
        start = pl.multiple_of(j * tk, tk)
        return _dot_nt(q, k_ref[0, 0, pl.ds(start, tk), :])

    s_sc[0] = scores(0)

    def step(j, carry):
        slot = j % 2
        s = s_sc[slot]
        s_sc[1 - slot] = scores(jnp.minimum(j + 1, n_seen - 1))
        start = pl.multiple_of(j * tk, tk)
        s = jnp.where(k_chunk0 + start // CHUNK <= q_chunk, s * scale, NEG_BIG)
        m_prev = m_sc[...]
        m_new = jnp.maximum(m_prev, jnp.max(s, axis=-1, keepdims=True))
        alpha = jnp.exp(m_prev - m_new)
        p = jnp.exp(s - m_new)
        l_sc[...] = alpha * l_sc[...] + jnp.sum(p, axis=-1, keepdims=True)
        acc_sc[...] = alpha * acc_sc[...] + _dot(p.astype(BF16), v_ref[0, 0, pl.ds(start, tk), :])
        m_sc[...] = m_new
        return carry

    lax.fori_loop(0, n_seen, step, 0)
    o_ref[0] = (acc_sc[...] / l_sc[...]).astype(o_ref.dtype)


def _attention(q, k, v, q_pos0, qk_dim):
    bsz, n_heads, seq, width = q.shape
    klen = k.shape[2]
    vdim = v.shape[3]
    tq = _tile(seq, 512)
    tk = _tile(klen, 512) if klen % 512 == 0 else klen
    assert tk == klen or tk % CHUNK == 0
    return pl.pallas_call(
        functools.partial(_attn_kernel, q_pos0=q_pos0, scale=float(qk_dim) ** -0.5, tk=tk),
        grid=(bsz, n_heads, seq // tq),
        in_specs=[
            pl.BlockSpec((1, 1, tq, width), lambda b, h, qi: (b, h, qi, 0)),
            pl.BlockSpec((1, 1, klen, width), lambda b, h, qi: (b, h, 0, 0)),
            pl.BlockSpec((1, 1, klen, vdim), lambda b, h, qi: (b, h, 0, 0)),
        ],
        out_specs=pl.BlockSpec((1, tq, vdim), lambda b, h, qi: (b, qi, h)),
        out_shape=jax.ShapeDtypeStruct((bsz, seq, n_heads * vdim), BF16),
        scratch_shapes=[pltpu.VMEM((tq, 1), F32), pltpu.VMEM((tq, 1), F32), pltpu.VMEM((tq, vdim), F32),
                        pltpu.VMEM((2, tq, tk), F32)],
        compiler_params=_params(("parallel", "parallel", "arbitrary")),
        name="mla_attention",
    )(q, k, v)


def _out_proj_kernel(a_ref, b_ref, c_ref, x_ref, mod_ref, gpost_ref, gpre_ref, wa_ref, wb_ref, wc_ref,
                     wrh_ref, wrl_ref, br_ref, x1_ref, h2_ref, gate_ref, idx_ref):
    mix = _dot(a_ref[0], wa_ref[...]) + _dot(b_ref[0], wb_ref[...]) + _dot(c_ref[0], wc_ref[...])
    mod = mod_ref[0]
    x1 = x_ref[0] + mod[2:3] * _rms(mix, gpost_ref[...])
    x1_ref[0] = x1
    h2 = _rms(x1, gpre_ref[...]) * (1.0 + mod[4:5]) + mod[3:4]
    _store_token_tiles(h2_ref, (0,), _pack_bf16_halves(h2))
    hi = h2.astype(BF16)
    lo = (h2 - hi.astype(F32)).astype(BF16)
    logits = _dot(hi, wrh_ref[...]) + (_dot(hi, wrl_ref[...]) + _dot(lo, wrh_ref[...])) + br_ref[...]
    lane = lax.broadcasted_iota(jnp.int32, logits.shape, 1)
    lane_f = lane.astype(F32)
    gates = jnp.zeros(logits.shape, F32)
    idxs = jnp.zeros(logits.shape, F32)
    top0 = None
    denom = None
    for k in range(TOP_K):
        m = jnp.max(logits, axis=-1, keepdims=True)
        sel = jnp.min(jnp.where(logits == m, lane_f, float(LANES)), axis=-1, keepdims=True)
        if k == 0:
            top0 = m
            e = jnp.ones_like(m)
            denom = e
        else:
            e = jnp.exp(m - top0)
            denom = denom + e
        gates = jnp.where(lane == k, e, gates)
        idxs = jnp.where(lane == k, sel, idxs)
        logits = jnp.where(lane_f == sel, NEG_BIG * 2.0, logits)
    gate_ref[0] = gates / denom
    idx_ref[0] = idxs.astype(jnp.int32)


def _out_proj(a_out, b_out, c_out, x, mod, g_post, g_pre, wa, wb, wc, wr_hi, wr_lo, br):
    bsz, seq, d = x.shape
    tm = _tile(seq, 256)
    nb = d // 2 // LANES
    row = lambda w: pl.BlockSpec((1, tm, w), lambda b, i: (b, i, 0))
    const = lambda a: pl.BlockSpec(a.shape, lambda b, i: (0, 0))
    return pl.pallas_call(
        _out_proj_kernel,
        grid=(bsz, seq // tm),
        in_specs=[
            row(a_out.shape[2]), row(b_out.shape[2]), row(c_out.shape[2]), row(d),
            pl.BlockSpec((1, 6, d), lambda b, i: (b, 0, 0)),
            const(g_post), const(g_pre), const(wa), const(wb), const(wc),
            const(wr_hi), const(wr_lo), const(br),
        ],
        out_specs=[row(d), pl.BlockSpec((1, tm * nb, LANES), lambda b, i: (b, i, 0)), row(LANES), row(LANES)],
        out_shape=[
            jax.ShapeDtypeStruct((bsz, seq, d), F32),
            jax.ShapeDtypeStruct((bsz, seq * nb, LANES), jnp.uint32),
            jax.ShapeDtypeStruct((bsz, seq, LANES), F32),
            jax.ShapeDtypeStruct((bsz, seq, LANES), jnp.int32),
        ],
        compiler_params=_params(("parallel", "parallel")),
        name="out_proj_router",
    )(a_out, b_out, c_out, x, mod, g_post, g_pre, wa, wb, wc, wr_hi, wr_lo, br)


def _moe_kernel(be_ref, nu_ref, src_ref, dst_ref, h_hbm, wg_ref, wu_ref, bg_ref, bu_ref, wd_ref, bd_ref,
                rows_hbm, xbuf, xb, acc, pbuf, gsem, ssem):
    g = pl.program_id(0)
    f = pl.program_id(1)
    nf = pl.num_programs(1)
    n_used = nu_ref[0]
    tm, d = xb.shape
    half = d // 2
    nb = half // LANES
    chunk = tm // nf
    live = jnp.logical_and(g >= 1, g <= n_used)

    def token(ref, t):
        return ref.at[pl.ds(pl.multiple_of(t * nb, nb), nb)]

    def gather_copy(r):
        return pltpu.make_async_copy(token(h_hbm, src_ref[0, r]), token(xbuf, r), gsem)

    def scatter_copy(r):
        return pltpu.make_async_copy(token(pbuf, r), token(rows_hbm, dst_ref[0, r]), ssem)

    def wait_gather():
        pltpu.make_async_copy(h_hbm.at[pl.ds(0, tm * nb)], xbuf, gsem).wait()

    def wait_scatter():
        pltpu.make_async_copy(pbuf, rows_hbm.at[pl.ds(0, tm * nb)], ssem).wait()

    @pl.when(jnp.logical_and(g == 0, f == 0))
    def _():
        pbuf[...] = jnp.zeros(pbuf.shape, pbuf.dtype)
        lax.fori_loop(0, tm, lambda r, c: (gather_copy(r).start(), c)[1], 0)

    @pl.when(jnp.logical_and(live, f == 0))
    def _():
        wait_gather()
        for c in range(nb):
            lo, hi = _unpack_bf16_halves(_load_token_tile_block(xbuf, c, tm, nb))
            xb[:, c * LANES:(c + 1) * LANES] = lo.astype(BF16)
            xb[:, half + c * LANES:half + (c + 1) * LANES] = hi.astype(BF16)
        acc[...] = jnp.broadcast_to(bd_ref[...], acc.shape)

    @pl.when(live)
    def _():
        row0 = f * chunk
        for i in range(chunk):
            gather_copy(row0 + i).start()
            scatter_copy(row0 + i).start()
        x = xb[...]
        gate = _dot(x, wg_ref[...].astype(BF16)) + bg_ref[...]
        up = _dot(x, wu_ref[...].astype(BF16)) + bu_ref[...]
        x_glu = jnp.minimum(gate, SWIGLU_LIMIT)
        x_lin = jnp.clip(up, -SWIGLU_LIMIT, SWIGLU_LIMIT)
        act = x_glu * _sigmoid(SWIGLU_ALPHA * x_glu) * (x_lin + 1.0)
        acc[...] += _dot(act.astype(BF16), wd_ref[...].astype(BF16))

    @pl.when(jnp.logical_and(live, f == nf - 1))
    def _():
        wait_scatter()
        _store_token_tiles(pbuf, None, _pack_bf16_halves(acc[...]))

    @pl.when(jnp.logical_and(g == n_used + 1, f == 0))
    def _():
        wait_gather()
        lax.fori_loop(0, tm, lambda r, c: (scatter_copy(r).start(), c)[1], 0)
        wait_scatter()


def _moe_ffn(h_all, src_rows, dst_rows, block_e, n_used, w_gu, b_gu, w_dn, b_dn, layer, n_out_rows):
    n_grid, tm = src_rows.shape
    d = w_dn.shape[3]
    half = d // 2
    nb = half // LANES
    d_exp = w_dn.shape[2]
    tf = _tile(d_exp, 256)
    nf = d_exp // tf
    assert tm % nf == 0 and h_all.shape == (h_all.shape[0], LANES) and h_all.shape[0] >= tm * nb and n_out_rows >= tm
    n_exp = w_gu.shape[1]
    b_gu4 = b_gu.reshape(b_gu.shape[0], n_exp, 1, 2 * d_exp)
    b_dn4 = b_dn.reshape(b_dn.shape[0], n_exp, 1, d)

    def f_idx(g, f, nu):
        return jnp.where(jnp.logical_and(g >= 1, g <= nu[0]), f, nf - 1)

    def e_idx(g, be, nu):
        return be[jnp.clip(g - 1, 0, nu[0] - 1)]

    grid_spec = pltpu.PrefetchScalarGridSpec(
        num_scalar_prefetch=2,
        grid=(n_grid, nf),
        in_specs=[
            pl.BlockSpec((None, 1, tm), lambda g, f, be, nu: (g, 0, 0), memory_space=pltpu.SMEM),
            pl.BlockSpec((None, 1, tm), lambda g, f, be, nu: (jnp.maximum(g - 2, 0), 0, 0), memory_space=pltpu.SMEM),
            pl.BlockSpec(memory_space=pl.ANY),
            pl.BlockSpec((None, None, d, tf), lambda g, f, be, nu: (layer, e_idx(g, be, nu), 0, f_idx(g, f, nu))),
            pl.BlockSpec((None, None, d, tf), lambda g, f, be, nu: (layer, e_idx(g, be, nu), 0, nf + f_idx(g, f, nu))),
            pl.BlockSpec((None, None, 1, tf), lambda g, f, be, nu: (layer, e_idx(g, be, nu), 0, f_idx(g, f, nu))),
            pl.BlockSpec((None, None, 1, tf), lambda g, f, be, nu: (layer, e_idx(g, be, nu), 0, nf + f_idx(g, f, nu))),
            pl.BlockSpec((None, None, tf, d), lambda g, f, be, nu: (layer, e_idx(g, be, nu), f_idx(g, f, nu), 0)),
            pl.BlockSpec((None, None, 1, d), lambda g, f, be, nu: (layer, e_idx(g, be, nu), 0, 0)),
        ],
        out_specs=pl.BlockSpec(memory_space=pl.ANY),
        scratch_shapes=[
            pltpu.VMEM((tm * nb, LANES), jnp.uint32),
            pltpu.VMEM((tm, d), BF16),
            pltpu.VMEM((tm, d), F32),
            pltpu.VMEM((tm * nb, LANES), jnp.uint32),
            pltpu.SemaphoreType.DMA(()),
            pltpu.SemaphoreType.DMA(()),
        ],
    )
    return pl.pallas_call(
        _moe_kernel,
        grid_spec=grid_spec,
        out_shape=jax.ShapeDtypeStruct((n_out_rows * nb, LANES), jnp.uint32),
        compiler_params=_params(("arbitrary", "arbitrary")),
        name="moe_expert_ffn",
    )(block_e, n_used, src_rows.reshape(n_grid, 1, tm), dst_rows.reshape(n_grid, 1, tm), h_all,
      w_gu, w_gu, b_gu4, b_gu4, w_dn, b_dn4)


def _route(top_idx, n_exp, tm, n_blocks):
    n_tok, top_k = top_idx.shape
    n_rows = n_tok * top_k
    flat_e = top_idx.reshape(-1)
    onehot = (flat_e[:, None] == jnp.arange(n_exp, dtype=jnp.int32)[None, :]).astype(jnp.int32)
    csum = jnp.cumsum(onehot, axis=0)
    rank = jnp.sum(csum * onehot, axis=1) - 1
    counts = csum[-1]
    padded = (counts + tm - 1) // tm * tm
    pad_end = jnp.cumsum(padded)
    pad_start = pad_end - padded
    dest = pad_start[flat_e] + rank
    n_used = (pad_end[-1] // tm).astype(jnp.int32).reshape(1)
    block_start = jnp.arange(n_blocks, dtype=jnp.int32) * tm
    block_e = jnp.minimum(jnp.sum((pad_end[None, :] <= block_start[:, None]).astype(jnp.int32), axis=1), n_exp - 1)
    n_pos = (n_blocks + 2) * tm
    row_of_pos = jnp.full((n_pos,), -1, jnp.int32).at[dest].set(jnp.arange(n_rows, dtype=jnp.int32))
    pos = jnp.arange(n_pos, dtype=jnp.int32)
    src_rows = jnp.where(row_of_pos >= 0, row_of_pos // top_k, 0).reshape(n_blocks + 2, tm)
    dst_rows = jnp.where(row_of_pos >= 0, (row_of_pos % top_k) * n_tok + row_of_pos // top_k,
                         n_rows + pos).reshape(n_blocks + 2, tm)
    return src_rows, dst_rows, block_e, n_used, n_rows + n_pos


def _ffn_post_kernel(x1_ref, *refs):
    row_refs, (gate_ref, mod_ref, g_ref, o_ref) = refs[:TOP_K], refs[TOP_K:]
    gates = gate_ref[0]
    tm, d = x1_ref.shape[1], x1_ref.shape[2]
    half = d // 2
    nb = half // LANES
    cols = lambda c, hi: slice(hi * half + c * LANES, hi * half + (c + 1) * LANES)
    blocks = {}
    ssq = jnp.zeros((tm, 1), F32)
    for c in range(nb):
        acc_lo = acc_hi = None
        for k in range(TOP_K):
            lo, hi = _unpack_bf16_halves(_load_token_tile_block(row_refs[k], c, tm, nb))
            gk = gates[:, k:k + 1]
            acc_lo = gk * lo if k == 0 else acc_lo + gk * lo
            acc_hi = gk * hi if k == 0 else acc_hi + gk * hi
        blocks[c, 0], blocks[c, 1] = acc_lo, acc_hi
        ssq = ssq + jnp.sum(acc_lo * acc_lo, axis=-1, keepdims=True) + jnp.sum(acc_hi * acc_hi, axis=-1, keepdims=True)
    inv = lax.rsqrt(ssq / d + EPS)
    gt = mod_ref[0][5:6]
    g = g_ref[...]
    for (c, hi), ffn in blocks.items():
        sl = cols(c, hi)
        o_ref[0, :, sl] = x1_ref[0, :, sl] + gt[:, sl] * (ffn * inv * g[:, sl])


def _ffn_post(x1, rows, gates, mod, g, tok_offset, n_tok):
    bsz, seq, d = x1.shape
    tm = _tile(seq, 256)
    assert tok_offset % tm == 0 and n_tok % tm == 0
    per_b = seq // tm
    nb = d // 2 // LANES
    row = pl.BlockSpec((1, tm, d), lambda b, i: (b, i, 0))

    def choice_spec(k):
        blk0 = (k * n_tok + tok_offset) // tm
        return pl.BlockSpec((tm * nb, LANES), lambda b, i: (blk0 + b * per_b + i, 0))

    return pl.pallas_call(
        _ffn_post_kernel,
        grid=(bsz, per_b),
        in_specs=[row] + [choice_spec(k) for k in range(TOP_K)] + [
            pl.BlockSpec((1, tm, LANES), lambda b, i: (b, i, 0)),
            pl.BlockSpec((1, 6, d), lambda b, i: (b, 0, 0)),
            pl.BlockSpec((1, d), lambda b, i: (0, 0)),
        ],
        out_specs=row,
        out_shape=jax.ShapeDtypeStruct((bsz, seq, d), F32),
        compiler_params=_params(("parallel", "parallel")),
        name="ffn_post",
    )(x1, *([rows] * TOP_K), gates, mod, g)


def _rot_half_cols(w, head_dim):
    d = w.shape[0]
    w3 = w.reshape(d, -1, head_dim)
    half = head_dim // 2
    return jnp.concatenate([-w3[..., half:], w3[..., :half]], axis=-1).reshape(d, -1)


def _rope_tables(pos, head_dim):
    half = head_dim // 2
    inv = ROPE_BASE ** (-jnp.arange(half, dtype=F32) / half)
    ang = pos.astype(F32)[:, None] * inv[None, :]
    cos = jnp.cos(ang)
    sin = jnp.sin(ang)
    return jnp.concatenate([cos, cos], axis=1), jnp.concatenate([sin, sin], axis=1)


class _Dims:
    def __init__(self, state_ret, cache_ckv, cache_kpe, lru_w_a, w_uk, w_uv, conv_w, w_router, w_down):
        _, _, self.ret_heads, self.ret_dk, self.ret_dv = state_ret.shape
        self.kv_lora = cache_ckv.shape[-1]
        self.qk_rope = cache_kpe.shape[-1]
        self.mix_a = conv_w.shape[-1]
        self.conv_w = conv_w.shape[1]
        self.mla_heads, self.qk_nope = w_uk.shape[2], w_uk.shape[3]
        self.v_head = w_uv.shape[3]
        self.n_exp = w_router.shape[-1]
        self.mix_b = self.ret_heads * self.ret_dv
        self.ret_qk = self.ret_heads * self.ret_dk


def _layer_weights(l, dm, w_in, conv_w, conv_b, lru_w_a, lru_b_a, lru_w_x, lru_b_x, lru_lam, g_q_norm, w_uq,
                   g_kv_norm, w_uk, w_uv, w_out, w_router, b_router, g_mix_pre, g_mix_post, g_ffn_pre, g_ffn_post):
    d = w_in.shape[1]
    q_lora = g_q_norm.shape[1]
    sizes = (dm.mix_a, dm.mix_a, dm.ret_qk, dm.ret_qk, dm.mix_b, dm.mix_b, q_lora, dm.kv_lora, dm.qk_rope)
    offs = np.concatenate([[0], np.cumsum(sizes)])
    wl = w_in[l]
    seg = [wl[:, offs[i]:offs[i + 1]] for i in range(len(sizes))]
    w_xa, w_ya, w_qr, w_kr, w_vr, w_gr, w_ql, w_kvl, w_kpe = seg
    blk = dm.mix_b
    kv_cols = dm.kv_lora + 2 * dm.qk_rope
    kv_pad = (-kv_cols) % blk
    cols = [w_xa, w_ya, jnp.concatenate([w_qr, w_kr], axis=1), w_vr, w_gr, w_ql,
            jnp.concatenate([w_kvl, w_kpe, _rot_half_cols(w_kpe, dm.qk_rope), jnp.zeros((d, kv_pad), F32)], axis=1),
            jnp.concatenate([_rot_half_cols(w_qr, dm.ret_dk), _rot_half_cols(w_kr, dm.ret_dk)], axis=1)]
    assert all(c.shape[1] == blk for c in cols), [c.shape for c in cols]
    w_ext = jnp.concatenate(cols, axis=1).astype(BF16)

    n_lru = lru_w_a.shape[1]
    eye = jnp.eye(n_lru, dtype=F32)
    block_diag = lambda w: jnp.einsum("nde,nm->ndme", w, eye).reshape(dm.mix_a, dm.mix_a).astype(BF16)

    qk_dim = dm.qk_nope + dm.qk_rope
    wq3 = w_uq[l].reshape(q_lora, dm.mla_heads, qk_dim)
    pe = wq3[..., dm.qk_nope:]
    half = dm.qk_rope // 2
    pe_rot = jnp.concatenate([-pe[..., half:], pe[..., :half]], axis=-1)
    w_uq_ext = jnp.concatenate([wq3, pe_rot], axis=-1).transpose(1, 0, 2).astype(BF16)

    wo = w_out[l].astype(BF16)
    n_exp = dm.n_exp
    wr = jnp.concatenate([w_router[l], jnp.zeros((d, LANES - n_exp), F32)], axis=1)
    wr_hi = wr.astype(BF16)
    wr_lo = (wr - wr_hi.astype(F32)).astype(BF16)
    br = jnp.concatenate([b_router[l], jnp.full((LANES - n_exp,), NEG_BIG, F32)]).reshape(1, LANES)
    row = lambda v: v[l].reshape(1, -1)
    return dict(
        w_ext=w_ext, conv_w=conv_w[l], conv_b=row(conv_b),
        wa_bd=block_diag(lru_w_a[l]), wx_bd=block_diag(lru_w_x[l]),
        ba=row(lru_b_a), bx=row(lru_b_x), lam=row(lru_lam),
        g_q=row(g_q_norm), w_uq_ext=w_uq_ext, g_kv=row(g_kv_norm),
        w_uk=w_uk[l].transpose(1, 0, 2).astype(BF16), w_uv=w_uv[l].transpose(1, 0, 2).astype(BF16),
        wo_a=wo[:dm.mix_a], wo_b=wo[dm.mix_a:dm.mix_a + dm.mix_b], wo_c=wo[dm.mix_a + dm.mix_b:],
        wr_hi=wr_hi, wr_lo=wr_lo, br=br,
        g_mix_pre=row(g_mix_pre), g_mix_post=row(g_mix_post), g_ffn_pre=row(g_ffn_pre), g_ffn_post=row(g_ffn_post),
    )


def _mixer(x, mod, wl, dm, conv_buf, h0, s0, past_ckv, past_kpe):
    bsz, seq, d = x.shape
    past_len = 0 if past_ckv is None else past_ckv.shape[1]
    pos = past_len + jnp.arange(seq, dtype=jnp.int32)
    u = _in_proj(x, mod, wl["g_mix_pre"], wl["w_ext"])

    ctx_rows = 8
    n_ctx = dm.conv_w - 1
    conv_ctx = jnp.concatenate([jnp.zeros((bsz, ctx_rows - n_ctx, dm.mix_a), F32), conv_buf], axis=1)
    a_out, conv_new, h_new = _rg_lru(u, conv_ctx, h0.reshape(bsz, 1, dm.mix_a), wl["conv_w"], wl["conv_b"],
                                     wl["wa_bd"], wl["wx_bd"], wl["ba"], wl["bx"], wl["lam"])

    cos_k, sin_k = _rope_tables(pos, dm.ret_dk)
    k_scale = float(dm.ret_dk) ** -0.5
    scale_row = jnp.concatenate([jnp.ones((dm.ret_qk,), F32), jnp.full((dm.ret_qk,), k_scale, F32)])[None, :]
    cos_t = jnp.tile(cos_k, (1, 2 * dm.ret_heads)) * scale_row
    sin_t = jnp.tile(sin_k, (1, 2 * dm.ret_heads)) * scale_row
    b_out, s_new = _retention(u, s0, cos_t, sin_t, blocks=(2, 3, 4, 7))

    cos_r, sin_r = _rope_tables(pos, dm.qk_rope)
    lane_pad = lambda t, left, right, fill: jnp.concatenate(
        [jnp.full((seq, left), fill, F32), t, jnp.zeros((seq, right), F32)], axis=1)
    ckv_new, kpe_new = _latent(u, wl["g_kv"], lane_pad(cos_r, 0, LANES - dm.qk_rope, 0.0),
                               lane_pad(sin_r, 0, LANES - dm.qk_rope, 0.0), kv_blk=6, blk_w=dm.mix_b, rope_w=dm.qk_rope)
    q = _q_proj(u, wl["g_q"], wl["w_uq_ext"], lane_pad(cos_r, dm.qk_nope, dm.qk_rope, 1.0),
                lane_pad(sin_r, dm.qk_nope, dm.qk_rope, 0.0), ql_blk=5, rope_w=dm.qk_rope)
    if past_ckv is None:
        ckv_all, kpe_all = ckv_new, kpe_new
    else:
        past_kpe_pad = jnp.concatenate([past_kpe, jnp.zeros(past_kpe.shape[:2] + (LANES - dm.qk_rope,), F32)], axis=2)
        ckv_all = jnp.concatenate([past_ckv, ckv_new], axis=1)
        kpe_all = jnp.concatenate([past_kpe_pad, kpe_new], axis=1)
    k, v = _kv_proj(ckv_all, kpe_all, wl["w_uk"], wl["w_uv"])
    c_out = _attention(q, k, v, past_len, dm.qk_nope + dm.qk_rope)

    x1, h2, gates, idx = _out_proj(a_out, b_out, c_out, x, mod, wl["g_mix_post"], wl["g_ffn_pre"],
                                   wl["wo_a"], wl["wo_b"], wl["wo_c"], wl["wr_hi"], wl["wr_lo"], wl["br"])
    states = (ckv_new, kpe_new[:, :, :dm.qk_rope], conv_new[:, ctx_rows - n_ctx:], h_new.reshape(bsz, dm.mix_a), s_new)
    return x1, h2, gates, idx[:, :, :TOP_K], states


def kernel(x_prompt, x_sample, c_prompt, c_sample, cache_ckv, cache_kpe, state_conv, state_lru, state_ret, w_ada, b_ada, g_mix_pre, g_mix_post, g_ffn_pre, g_ffn_post, w_in, conv_w, conv_b, lru_w_a, lru_b_a, lru_w_x, lru_b_x, lru_lam, g_q_norm, w_uq, g_kv_norm, w_uk, w_uv, w_out, w_router, b_router, w_gate_up, b_gate_up, w_down, b_down):
    depth = w_in.shape[0]
    dm = _Dims(state_ret, cache_ckv, cache_kpe, lru_w_a, w_uk, w_uv, conv_w, w_router, w_down)
    bp, lp, d = x_prompt.shape
    bs, ls, _ = x_sample.shape
    n_tok = bp * lp + bs * ls
    moe_tm = min(1024, max(16, 1 << int(np.log2((n_tok * TOP_K) // dm.n_exp))))
    n_blocks = -(-(n_tok * TOP_K) // moe_tm) + dm.n_exp

    xp, xs = x_prompt, x_sample
    p_states, s_states = [], []
    for l in range(depth):
        wl = _layer_weights(l, dm, w_in, conv_w, conv_b, lru_w_a, lru_b_a, lru_w_x, lru_b_x, lru_lam, g_q_norm, w_uq,
                            g_kv_norm, w_uk, w_uv, w_out, w_router, b_router, g_mix_pre, g_mix_post, g_ffn_pre, g_ffn_post)
        mod_p = _ada_mod(c_prompt, w_ada, b_ada, l).reshape(bp, 6, d)
        mod_s = _ada_mod(c_sample, w_ada, b_ada, l).reshape(bs, 6, d)
        zeros = lambda *s: jnp.zeros(s, F32)
        x1p, h2p, gp, ip, st_p = _mixer(
            xp, mod_p, wl, dm, zeros(bp, dm.conv_w - 1, dm.mix_a), zeros(bp, dm.mix_a),
            zeros(bp, dm.ret_heads, dm.ret_dk, dm.ret_dv), None, None)
        x1s, h2s, gs, is_, st_s = _mixer(
            xs, mod_s, wl, dm, state_conv[l], state_lru[l], state_ret[l], cache_ckv[l], cache_kpe[l])
        p_states.append(st_p)
        s_states.append(st_s)

        h2 = jnp.concatenate([h2p.reshape(-1, LANES), h2s.reshape(-1, LANES)], axis=0)
        top_idx = jnp.concatenate([ip.reshape(-1, TOP_K), is_.reshape(-1, TOP_K)], axis=0)
        src_rows, dst_rows, block_e, n_used, n_out_rows = _route(top_idx, dm.n_exp, moe_tm, n_blocks)
        rows = _moe_ffn(h2, src_rows, dst_rows, block_e, n_used, w_gate_up, b_gate_up, w_down, b_down, l, n_out_rows)
        xp = _ffn_post(x1p, rows, gp, mod_p, wl["g_ffn_post"], 0, n_tok)
        xs = _ffn_post(x1s, rows, gs, mod_s, wl["g_ffn_post"], bp * lp, n_tok)

    stack = lambda sts: tuple(jnp.stack(t, axis=0) for t in zip(*sts))
    p_ckv, p_kpe, p_conv, p_lru, p_ret = stack(p_states)
    s_ckv, s_kpe, s_conv, s_lru, s_ret = stack(s_states)
    return (xp, xs, p_ckv, p_kpe, p_conv, p_lru, p_ret, s_ckv, s_kpe, s_conv, s_lru, s_ret)
```

```python
import functools

import numpy as np
import jax
import jax.numpy as jnp
from jax import lax
from jax.experimental import pallas as pl
from jax.experimental.pallas import tpu as pltpu

CHUNK = 64
EPS = 1e-6
ROPE_BASE = 10000.0
RG_C = 8.0
TOP_K = 4
SWIGLU_LIMIT = 7.0
SWIGLU_ALPHA = 1.702
NEG_BIG = -1e30
GELU_C = float(np.sqrt(2.0 / np.pi))

LANES = 128
VMEM_LIMIT_BYTES = 56 * 1024 * 1024

F32 = jnp.float32
BF16 = jnp.bfloat16


def _params(semantics):
    return pltpu.CompilerParams(dimension_semantics=semantics, vmem_limit_bytes=VMEM_LIMIT_BYTES)


def _tile(n, pref):
    if n <= pref:
        return n
    t = pref
    while n % t:
        t //= 2
    return t


def _rms(x, g):
    return x * lax.rsqrt(jnp.mean(x * x, axis=-1, keepdims=True) + EPS) * g


def _dot(a, b):
    return jnp.dot(a, b, preferred_element_type=F32)


def _dot_nt(a, b):
    return lax.dot_general(a, b, (((1,), (1,)), ((), ())), preferred_element_type=F32)


def _dot_tn(a, b):
    return lax.dot_general(a, b, (((0,), (0,)), ((), ())), preferred_element_type=F32)


def _sigmoid(x):
    return 1.0 / (1.0 + jnp.exp(-x))


def _pack_bf16_halves(x):
    n = x.shape[1] // 2
    bits = lax.bitcast_convert_type(x.astype(BF16).astype(F32), jnp.uint32)
    return (bits[:, :n] >> 16) | (bits[:, n:] & jnp.uint32(0xFFFF0000))


def _unpack_bf16_halves(w):
    lo = lax.bitcast_convert_type(w << 16, F32)
    hi = lax.bitcast_convert_type(w & jnp.uint32(0xFFFF0000), F32)
    return lo, hi


def _store_token_tiles(ref, lead, packed):
    tm, n = packed.shape
    nb = n // LANES
    for c in range(nb):
        idx = (pl.ds(c, tm, stride=nb), slice(None))
        ref[lead + idx if lead else idx] = packed[:, c * LANES:(c + 1) * LANES]


def _load_token_tile_block(ref, c, tm, nb):
    return ref[pl.ds(c, tm, stride=nb), :]


def _ada_kernel(c_ref, w_ref, b_ref, o_ref):
    c = c_ref[...]
    s = (c * _sigmoid(c)).astype(BF16)
    o_ref[...] = _dot(s, w_ref[...].astype(BF16)) + b_ref[...]


def _ada_mod(c, w_ada, b_ada, layer):
    bsz, d = c.shape
    n = w_ada.shape[-1]
    tn = _tile(n, 1024)
    return pl.pallas_call(
        _ada_kernel,
        grid=(n // tn,),
        in_specs=[
            pl.BlockSpec((bsz, d), lambda j: (0, 0)),
            pl.BlockSpec((None, d, tn), lambda j: (layer, 0, j)),
            pl.BlockSpec((None, 1, tn), lambda j: (layer, 0, j)),
        ],
        out_specs=pl.BlockSpec((bsz, tn), lambda j: (0, j)),
        out_shape=jax.ShapeDtypeStruct((bsz, n), F32),
        compiler_params=_params(("arbitrary",)),
        name="ada_mod",
    )(c, w_ada, b_ada.reshape(b_ada.shape[0], 1, n))


def _in_proj_kernel(x_ref, mod_ref, g_ref, w_ref, u_ref):
    mod = mod_ref[0]
    h = _rms(x_ref[0], g_ref[...]) * (1.0 + mod[1:2]) + mod[0:1]
    u_ref[0] = _dot(h.astype(BF16), w_ref[...])


def _in_proj(x, mod, g, w_ext):
    bsz, seq, d = x.shape
    n = w_ext.shape[1]
    tm = _tile(seq, 256)
    return pl.pallas_call(
        _in_proj_kernel,
        grid=(bsz, seq // tm),
        in_specs=[
            pl.BlockSpec((1, tm, d), lambda b, i: (b, i, 0)),
            pl.BlockSpec((1, 6, d), lambda b, i: (b, 0, 0)),
            pl.BlockSpec((1, d), lambda b, i: (0, 0)),
            pl.BlockSpec((d, n), lambda b, i: (0, 0)),
        ],
        out_specs=pl.BlockSpec((1, tm, n), lambda b, i: (b, i, 0)),
        out_shape=jax.ShapeDtypeStruct((bsz, seq, n), F32),
        compiler_params=_params(("parallel", "parallel")),
        name="in_proj",
    )(x, mod, g, w_ext)


def _lru_kernel(xa_ref, ya_ref, cbuf_ref, h0_ref, cw_ref, cb_ref, wa_ref, wx_ref, ba_ref, bx_ref,
                lam_ref, out_ref, cnew_ref, hlast_ref, xbuf, hcar):
    tl = xa_ref.shape[1]
    width = xa_ref.shape[2]
    ctx = xbuf.shape[0] - tl

    @pl.when(pl.program_id(1) == 0)
    def _():
        xbuf[0:ctx, :] = cbuf_ref[0]
        hcar[...] = h0_ref[0]

    xa = xa_ref[0]
    xbuf[ctx:ctx + tl, :] = xa
    cw = cw_ref[...]
    n_tap = cw.shape[0]
    xc = cb_ref[...] + cw[n_tap - 1:n_tap] * xa
    for k in range(n_tap - 1):
        off = ctx - (n_tap - 1) + k
        xc = xc + cw[k:k + 1] * xbuf[off:off + tl, :]
    new_ctx = xbuf[tl:tl + ctx, :]
    xbuf[0:ctx, :] = new_ctx
    cnew_ref[0] = new_ctx

    xcb = xc.astype(BF16)
    r = _sigmoid(_dot(xcb, wa_ref[...]) + ba_ref[...])
    gi = _sigmoid(_dot(xcb, wx_ref[...]) + bx_ref[...])
    z = -lam_ref[...]
    softplus = jnp.maximum(z, 0.0) + jnp.log1p(jnp.exp(-jnp.abs(z)))
    log_a = (-RG_C) * r * softplus
    a = jnp.exp(log_a)
    b = jnp.sqrt(1.0 - jnp.exp(2.0 * log_a)) * gi * xc

    row = lax.broadcasted_iota(jnp.int32, (tl, width), 0)
    s = 1
    while s < tl:
        a_sh = pltpu.roll(a, s, 0)
        b_sh = pltpu.roll(b, s, 0)
        valid = row >= s
        b = jnp.where(valid, a * b_sh + b, b)
        a = jnp.where(valid, a * a_sh, a)
        s *= 2
    h = a * hcar[...] + b
    h_last = h[tl - 1:tl, :]
    hcar[...] = h_last
    hlast_ref[0] = h_last

    ya = ya_ref[0]
    gelu = 0.5 * ya * (1.0 + jnp.tanh(GELU_C * (ya + 0.044715 * (ya * ya * ya))))
    out_ref[0] = (h * gelu).astype(out_ref.dtype)


def _rg_lru(u, conv_ctx, h0, cw, cb, wa_bd, wx_bd, ba, bx, lam):
    bsz, seq, _ = u.shape
    width = cw.shape[1]
    tl = _tile(seq, 512)
    ctx = conv_ctx.shape[1]
    xa_blk = 0
    ya_blk = 1
    vec = lambda: pl.BlockSpec((1, width), lambda b, t: (0, 0))
    return pl.pallas_call(
        _lru_kernel,
        grid=(bsz, seq // tl),
        in_specs=[
            pl.BlockSpec((1, tl, width), lambda b, t: (b, t, xa_blk)),
            pl.BlockSpec((1, tl, width), lambda b, t: (b, t, ya_blk)),
            pl.BlockSpec((1, ctx, width), lambda b, t: (b, 0, 0)),
            pl.BlockSpec((1, 1, width), lambda b, t: (b, 0, 0)),
            pl.BlockSpec(cw.shape, lambda b, t: (0, 0)),
            vec(),
            pl.BlockSpec((width, width), lambda b, t: (0, 0)),
            pl.BlockSpec((width, width), lambda b, t: (0, 0)),
            vec(), vec(), vec(),
        ],
        out_specs=[
            pl.BlockSpec((1, tl, width), lambda b, t: (b, t, 0)),
            pl.BlockSpec((1, ctx, width), lambda b, t: (b, 0, 0)),
            pl.BlockSpec((1, 1, width), lambda b, t: (b, 0, 0)),
        ],
        out_shape=[
            jax.ShapeDtypeStruct((bsz, seq, width), BF16),
            jax.ShapeDtypeStruct((bsz, ctx, width), F32),
            jax.ShapeDtypeStruct((bsz, 1, width), F32),
        ],
        scratch_shapes=[pltpu.VMEM((tl + ctx, width), F32), pltpu.VMEM((1, width), F32)],
        compiler_params=_params(("parallel", "arbitrary")),
        name="rg_lru",
    )(u, u, conv_ctx, h0, cw, cb, wa_bd, wx_bd, ba, bx, lam)


def _ret_kernel(qk_ref, rot_ref, v_ref, g_ref, cos_ref, sin_ref, dmask_ref, qdec_ref, kdec_ref,
                s0_ref, out_ref, slast_ref, s_sc, *, gchunk):
    n_heads, dk, dv = s_sc.shape
    hdk = n_heads * dk

    @pl.when(pl.program_id(1) == 0)
    def _():
        s_sc[...] = s0_ref[0]

    qkr = qk_ref[0] * cos_ref[...] + rot_ref[0] * sin_ref[...]
    v = v_ref[0]
    vdec = v * kdec_ref[...]
    gate = g_ref[0]
    qdec = qdec_ref[...]
    for h in range(n_heads):
        q = qkr[:, h * dk:(h + 1) * dk].astype(BF16)
        k = qkr[:, hdk + h * dk:hdk + (h + 1) * dk].astype(BF16)
        vh = v[:, h * dv:(h + 1) * dv].astype(BF16)
        vd = vdec[:, h * dv:(h + 1) * dv].astype(BF16)
        s_prev = s_sc[h]
        scores = _dot_nt(q, k) * dmask_ref[h]
        o = _dot(scores.astype(BF16), vh) + qdec[:, h * dv:(h + 1) * dv] * _dot(q, s_prev.astype(BF16))
        s_sc[h] = s_prev * gchunk[h] + _dot_tn(k, vd)
        o = o * lax.rsqrt(jnp.mean(o * o, axis=-1, keepdims=True) + EPS)
        gh = gate[:, h * dv:(h + 1) * dv]
        out_ref[0, :, h * dv:(h + 1) * dv] = (o * (gh * _sigmoid(gh))).astype(out_ref.dtype)
    slast_ref[0] = s_sc[...]


def _retention(u, s0, cos_t, sin_t, blocks):
    bsz, seq, _ = u.shape
    _, n_heads, dk, dv = s0.shape
    width = n_heads * dv
    tc = _tile(seq, 256)
    log_g = np.log1p(-np.exp2(-5.0 - np.arange(n_heads, dtype=np.float64)))
    idx = np.arange(tc, dtype=np.float64)
    diff = idx[:, None] - idx[None, :]
    dmask = np.where(diff[None] >= 0, np.exp(np.maximum(diff, 0.0)[None] * log_g[:, None, None]), 0.0)
    qdec = np.repeat(np.exp((idx + 1.0)[:, None] * log_g[None, :]), dv, axis=1)
    kdec = np.repeat(np.exp((tc - 1.0 - idx)[:, None] * log_g[None, :]), dv, axis=1)
    gchunk = tuple(float(g) for g in np.exp(tc * log_g))
    qk_blk, v_blk, g_blk, rot_blk = blocks
    u_spec = lambda blk: pl.BlockSpec((1, tc, width), lambda b, n: (b, n, blk))
    full2 = lambda a: pl.BlockSpec(a.shape, lambda b, n: (0, 0))
    return pl.pallas_call(
        functools.partial(_ret_kernel, gchunk=gchunk),
        grid=(bsz, seq // tc),
        in_specs=[
            u_spec(qk_blk), u_spec(rot_blk), u_spec(v_blk), u_spec(g_blk),
            pl.BlockSpec((tc, width), lambda b, n: (n, 0)),
            pl.BlockSpec((tc, width), lambda b, n: (n, 0)),
            pl.BlockSpec(dmask.shape, lambda b, n: (0, 0, 0)),
            full2(qdec), full2(kdec),
            pl.BlockSpec((1, n_heads, dk, dv), lambda b, n: (b, 0, 0, 0)),
        ],
        out_specs=[
            pl.BlockSpec((1, tc, width), lambda b, n: (b, n, 0)),
            pl.BlockSpec((1, n_heads, dk, dv), lambda b, n: (b, 0, 0, 0)),
        ],
        out_shape=[
            jax.ShapeDtypeStruct((bsz, seq, width), BF16),
            jax.ShapeDtypeStruct((bsz, n_heads, dk, dv), F32),
        ],
        scratch_shapes=[pltpu.VMEM((n_heads, dk, dv), F32)],
        compiler_params=_params(("parallel", "arbitrary")),
        name="retention",
    )(u, u, u, u, cos_t, sin_t, jnp.asarray(dmask, F32), jnp.asarray(qdec, F32),
      jnp.asarray(kdec, F32), s0)


def _latent_kernel(kv_ref, g_ref, cos_ref, sin_ref, ckv_ref, kpe_ref, *, rope_w):
    lora = ckv_ref.shape[2]
    blk = kv_ref[0]
    ckv_ref[0] = _rms(blk[:, :lora], g_ref[...])
    kp = blk[:, lora:lora + LANES]
    rolled = pltpu.roll(kp, LANES - rope_w, 1)
    kpe_ref[0] = kp * cos_ref[...] + rolled * sin_ref[...]


def _latent(u, g_kv, cos_p, sin_p, kv_blk, blk_w, rope_w):
    bsz, seq, _ = u.shape
    lora = g_kv.shape[1]
    tm = _tile(seq, 512)
    return pl.pallas_call(
        functools.partial(_latent_kernel, rope_w=rope_w),
        grid=(bsz, seq // tm),
        in_specs=[
            pl.BlockSpec((1, tm, blk_w), lambda b, i: (b, i, kv_blk)),
            pl.BlockSpec((1, lora), lambda b, i: (0, 0)),
            pl.BlockSpec((tm, LANES), lambda b, i: (i, 0)),
            pl.BlockSpec((tm, LANES), lambda b, i: (i, 0)),
        ],
        out_specs=[
            pl.BlockSpec((1, tm, lora), lambda b, i: (b, i, 0)),
            pl.BlockSpec((1, tm, LANES), lambda b, i: (b, i, 0)),
        ],
        out_shape=[
            jax.ShapeDtypeStruct((bsz, seq, lora), F32),
            jax.ShapeDtypeStruct((bsz, seq, LANES), F32),
        ],
        compiler_params=_params(("parallel", "parallel")),
        name="latent_norm_rope",
    )(u, g_kv, cos_p, sin_p)


def _q_proj_kernel(ql_ref, g_ref, w_ref, cos_ref, sin_ref, q_ref, *, rope_w):
    cq = _rms(ql_ref[0], g_ref[...]).astype(BF16)
    cos = cos_ref[...]
    sin = sin_ref[...]
    for h in range(w_ref.shape[0]):
        r = _dot(cq, w_ref[h])
        rolled = pltpu.roll(r, r.shape[1] - rope_w, 1)
        q_ref[0, h] = (r * cos + rolled * sin).astype(q_ref.dtype)


def _q_proj(u, g_q, w_uq_ext, cos_q, sin_q, ql_blk, rope_w):
    bsz, seq, _ = u.shape
    n_heads, q_lora, width = w_uq_ext.shape
    tm = _tile(seq, 512)
    return pl.pallas_call(
        functools.partial(_q_proj_kernel, rope_w=rope_w),
        grid=(bsz, seq // tm),
        in_specs=[
            pl.BlockSpec((1, tm, q_lora), lambda b, i: (b, i, ql_blk)),
            pl.BlockSpec((1, q_lora), lambda b, i: (0, 0)),
            pl.BlockSpec((n_heads, q_lora, width), lambda b, i: (0, 0, 0)),
            pl.BlockSpec((tm, width), lambda b, i: (i, 0)),
            pl.BlockSpec((tm, width), lambda b, i: (i, 0)),
        ],
        out_specs=pl.BlockSpec((1, n_heads, tm, width), lambda b, i: (b, 0, i, 0)),
        out_shape=jax.ShapeDtypeStruct((bsz, n_heads, seq, width), BF16),
        compiler_params=_params(("parallel", "parallel")),
        name="mla_q_proj",
    )(u, g_q, w_uq_ext, cos_q, sin_q)


def _kv_proj_kernel(ckv_ref, kpe_ref, wk_ref, wv_ref, k_ref, v_ref):
    ckv = ckv_ref[0].astype(BF16)
    kpe = kpe_ref[0].astype(k_ref.dtype)
    nope = wk_ref.shape[2]
    for h in range(wk_ref.shape[0]):
        k_ref[0, h, :, :nope] = _dot(ckv, wk_ref[h]).astype(k_ref.dtype)
        k_ref[0, h, :, nope:] = kpe
        v_ref[0, h] = _dot(ckv, wv_ref[h]).astype(v_ref.dtype)


def _kv_proj(ckv, kpe, w_uk, w_uv):
    bsz, klen, lora = ckv.shape
    n_heads, _, nope = w_uk.shape
    vdim = w_uv.shape[2]
    tm = _tile(klen, 512) if klen % 512 == 0 else klen
    return pl.pallas_call(
        _kv_proj_kernel,
        grid=(bsz, klen // tm),
        in_specs=[
            pl.BlockSpec((1, tm, lora), lambda b, i: (b, i, 0)),
            pl.BlockSpec((1, tm, LANES), lambda b, i: (b, i, 0)),
            pl.BlockSpec((n_heads, lora, nope), lambda b, i: (0, 0, 0)),
            pl.BlockSpec((n_heads, lora, vdim), lambda b, i: (0, 0, 0)),
        ],
        out_specs=[
            pl.BlockSpec((1, n_heads, tm, nope + LANES), lambda b, i: (b, 0, i, 0)),
            pl.BlockSpec((1, n_heads, tm, vdim), lambda b, i: (b, 0, i, 0)),
        ],
        out_shape=[
            jax.ShapeDtypeStruct((bsz, n_heads, klen, nope + LANES), BF16),
            jax.ShapeDtypeStruct((bsz, n_heads, klen, vdim), BF16),
        ],
        compiler_params=_params(("parallel", "parallel")),
        name="mla_kv_proj",
    )(ckv, kpe, w_uk, w_uv)


def _attn_kernel(q_ref, k_ref, v_ref, o_ref, m_sc, l_sc, acc_sc, s_sc, *, q_pos0, scale, tk):
    qi = pl.program_id(2)
    tq = q_ref.shape[2]
    nk = k_ref.shape[2] // tk
    q = q_ref[0, 0]
    m_sc[...] = jnp.full(m_sc.shape, NEG_BIG, F32)
    l_sc[...] = jnp.zeros(l_sc.shape, F32)
    acc_sc[...] = jnp.zeros(acc_sc.shape, F32)

    last_q_chunk = (q_pos0 + qi * tq + tq - 1) // CHUNK
    n_seen = jnp.minimum((last_q_chunk * CHUNK + CHUNK - 1) // tk + 1, nk)
    q_chunk = (q_pos0 + qi * tq + lax.broadcasted_iota(jnp.int32, (tq, 1), 0)) // CHUNK
    k_chunk0 = lax.broadcasted_iota(jnp.int32, (1, tk), 1) // CHUNK

    def scores(j):
        start = pl.multiple_of(j * tk, tk)
        return _dot_nt(q, k_ref[0, 0, pl.ds(start, tk), :])

    s_sc[0] = scores(0)

    def step(j, carry):
        slot = j % 2
        s = s_sc[slot]
        s_sc[1 - slot] = scores(jnp.minimum(j + 1, n_seen - 1))
        start = pl.multiple_of(j * tk, tk)
        s = jnp.where(k_chunk0 + start // CHUNK <= q_chunk, s * scale, NEG_BIG)
        m_prev = m_sc[...]
        m_new = jnp.maximum(m_prev, jnp.max(s, axis=-1, keepdims=True))
        alpha = jnp.exp(m_prev - m_new)
        p = jnp.exp(s - m_new)
        l_sc[...] = alpha * l_sc[...] + jnp.sum(p, axis=-1, keepdims=True)
        acc_sc[...] = alpha * acc_sc[...] + _dot(p.astype(BF16), v_ref[0, 0, pl.ds(start, tk), :])
        m_sc[...] = m_new
        return carry

    lax.fori_loop(0, n_seen, step, 0)
    o_ref[0] = (acc_sc[...] / l_sc[...]).astype(o_ref.dtype)


def _attention(q, k, v, q_pos0, qk_dim):
    bsz, n_heads, seq, width = q.shape
    klen = k.shape[2]
    vdim = v.shape[3]
    tq = _tile(seq, 512)
    tk = _tile(klen, 512) if klen % 512 == 0 else klen
    assert tk == klen or tk % CHUNK == 0
    return pl.pallas_call(
        functools.partial(_attn_kernel, q_pos0=q_pos0, scale=float(qk_dim) ** -0.5, tk=tk),
        grid=(bsz, n_heads, seq // tq),
        in_specs=[
            pl.BlockSpec((1, 1, tq, width), lambda b, h, qi: (b, h, qi, 0)),
            pl.BlockSpec((1, 1, klen, width), lambda b, h, qi: (b, h, 0, 0)),
            pl.BlockSpec((1, 1, klen, vdim), lambda b, h, qi: (b, h, 0, 0)),
        ],
        out_specs=pl.BlockSpec((1, tq, vdim), lambda b, h, qi: (b, qi, h)),
        out_shape=jax.ShapeDtypeStruct((bsz, seq, n_heads * vdim), BF16),
        scratch_shapes=[pltpu.VMEM((tq, 1), F32), pltpu.VMEM((tq, 1), F32), pltpu.VMEM((tq, vdim), F32),
                        pltpu.VMEM((2, tq, tk), F32)],
        compiler_params=_params(("parallel", "parallel", "arbitrary")),
        name="mla_attention",
    )(q, k, v)


def _out_proj_kernel(a_ref, b_ref, c_ref, x_ref, mod_ref, gpost_ref, gpre_ref, wa_ref, wb_ref, wc_ref,
                     wrh_ref, wrl_ref, br_ref, x1_ref, h2_ref, gate_ref, idx_ref):
    mix = _dot(a_ref[0], wa_ref[...]) + _dot(b_ref[0], wb_ref[...]) + _dot(c_ref[0], wc_ref[...])
    mod = mod_ref[0]
    x1 = x_ref[0] + mod[2:3] * _rms(mix, gpost_ref[...])
    x1_ref[0] = x1
    h2 = _rms(x1, gpre_ref[...]) * (1.0 + mod[4:5]) + mod[3:4]
    _store_token_tiles(h2_ref, (0,), _pack_bf16_halves(h2))
    hi = h2.astype(BF16)
    lo = (h2 - hi.astype(F32)).astype(BF16)
    logits = _dot(hi, wrh_ref[...]) + (_dot(hi, wrl_ref[...]) + _dot(lo, wrh_ref[...])) + br_ref[...]
    lane = lax.broadcasted_iota(jnp.int32, logits.shape, 1)
    lane_f = lane.astype(F32)
    gates = jnp.zeros(logits.shape, F32)
    idxs = jnp.zeros(logits.shape, F32)
    top0 = None
    denom = None
    for k in range(TOP_K):
        m = jnp.max(logits, axis=-1, keepdims=True)
        sel = jnp.min(jnp.where(logits == m, lane_f, float(LANES)), axis=-1, keepdims=True)
        if k == 0:
            top0 = m
            e = jnp.ones_like(m)
            denom = e
        else:
            e = jnp.exp(m - top0)
            denom = denom + e
        gates = jnp.where(lane == k, e, gates)
        idxs = jnp.where(lane == k, sel, idxs)
        logits = jnp.where(lane_f == sel, NEG_BIG * 2.0, logits)
    gate_ref[0] = gates / denom
    idx_ref[0] = idxs.astype(jnp.int32)


def _out_proj(a_out, b_out, c_out, x, mod, g_post, g_pre, wa, wb, wc, wr_hi, wr_lo, br):
    bsz, seq, d = x.shape
    tm = _tile(seq, 256)
    nb = d // 2 // LANES
    row = lambda w: pl.BlockSpec((1, tm, w), lambda b, i: (b, i, 0))
    const = lambda a: pl.BlockSpec(a.shape, lambda b, i: (0, 0))
    return pl.pallas_call(
        _out_proj_kernel,
        grid=(bsz, seq // tm),
        in_specs=[
            row(a_out.shape[2]), row(b_out.shape[2]), row(c_out.shape[2]), row(d),
            pl.BlockSpec((1, 6, d), lambda b, i: (b, 0, 0)),
            const(g_post), const(g_pre), const(wa), const(wb), const(wc),
            const(wr_hi), const(wr_lo), const(br),
        ],
        out_specs=[row(d), pl.BlockSpec((1, tm * nb, LANES), lambda b, i: (b, i, 0)), row(LANES), row(LANES)],
        out_shape=[
            jax.ShapeDtypeStruct((bsz, seq, d), F32),
            jax.ShapeDtypeStruct((bsz, seq * nb, LANES), jnp.uint32),
            jax.ShapeDtypeStruct((bsz, seq, LANES), F32),
            jax.ShapeDtypeStruct((bsz, seq, LANES), jnp.int32),
        ],
        compiler_params=_params(("parallel", "parallel")),
        name="out_proj_router",
    )(a_out, b_out, c_out, x, mod, g_post, g_pre, wa, wb, wc, wr_hi, wr_lo, br)


def _moe_kernel(be_ref, nu_ref, src_ref, dst_ref, h_hbm, wg_ref, wu_ref, bg_ref, bu_ref, wd_ref, bd_ref,
                rows_hbm, xbuf, xb, acc, pbuf, gsem, ssem):
    g = pl.program_id(0)
    f = pl.program_id(1)
    nf = pl.num_programs(1)
    n_used = nu_ref[0]
    tm, d = xb.shape
    half = d // 2
    nb = half // LANES
    chunk = tm // nf
    live = jnp.logical_and(g >= 1, g <= n_used)

    def token(ref, t):
        return ref.at[pl.ds(pl.multiple_of(t * nb, nb), nb)]

    def gather_copy(r):
        return pltpu.make_async_copy(token(h_hbm, src_ref[0, r]), token(xbuf, r), gsem)

    def scatter_copy(r):
        return pltpu.make_async_copy(token(pbuf, r), token(rows_hbm, dst_ref[0, r]), ssem)

    def wait_gather():
        pltpu.make_async_copy(h_hbm.at[pl.ds(0, tm * nb)], xbuf, gsem).wait()

    def wait_scatter():
        pltpu.make_async_copy(pbuf, rows_hbm.at[pl.ds(0, tm * nb)], ssem).wait()

    @pl.when(jnp.logical_and(g == 0, f == 0))
    def _():
        pbuf[...] = jnp.zeros(pbuf.shape, pbuf.dtype)
        lax.fori_loop(0, tm, lambda r, c: (gather_copy(r).start(), c)[1], 0)

    @pl.when(jnp.logical_and(live, f == 0))
    def _():
        wait_gather()
        for c in range(nb):
            lo, hi = _unpack_bf16_halves(_load_token_tile_block(xbuf, c, tm, nb))
            xb[:, c * LANES:(c + 1) * LANES] = lo.astype(BF16)
            xb[:, half + c * LANES:half + (c + 1) * LANES] = hi.astype(BF16)
        acc[...] = jnp.broadcast_to(bd_ref[...], acc.shape)

    @pl.when(live)
    def _():
        row0 = f * chunk
        for i in range(chunk):
            gather_copy(row0 + i).start()
            scatter_copy(row0 + i).start()
        x = xb[...]
        gate = _dot(x, wg_ref[...].astype(BF16)) + bg_ref[...]
        up = _dot(x, wu_ref[...].astype(BF16)) + bu_ref[...]
        x_glu = jnp.minimum(gate, SWIGLU_LIMIT)
        x_lin = jnp.clip(up, -SWIGLU_LIMIT, SWIGLU_LIMIT)
        act = x_glu * _sigmoid(SWIGLU_ALPHA * x_glu) * (x_lin + 1.0)
        acc[...] += _dot(act.astype(BF16), wd_ref[...].astype(BF16))

    @pl.when(jnp.logical_and(live, f == nf - 1))
    def _():
        wait_scatter()
        _store_token_tiles(pbuf, None, _pack_bf16_halves(acc[...]))

    @pl.when(jnp.logical_and(g == n_used + 1, f == 0))
    def _():
        wait_gather()
        lax.fori_loop(0, tm, lambda r, c: (scatter_copy(r).start(), c)[1], 0)
        wait_scatter()


def _moe_ffn(h_all, src_rows, dst_rows, block_e, n_used, w_gu, b_gu, w_dn, b_dn, layer, n_out_rows):
    n_grid, tm = src_rows.shape
    d = w_dn.shape[3]
    half = d // 2
    nb = half // LANES
    d_exp = w_dn.shape[2]
    tf = _tile(d_exp, 256)
    nf = d_exp // tf
    assert tm % nf == 0 and h_all.shape == (h_all.shape[0], LANES) and h_all.shape[0] >= tm * nb and n_out_rows >= tm
    n_exp = w_gu.shape[1]
    b_gu4 = b_gu.reshape(b_gu.shape[0], n_exp, 1, 2 * d_exp)
    b_dn4 = b_dn.reshape(b_dn.shape[0], n_exp, 1, d)

    def f_idx(g, f, nu):
        return jnp.where(jnp.logical_and(g >= 1, g <= nu[0]), f, nf - 1)

    def e_idx(g, be, nu):
        return be[jnp.clip(g - 1, 0, nu[0] - 1)]

    grid_spec = pltpu.PrefetchScalarGridSpec(
        num_scalar_prefetch=2,
        grid=(n_grid, nf),
        in_specs=[
            pl.BlockSpec((None, 1, tm), lambda g, f, be, nu: (g, 0, 0), memory_space=pltpu.SMEM),
            pl.BlockSpec((None, 1, tm), lambda g, f, be, nu: (jnp.maximum(g - 2, 0), 0, 0), memory_space=pltpu.SMEM),
            pl.BlockSpec(memory_space=pl.ANY),
            pl.BlockSpec((None, None, d, tf), lambda g, f, be, nu: (layer, e_idx(g, be, nu), 0, f_idx(g, f, nu))),
            pl.BlockSpec((None, None, d, tf), lambda g, f, be, nu: (layer, e_idx(g, be, nu), 0, nf + f_idx(g, f, nu))),
            pl.BlockSpec((None, None, 1, tf), lambda g, f, be, nu: (layer, e_idx(g, be, nu), 0, f_idx(g, f, nu))),
            pl.BlockSpec((None, None, 1, tf), lambda g, f, be, nu: (layer, e_idx(g, be, nu), 0, nf + f_idx(g, f, nu))),
            pl.BlockSpec((None, None, tf, d), lambda g, f, be, nu: (layer, e_idx(g, be, nu), f_idx(g, f, nu), 0)),
            pl.BlockSpec((None, None, 1, d), lambda g, f, be, nu: (layer, e_idx(g, be, nu), 0, 0)),
        ],
        out_specs=pl.BlockSpec(memory_space=pl.ANY),
        scratch_shapes=[
            pltpu.VMEM((tm * nb, LANES), jnp.uint32),
            pltpu.VMEM((tm, d), BF16),
            pltpu.VMEM((tm, d), F32),
            pltpu.VMEM((tm * nb, LANES), jnp.uint32),
            pltpu.SemaphoreType.DMA(()),
            pltpu.SemaphoreType.DMA(()),
        ],
    )
    return pl.pallas_call(
        _moe_kernel,
        grid_spec=grid_spec,
        out_shape=jax.ShapeDtypeStruct((n_out_rows * nb, LANES), jnp.uint32),
        compiler_params=_params(("arbitrary", "arbitrary")),
        name="moe_expert_ffn",
    )(block_e, n_used, src_rows.reshape(n_grid, 1, tm), dst_rows.reshape(n_grid, 1, tm), h_all,
      w_gu, w_gu, b_gu4, b_gu4, w_dn, b_dn4)


def _route(top_idx, n_exp, tm, n_blocks):
    n_tok, top_k = top_idx.shape
    n_rows = n_tok * top_k
    flat_e = top_idx.reshape(-1)
    onehot = (flat_e[:, None] == jnp.arange(n_exp, dtype=jnp.int32)[None, :]).astype(jnp.int32)
    counts = jnp.sum(onehot, axis=0)
    padded = (counts + tm - 1) // tm * tm
    pad_end = jnp.cumsum(padded)
    n_used = (pad_end[-1] // tm).astype(jnp.int32).reshape(1)
    block_start = jnp.arange(n_blocks, dtype=jnp.int32) * tm
    block_e = jnp.minimum(jnp.sum((pad_end[None, :] <= block_start[:, None]).astype(jnp.int32), axis=1), n_exp - 1)
    n_pos = (n_blocks + 2) * tm
    pos = jnp.arange(n_pos, dtype=jnp.int32)
    pad_cum = jnp.cumsum(padded - counts)
    pad_slot = jnp.arange(n_pos - n_rows, dtype=jnp.int32)
    pad_e = jnp.sum((pad_cum[None, :] <= pad_slot[:, None]).astype(jnp.int32), axis=1)
    keys = jnp.concatenate([flat_e * 2, pad_e * 2 + 1])
    vals = jnp.concatenate([jnp.arange(n_rows, dtype=jnp.int32), jnp.full((n_pos - n_rows,), -1, jnp.int32)])
    _, row_of_pos = lax.sort((keys, vals), num_keys=1, is_stable=True)
    src_rows = jnp.where(row_of_pos >= 0, row_of_pos // top_k, 0).reshape(n_blocks + 2, tm)
    dst_rows = jnp.where(row_of_pos >= 0, (row_of_pos % top_k) * n_tok + row_of_pos // top_k,
                         n_rows + pos).reshape(n_blocks + 2, tm)
    return src_rows, dst_rows, block_e, n_used, n_rows + n_pos


def _ffn_post_kernel(x1_ref, *refs):
    row_refs, (gate_ref, mod_ref, g_ref, o_ref) = refs[:TOP_K], refs[TOP_K:]
    gates = gate_ref[0]
    tm, d = x1_ref.shape[1], x1_ref.shape[2]
    half = d // 2
    nb = half // LANES
    cols = lambda c, hi: slice(hi * half + c * LANES, hi * half + (c + 1) * LANES)
    blocks = {}
    ssq = jnp.zeros((tm, 1), F32)
    for c in range(nb):
        acc_lo = acc_hi = None
        for k in range(TOP_K):
            lo, hi = _unpack_bf16_halves(_load_token_tile_block(row_refs[k], c, tm, nb))
            gk = gates[:, k:k + 1]
            acc_lo = gk * lo if k == 0 else acc_lo + gk * lo
            acc_hi = gk * hi if k == 0 else acc_hi + gk * hi
        blocks[c, 0], blocks[c, 1] = acc_lo, acc_hi
        ssq = ssq + jnp.sum(acc_lo * acc_lo, axis=-1, keepdims=True) + jnp.sum(acc_hi * acc_hi, axis=-1, keepdims=True)
    inv = lax.rsqrt(ssq / d + EPS)
    gt = mod_ref[0][5:6]
    g = g_ref[...]
    for (c, hi), ffn in blocks.items():
        sl = cols(c, hi)
        o_ref[0, :, sl] = x1_ref[0, :, sl] + gt[:, sl] * (ffn * inv * g[:, sl])


def _ffn_post(x1, rows, gates, mod, g, tok_offset, n_tok):
    bsz, seq, d = x1.shape
    tm = _tile(seq, 256)
    assert tok_offset % tm == 0 and n_tok % tm == 0
    per_b = seq // tm
    nb = d // 2 // LANES
    row = pl.BlockSpec((1, tm, d), lambda b, i: (b, i, 0))

    def choice_spec(k):
        blk0 = (k * n_tok + tok_offset) // tm
        return pl.BlockSpec((tm * nb, LANES), lambda b, i: (blk0 + b * per_b + i, 0))

    return pl.pallas_call(
        _ffn_post_kernel,
        grid=(bsz, per_b),
        in_specs=[row] + [choice_spec(k) for k in range(TOP_K)] + [
            pl.BlockSpec((1, tm, LANES), lambda b, i: (b, i, 0)),
            pl.BlockSpec((1, 6, d), lambda b, i: (b, 0, 0)),
            pl.BlockSpec((1, d), lambda b, i: (0, 0)),
        ],
        out_specs=row,
        out_shape=jax.ShapeDtypeStruct((bsz, seq, d), F32),
        compiler_params=_params(("parallel", "parallel")),
        name="ffn_post",
    )(x1, *([rows] * TOP_K), gates, mod, g)


def _rot_half_cols(w, head_dim):
    d = w.shape[0]
    w3 = w.reshape(d, -1, head_dim)
    half = head_dim // 2
    return jnp.concatenate([-w3[..., half:], w3[..., :half]], axis=-1).reshape(d, -1)


def _rope_tables(pos, head_dim):
    half = head_dim // 2
    inv = ROPE_BASE ** (-jnp.arange(half, dtype=F32) / half)
    ang = pos.astype(F32)[:, None] * inv[None, :]
    cos = jnp.cos(ang)
    sin = jnp.sin(ang)
    return jnp.concatenate([cos, cos], axis=1), jnp.concatenate([sin, sin], axis=1)


class _Dims:
    def __init__(self, state_ret, cache_ckv, cache_kpe, lru_w_a, w_uk, w_uv, conv_w, w_router, w_down):
        _, _, self.ret_heads, self.ret_dk, self.ret_dv = state_ret.shape
        self.kv_lora = cache_ckv.shape[-1]
        self.qk_rope = cache_kpe.shape[-1]
        self.mix_a = conv_w.shape[-1]
        self.conv_w = conv_w.shape[1]
        self.mla_heads, self.qk_nope = w_uk.shape[2], w_uk.shape[3]
        self.v_head = w_uv.shape[3]
        self.n_exp = w_router.shape[-1]
        self.mix_b = self.ret_heads * self.ret_dv
        self.ret_qk = self.ret_heads * self.ret_dk


def _layer_weights(l, dm, w_in, conv_w, conv_b, lru_w_a, lru_b_a, lru_w_x, lru_b_x, lru_lam, g_q_norm, w_uq,
                   g_kv_norm, w_uk, w_uv, w_out, w_router, b_router, g_mix_pre, g_mix_post, g_ffn_pre, g_ffn_post):
    d = w_in.shape[1]
    q_lora = g_q_norm.shape[1]
    sizes = (dm.mix_a, dm.mix_a, dm.ret_qk, dm.ret_qk, dm.mix_b, dm.mix_b, q_lora, dm.kv_lora, dm.qk_rope)
    offs = np.concatenate([[0], np.cumsum(sizes)])
    wl = w_in[l]
    seg = [wl[:, offs[i]:offs[i + 1]] for i in range(len(sizes))]
    w_xa, w_ya, w_qr, w_kr, w_vr, w_gr, w_ql, w_kvl, w_kpe = seg
    blk = dm.mix_b
    kv_cols = dm.kv_lora + 2 * dm.qk_rope
    kv_pad = (-kv_cols) % blk
    cols = [w_xa, w_ya, jnp.concatenate([w_qr, w_kr], axis=1), w_vr, w_gr, w_ql,
            jnp.concatenate([w_kvl, w_kpe, _rot_half_cols(w_kpe, dm.qk_rope), jnp.zeros((d, kv_pad), F32)], axis=1),
            jnp.concatenate([_rot_half_cols(w_qr, dm.ret_dk), _rot_half_cols(w_kr, dm.ret_dk)], axis=1)]
    assert all(c.shape[1] == blk for c in cols), [c.shape for c in cols]
    w_ext = jnp.concatenate(cols, axis=1).astype(BF16)

    n_lru = lru_w_a.shape[1]
    eye = jnp.eye(n_lru, dtype=F32)
    block_diag = lambda w: jnp.einsum("nde,nm->ndme", w, eye).reshape(dm.mix_a, dm.mix_a).astype(BF16)

    qk_dim = dm.qk_nope + dm.qk_rope
    wq3 = w_uq[l].reshape(q_lora, dm.mla_heads, qk_dim)
    pe = wq3[..., dm.qk_nope:]
    half = dm.qk_rope // 2
    pe_rot = jnp.concatenate([-pe[..., half:], pe[..., :half]], axis=-1)
    w_uq_ext = jnp.concatenate([wq3, pe_rot], axis=-1).transpose(1, 0, 2).astype(BF16)

    wo = w_out[l].astype(BF16)
    n_exp = dm.n_exp
    wr = jnp.concatenate([w_router[l], jnp.zeros((d, LANES - n_exp), F32)], axis=1)
    wr_hi = wr.astype(BF16)
    wr_lo = (wr - wr_hi.astype(F32)).astype(BF16)
    br = jnp.concatenate([b_router[l], jnp.full((LANES - n_exp,), NEG_BIG, F32)]).reshape(1, LANES)
    row = lambda v: v[l].reshape(1, -1)
    return dict(
        w_ext=w_ext, conv_w=conv_w[l], conv_b=row(conv_b),
        wa_bd=block_diag(lru_w_a[l]), wx_bd=block_diag(lru_w_x[l]),
        ba=row(lru_b_a), bx=row(lru_b_x), lam=row(lru_lam),
        g_q=row(g_q_norm), w_uq_ext=w_uq_ext, g_kv=row(g_kv_norm),
        w_uk=w_uk[l].transpose(1, 0, 2).astype(BF16), w_uv=w_uv[l].transpose(1, 0, 2).astype(BF16),
        wo_a=wo[:dm.mix_a], wo_b=wo[dm.mix_a:dm.mix_a + dm.mix_b], wo_c=wo[dm.mix_a + dm.mix_b:],
        wr_hi=wr_hi, wr_lo=wr_lo, br=br,
        g_mix_pre=row(g_mix_pre), g_mix_post=row(g_mix_post), g_ffn_pre=row(g_ffn_pre), g_ffn_post=row(g_ffn_post),
    )


def _mixer(x, mod, wl, dm, conv_buf, h0, s0, past_ckv, past_kpe):
    bsz, seq, d = x.shape
    past_len = 0 if past_ckv is None else past_ckv.shape[1]
    pos = past_len + jnp.arange(seq, dtype=jnp.int32)
    u = _in_proj(x, mod, wl["g_mix_pre"], wl["w_ext"])

    ctx_rows = 8
    n_ctx = dm.conv_w - 1
    conv_ctx = jnp.concatenate([jnp.zeros((bsz, ctx_rows - n_ctx, dm.mix_a), F32), conv_buf], axis=1)
    a_out, conv_new, h_new = _rg_lru(u, conv_ctx, h0.reshape(bsz, 1, dm.mix_a), wl["conv_w"], wl["conv_b"],
                                     wl["wa_bd"], wl["wx_bd"], wl["ba"], wl["bx"], wl["lam"])

    cos_k, sin_k = _rope_tables(pos, dm.ret_dk)
    k_scale = float(dm.ret_dk) ** -0.5
    scale_row = jnp.concatenate([jnp.ones((dm.ret_qk,), F32), jnp.full((dm.ret_qk,), k_scale, F32)])[None, :]
    cos_t = jnp.tile(cos_k, (1, 2 * dm.ret_heads)) * scale_row
    sin_t = jnp.tile(sin_k, (1, 2 * dm.ret_heads)) * scale_row
    b_out, s_new = _retention(u, s0, cos_t, sin_t, blocks=(2, 3, 4, 7))

    cos_r, sin_r = _rope_tables(pos, dm.qk_rope)
    lane_pad = lambda t, left, right, fill: jnp.concatenate(
        [jnp.full((seq, left), fill, F32), t, jnp.zeros((seq, right), F32)], axis=1)
    ckv_new, kpe_new = _latent(u, wl["g_kv"], lane_pad(cos_r, 0, LANES - dm.qk_rope, 0.0),
                               lane_pad(sin_r, 0, LANES - dm.qk_rope, 0.0), kv_blk=6, blk_w=dm.mix_b, rope_w=dm.qk_rope)
    q = _q_proj(u, wl["g_q"], wl["w_uq_ext"], lane_pad(cos_r, dm.qk_nope, dm.qk_rope, 1.0),
                lane_pad(sin_r, dm.qk_nope, dm.qk_rope, 0.0), ql_blk=5, rope_w=dm.qk_rope)
    if past_ckv is None:
        ckv_all, kpe_all = ckv_new, kpe_new
    else:
        past_kpe_pad = jnp.concatenate([past_kpe, jnp.zeros(past_kpe.shape[:2] + (LANES - dm.qk_rope,), F32)], axis=2)
        ckv_all = jnp.concatenate([past_ckv, ckv_new], axis=1)
        kpe_all = jnp.concatenate([past_kpe_pad, kpe_new], axis=1)
    k, v = _kv_proj(ckv_all, kpe_all, wl["w_uk"], wl["w_uv"])
    c_out = _attention(q, k, v, past_len, dm.qk_nope + dm.qk_rope)

    x1, h2, gates, idx = _out_proj(a_out, b_out, c_out, x, mod, wl["g_mix_post"], wl["g_ffn_pre"],
                                   wl["wo_a"], wl["wo_b"], wl["wo_c"], wl["wr_hi"], wl["wr_lo"], wl["br"])
    states = (ckv_new, kpe_new[:, :, :dm.qk_rope], conv_new[:, ctx_rows - n_ctx:], h_new.reshape(bsz, dm.mix_a), s_new)
    return x1, h2, gates, idx[:, :, :TOP_K], states


def kernel(x_prompt, x_sample, c_prompt, c_sample, cache_ckv, cache_kpe, state_conv, state_lru, state_ret, w_ada, b_ada, g_mix_pre, g_mix_post, g_ffn_pre, g_ffn_post, w_in, conv_w, conv_b, lru_w_a, lru_b_a, lru_w_x, lru_b_x, lru_lam, g_q_norm, w_uq, g_kv_norm, w_uk, w_uv, w_out, w_router, b_router, w_gate_up, b_gate_up, w_down, b_down):
    depth = w_in.shape[0]
    dm = _Dims(state_ret, cache_ckv, cache_kpe, lru_w_a, w_uk, w_uv, conv_w, w_router, w_down)
    bp, lp, d = x_prompt.shape
    bs, ls, _ = x_sample.shape
    n_tok = bp * lp + bs * ls
    moe_tm = min(1024, max(16, 1 << int(np.log2((n_tok * TOP_K) // dm.n_exp))))
    n_blocks = -(-(n_tok * TOP_K) // moe_tm) + dm.n_exp

    xp, xs = x_prompt, x_sample
    p_states, s_states = [], []
    for l in range(depth):
        wl = _layer_weights(l, dm, w_in, conv_w, conv_b, lru_w_a, lru_b_a, lru_w_x, lru_b_x, lru_lam, g_q_norm, w_uq,
                            g_kv_norm, w_uk, w_uv, w_out, w_router, b_router, g_mix_pre, g_mix_post, g_ffn_pre, g_ffn_post)
        mod_p = _ada_mod(c_prompt, w_ada, b_ada, l).reshape(bp, 6, d)
        mod_s = _ada_mod(c_sample, w_ada, b_ada, l).reshape(bs, 6, d)
        zeros = lambda *s: jnp.zeros(s, F32)
        x1p, h2p, gp, ip, st_p = _mixer(
            xp, mod_p, wl, dm, zeros(bp, dm.conv_w - 1, dm.mix_a), zeros(bp, dm.mix_a),
            zeros(bp, dm.ret_heads, dm.ret_dk, dm.ret_dv), None, None)
        x1s, h2s, gs, is_, st_s = _mixer(
            xs, mod_s, wl, dm, state_conv[l], state_lru[l], state_ret[l], cache_ckv[l], cache_kpe[l])
        p_states.append(st_p)
        s_states.append(st_s)

        h2 = jnp.concatenate([h2p.reshape(-1, LANES), h2s.reshape(-1, LANES)], axis=0)
        top_idx = jnp.concatenate([ip.reshape(-1, TOP_K), is_.reshape(-1, TOP_K)], axis=0)
        src_rows, dst_rows, block_e, n_used, n_out_rows = _route(top_idx, dm.n_exp, moe_tm, n_blocks)
        rows = _moe_ffn(h2, src_rows, dst_rows, block_e, n_used, w_gate_up, b_gate_up, w_down, b_down, l, n_out_rows)
        xp = _ffn_post(x1p, rows, gp, mod_p, wl["g_ffn_post"], 0, n_tok)
        xs = _ffn_post(x1s, rows, gs, mod_s, wl["g_ffn_post"], bp * lp, n_tok)

    stack = lambda sts: tuple(jnp.stack(t, axis=0) for t in zip(*sts))
    p_ckv, p_kpe, p_conv, p_lru, p_ret = stack(p_states)
    s_ckv, s_kpe, s_conv, s_lru, s_ret = stack(s_states)
    return (xp, xs, p_ckv, p_kpe, p_conv, p_lru, p_ret, s_ckv, s_kpe, s_conv, s_lru, s_ret)
```
